```python
import math
import jax, jax.numpy as jnp
from jax import lax
import numpy as np

D_MODEL = 1024
BATCH = 8
SEQ = 4096
DEPTH = 1

MIX_WIDTH = D_MODEL
CONV_WIDTH = MIX_WIDTH // 2
CONV_GROUPS = 8
CONV_KERNEL = 31
DIFF_WIDTH = MIX_WIDTH - CONV_WIDTH
DIFF_HEADS = 4
DIFF_VDIM = DIFF_WIDTH // DIFF_HEADS
DIFF_QKDIM = DIFF_VDIM // 2
Q_BLOCK = 128
RMS_EPS = 1e-6
LN_EPS = 1e-5
COL_GLU = 2 * CONV_WIDTH
COL_CGATE = CONV_WIDTH
COL_Q = DIFF_WIDTH
COL_K = DIFF_WIDTH
COL_V = DIFF_WIDTH
COL_DGATE = DIFF_WIDTH
IN_COLS = COL_GLU + COL_CGATE + COL_Q + COL_K + COL_V + COL_DGATE

kernel_name = "hymba_conformer_diffattn_alibi_layer"


def rmsnorm(x, g, eps=RMS_EPS):
    xf = x.astype(jnp.float32)
    y = xf * lax.rsqrt(jnp.mean(xf * xf, axis=-1, keepdims=True) + eps)
    return (y * g.astype(jnp.float32)).astype(x.dtype)


def layernorm(x, g, b, eps=LN_EPS):
    xf = x.astype(jnp.float32)
    mu = jnp.mean(xf, axis=-1, keepdims=True)
    var = jnp.mean(jnp.square(xf - mu), axis=-1, keepdims=True)
    y = (xf - mu) * lax.rsqrt(var + eps)
    return (y * g.astype(jnp.float32) + b.astype(jnp.float32)).astype(x.dtype)


def lambda_init_for(layer_idx):
    return 0.8 - 0.6 * math.exp(-0.3 * layer_idx)


def alibi_slopes(n_heads):
    return jnp.exp2(-8.0 * (jnp.arange(n_heads, dtype=jnp.float32) + 1.0) / n_heads)


def conformer_conv_branch(u, dw_w, dw_b, ln_g, ln_b, pw_w, pw_b):
    a, b = jnp.split(u, 2, axis=-1)
    h = a * jax.nn.sigmoid(b)
    c = h.shape[-1]
    rhs = dw_w[:, None, :]
    h = lax.conv_general_dilated(
        h, rhs.astype(h.dtype), window_strides=(1,),
        padding=[(CONV_KERNEL - 1, 0)],
        dimension_numbers=("NWC", "WIO", "NWC"),
        feature_group_count=c) + dw_b
    h = layernorm(h, ln_g, ln_b)
    h = jax.nn.silu(h)
    return jnp.einsum("bsc,ce->bse", h, pw_w) + pw_b


def diff_attention(q, k, v, lam, slopes):
    B, S, H = q.shape[0], q.shape[1], q.shape[2]
    nblk = S // Q_BLOCK
    scale = DIFF_QKDIM ** -0.5
    qb = q.reshape(B, nblk, Q_BLOCK, H, 2, DIFF_QKDIM).transpose(1, 0, 2, 3, 4, 5)
    kpos = jnp.arange(S)

    def one_block(args):
        qi, blk = args
        qpos = blk * Q_BLOCK + jnp.arange(Q_BLOCK)
        dist = (qpos[:, None] - kpos[None, :]).astype(jnp.float32)
        bias = jnp.where(dist[None] >= 0, -slopes[:, None, None] * dist[None], -jnp.inf)
        s = jnp.einsum("bqhcd,bkhcd->bhcqk", qi, k).astype(jnp.float32) * scale
        p = jax.nn.softmax(s + bias[None, :, None], axis=-1)
        a = p[:, :, 0] - lam * p[:, :, 1]
        return jnp.einsum("bhqk,bkhe->bqhe", a.astype(v.dtype), v)

    out = lax.map(one_block, (qb, jnp.arange(nblk)))
    return out.transpose(1, 0, 2, 3, 4).reshape(B, S, H, DIFF_VDIM)


def setup_inputs(seed: int = 0) -> dict:
    key = jax.random.key(seed)
    ks = jax.random.split(key, 20)
    f32 = jnp.float32
    L = DEPTH
    nrm = lambda k, shape, s: (jax.random.normal(k, shape, f32) * s)
    return {
        "x": jax.random.normal(ks[0], (BATCH, SEQ, D_MODEL), f32),
        "pre_norm_g": 1.0 + nrm(ks[1], (L, D_MODEL), 0.02),
        "w_in": nrm(ks[2], (L, D_MODEL, IN_COLS), D_MODEL ** -0.5),
        "conv_dw_w": nrm(ks[3], (L, CONV_KERNEL, CONV_WIDTH), CONV_KERNEL ** -0.5),
        "conv_dw_b": nrm(ks[4], (L, CONV_WIDTH), 0.02),
        "conv_ln_g": 1.0 + nrm(ks[5], (L, CONV_WIDTH), 0.02),
        "conv_ln_b": nrm(ks[6], (L, CONV_WIDTH), 0.02),
        "conv_pw_w": nrm(ks[7], (L, CONV_WIDTH, CONV_WIDTH), CONV_WIDTH ** -0.5),
        "conv_pw_b": nrm(ks[8], (L, CONV_WIDTH), 0.02),
        "lambda_q1": nrm(ks[9], (L, DIFF_QKDIM), 0.1),
        "lambda_k1": nrm(ks[10], (L, DIFF_QKDIM), 0.1),
        "lambda_q2": nrm(ks[11], (L, DIFF_QKDIM), 0.1),
        "lambda_k2": nrm(ks[12], (L, DIFF_QKDIM), 0.1),
        "diff_subln_g": 1.0 + nrm(ks[13], (L, DIFF_VDIM), 0.02),
        "w_out": nrm(ks[14], (L, MIX_WIDTH, D_MODEL), MIX_WIDTH ** -0.5),
        "post_norm_g": 1.0 + nrm(ks[15], (L, D_MODEL), 0.02),
    }


def reference(x, pre_norm_g, w_in, conv_dw_w, conv_dw_b, conv_ln_g, conv_ln_b,
              conv_pw_w, conv_pw_b, lambda_q1, lambda_k1, lambda_q2, lambda_k2,
              diff_subln_g, w_out, post_norm_g):
    B, S, _ = x.shape
    slopes = alibi_slopes(DIFF_HEADS)
    split_at = np.cumsum([COL_GLU, COL_CGATE, COL_Q, COL_K, COL_V])
    h = x
    for i in range(DEPTH):
        lam_init = lambda_init_for(i)
        xn = rmsnorm(h, pre_norm_g[i])
        z = jnp.einsum("bsd,dc->bsc", xn, w_in[i])
        u_glu, g_conv, q, k, v, g_diff = jnp.split(z, split_at, axis=-1)

        y_conv = conformer_conv_branch(u_glu, conv_dw_w[i], conv_dw_b[i], conv_ln_g[i],
                                       conv_ln_b[i], conv_pw_w[i], conv_pw_b[i])
        y_conv = y_conv * jax.nn.silu(g_conv)

        qh = q.reshape(B, S, DIFF_HEADS, 2, DIFF_QKDIM)
        kh = k.reshape(B, S, DIFF_HEADS, 2, DIFF_QKDIM)
        vh = v.reshape(B, S, DIFF_HEADS, DIFF_VDIM)
        lq1 = lambda_q1[i].astype(jnp.float32); lk1 = lambda_k1[i].astype(jnp.float32)
        lq2 = lambda_q2[i].astype(jnp.float32); lk2 = lambda_k2[i].astype(jnp.float32)
        lam = jnp.exp(jnp.sum(lq1 * lk1)) - jnp.exp(jnp.sum(lq2 * lk2)) + lam_init
        o = diff_attention(qh, kh, vh, lam, slopes)
        o = rmsnorm(o, diff_subln_g[i]) * (1.0 - lam_init)
        y_diff = o.reshape(B, S, DIFF_WIDTH) * jax.nn.silu(g_diff)

        y = jnp.concatenate([y_conv, y_diff], axis=-1)
        y = jnp.einsum("bsc,cd->bsd", y, w_out[i])
        h = h + rmsnorm(y, post_norm_g[i])
    return h
```

```python
import functools
import math

import jax
import jax.numpy as jnp
from jax import lax
from jax.experimental import pallas as pl
from jax.experimental.pallas import tpu as pltpu

D_MODEL = 1024
CONV_WIDTH = 512
CONV_KERNEL = 31
DIFF_WIDTH = 512
DIFF_HEADS = 4
DIFF_VDIM = 128
DIFF_QKDIM = 64
RMS_EPS = 1e-6
LN_EPS = 1e-5


def _lambda_init(layer_idx):
    return 0.8 - 0.6 * math.exp(-0.3 * layer_idx)


LOG2E = math.log2(math.e)
Q_PRESCALE = (DIFF_QKDIM ** -0.5) * LOG2E

C_GLU = 0
C_CGATE = 1024
C_Q = 1536
C_K = 2048
C_V = 2560
C_DGATE = 3072

SEQ_TILE = 256
CONV_HALO = 32
CONV_CHUNK = 32
ATT_TQ = 512
ATT_TK = 512
VMEM_LIMIT = 48 * 1024 * 1024

_NT = (((1,), (1,)), ((), ()))


def _silu(x):
    return x * jax.nn.sigmoid(x)


def _proj_conv_kernel(x_ref, g_ref, w_ref, wvt_ref, dww_ref, dwb_ref, lng_ref, lnb_ref, pww_ref, pwb_ref,
                      yc_ref, q_ref, k_ref, vt_ref, gd_ref, hbuf, cbuf):
    ts = x_ref.shape[0]
    s_idx = pl.program_id(1)

    x = x_ref[...]
    ms = jnp.mean(x * x, axis=-1, keepdims=True)
    xb = (x * lax.rsqrt(ms + RMS_EPS) * g_ref[...]).astype(jnp.bfloat16)

    def proj(c0, width):
        return jnp.dot(xb, w_ref[:, c0:c0 + width], preferred_element_type=jnp.float32)

    q_ref[...] = (proj(C_Q, DIFF_WIDTH) * Q_PRESCALE).astype(jnp.bfloat16)
    k_ref[...] = proj(C_K, DIFF_WIDTH).astype(jnp.bfloat16)
    vt_ref[...] = lax.dot_general(wvt_ref[...], xb, _NT,
                                  preferred_element_type=jnp.float32).astype(jnp.bfloat16)
    gd_ref[...] = _silu(proj(C_DGATE, DIFF_WIDTH))

    a = proj(C_GLU, CONV_WIDTH)
    b = proj(C_GLU + CONV_WIDTH, CONV_WIDTH)

    @pl.when(s_idx == 0)
    def _():
        hbuf[0:CONV_HALO, :] = jnp.zeros((CONV_HALO, CONV_WIDTH), jnp.float32)

    hbuf[CONV_HALO:CONV_HALO + ts, :] = a * jax.nn.sigmoid(b)

    off = CONV_HALO - (CONV_KERNEL - 1)
    for c in range(ts // CONV_CHUNK):
        r0 = c * CONV_CHUNK
        acc = jnp.broadcast_to(dwb_ref[...], (CONV_CHUNK, CONV_WIDTH))
        for tau in range(CONV_KERNEL):
            acc = acc + dww_ref[tau:tau + 1, :] * hbuf[r0 + off + tau:r0 + off + tau + CONV_CHUNK, :]
        cbuf[r0:r0 + CONV_CHUNK, :] = acc

    hbuf[0:CONV_HALO, :] = hbuf[ts:ts + CONV_HALO, :]

    cv = cbuf[...]
    mu = jnp.mean(cv, axis=-1, keepdims=True)
    d = cv - mu
    var = jnp.mean(d * d, axis=-1, keepdims=True)
    hn = d * lax.rsqrt(var + LN_EPS) * lng_ref[...] + lnb_ref[...]
    hs = _silu(hn).astype(jnp.bfloat16)
    yc = jnp.dot(hs, pww_ref[...], preferred_element_type=jnp.float32) + pwb_ref[...]
    yc_ref[...] = (yc * _silu(proj(C_CGATE, CONV_WIDTH))).astype(jnp.bfloat16)


def _proj_conv(x, pre_g, w_in_b, w_vt_b, dw_w, dw_b, ln_g, ln_b, pw_w_b, pw_b):
    B, S, D = x.shape
    ts = SEQ_TILE
    grid = (B, S // ts)
    row = lambda b, s: (b, s, 0)
    const2 = lambda b, s: (0, 0)
    in_specs = [
        pl.BlockSpec((None, ts, D), row),
        pl.BlockSpec((1, D), const2),
        pl.BlockSpec(w_in_b.shape, const2),
        pl.BlockSpec(w_vt_b.shape, const2),
        pl.BlockSpec(dw_w.shape, const2),
        pl.BlockSpec((1, CONV_WIDTH), const2),
        pl.BlockSpec((1, CONV_WIDTH), const2),
        pl.BlockSpec((1, CONV_WIDTH), const2),
        pl.BlockSpec(pw_w_b.shape, const2),
        pl.BlockSpec((1, CONV_WIDTH), const2),
    ]
    out_specs = [
        pl.BlockSpec((None, ts, CONV_WIDTH), row),
        pl.BlockSpec((None, ts, DIFF_WIDTH), row),
        pl.BlockSpec((None, ts, DIFF_WIDTH), row),
        pl.BlockSpec((None, DIFF_WIDTH, ts), lambda b, s: (b, 0, s)),
        pl.BlockSpec((None, ts, DIFF_WIDTH), row),
    ]
    out_shape = [
        jax.ShapeDtypeStruct((B, S, CONV_WIDTH), jnp.bfloat16),
        jax.ShapeDtypeStruct((B, S, DIFF_WIDTH), jnp.bfloat16),
        jax.ShapeDtypeStruct((B, S, DIFF_WIDTH), jnp.bfloat16),
        jax.ShapeDtypeStruct((B, DIFF_WIDTH, S), jnp.bfloat16),
        jax.ShapeDtypeStruct((B, S, DIFF_WIDTH), jnp.float32),
    ]
    return pl.pallas_call(
        _proj_conv_kernel,
        grid=grid,
        in_specs=in_specs,
        out_specs=out_specs,
        out_shape=out_shape,
        scratch_shapes=[
            pltpu.VMEM((CONV_HALO + ts, CONV_WIDTH), jnp.float32),
            pltpu.VMEM((ts, CONV_WIDTH), jnp.float32),
        ],
        compiler_params=pltpu.CompilerParams(
            dimension_semantics=("arbitrary", "arbitrary"),
            vmem_limit_bytes=VMEM_LIMIT),
        name="proj_conv",
    )(x, pre_g, w_in_b, w_vt_b, dw_w, dw_b, ln_g, ln_b, pw_w_b, pw_b)


def _attn_kernel(q_ref, k_ref, vt_ref, gd_ref, lq1_ref, lk1_ref, lq2_ref, lk2_ref, sg_ref,
                 o_ref, acc_ref, m_ref, l_ref, bias_ref, *, lam_init):
    tq = q_ref.shape[0]
    tk = ATT_TK
    h = pl.program_id(1)
    qi = pl.program_id(2)

    hv = jnp.full((1, 1), h, jnp.int32).astype(jnp.float32)
    slope = jnp.exp2(-(8.0 / DIFF_HEADS) * (hv + 1.0)) * LOG2E

    @pl.when(qi == 0)
    def _():
        jj = lax.broadcasted_iota(jnp.int32, (tk, tq), 0)
        ii = lax.broadcasted_iota(jnp.int32, (tk, tq), 1)
        rel = (jj - ii).astype(jnp.float32) * slope
        bias_ref[0] = rel
        bias_ref[1] = jnp.where(jj > ii, -jnp.inf, rel)

    q = q_ref[...]
    lane = lax.broadcasted_iota(jnp.int32, q.shape, 1)
    zero = jnp.zeros_like(q)
    qz = (jnp.where(lane < DIFF_QKDIM, q, zero), jnp.where(lane >= DIFF_QKDIM, q, zero))

    m_ref[...] = jnp.full(m_ref.shape, -jnp.inf, jnp.float32)
    l_ref[...] = jnp.zeros(l_ref.shape, jnp.float32)
    acc_ref[...] = jnp.zeros(acc_ref.shape, jnp.float32)

    def block(j, bias_idx):
        k0 = pl.multiple_of(j * tk, tk)
        kblk = k_ref[pl.ds(k0, tk), :]
        vtblk = vt_ref[:, pl.ds(k0, tk)]
        bias = bias_ref[bias_idx]
        cblk = slope * (j * tk - qi * tq).astype(jnp.float32)
        for c in range(2):
            t = lax.dot_general(kblk, qz[c], _NT, preferred_element_type=jnp.float32) + bias
            m_old = m_ref[c]
            m_new = jnp.maximum(m_old, jnp.max(t, axis=0, keepdims=True) + cblk)
            p = jnp.exp2(t - (m_new - cblk))
            alpha = jnp.exp2(m_old - m_new)
            l_ref[c] = alpha * l_ref[c] + jnp.sum(p, axis=0, keepdims=True)
            acc_ref[c] = alpha * acc_ref[c] + jnp.dot(vtblk, p.astype(jnp.bfloat16),
                                                      preferred_element_type=jnp.float32)
            m_ref[c] = m_new

    def body(j, carry):
        block(j, 0)
        return carry

    lax.fori_loop(0, qi, body, 0)
    block(qi, 1)

    lam = (jnp.exp(jnp.sum(lq1_ref[...] * lk1_ref[...], axis=-1, keepdims=True))
           - jnp.exp(jnp.sum(lq2_ref[...] * lk2_ref[...], axis=-1, keepdims=True)) + lam_init)
    o = acc_ref[0] / l_ref[0] - lam * (acc_ref[1] / l_ref[1])
    ms = jnp.mean(o * o, axis=0, keepdims=True)
    on = (o * lax.rsqrt(ms + RMS_EPS)).T
    y = on * (sg_ref[...] * (1.0 - lam_init)) * gd_ref[...]
    o_ref[...] = y.astype(jnp.bfloat16)


def _attention(q, k, vt, gd, lq1, lk1, lq2, lk2, sg, lam_init):
    B, S, _ = q.shape
    tq = ATT_TQ
    grid = (B, DIFF_HEADS, S // tq)
    vec = lambda b, h, i: (0, 0)
    in_specs = [
        pl.BlockSpec((None, tq, DIFF_VDIM), lambda b, h, i: (b, i, h)),
        pl.BlockSpec((None, S, DIFF_VDIM), lambda b, h, i: (b, 0, h)),
        pl.BlockSpec((None, DIFF_VDIM, S), lambda b, h, i: (b, h, 0)),
        pl.BlockSpec((None, tq, DIFF_VDIM), lambda b, h, i: (b, i, h)),
        pl.BlockSpec((1, DIFF_QKDIM), vec),
        pl.BlockSpec((1, DIFF_QKDIM), vec),
        pl.BlockSpec((1, DIFF_QKDIM), vec),
        pl.BlockSpec((1, DIFF_QKDIM), vec),
        pl.BlockSpec((1, DIFF_VDIM), vec),
    ]
    return pl.pallas_call(
        functools.partial(_attn_kernel, lam_init=lam_init),
        grid=grid,
        in_specs=in_specs,
        out_specs=pl.BlockSpec((None, tq, DIFF_VDIM), lambda b, h, i: (b, i, h)),
        out_shape=jax.ShapeDtypeStruct((B, S, DIFF_WIDTH), jnp.bfloat16),
        scratch_shapes=[
            pltpu.VMEM((2, DIFF_VDIM, tq), jnp.float32),
            pltpu.VMEM((2, 1, tq), jnp.float32),
            pltpu.VMEM((2, 1, tq), jnp.float32),
            pltpu.VMEM((2, ATT_TK, tq), jnp.float32),
        ],
        compiler_params=pltpu.CompilerParams(
            dimension_semantics=("arbitrary", "arbitrary", "arbitrary"),
            vmem_limit_bytes=VMEM_LIMIT),
        name="diff_attn",
    )(q, k, vt, gd, lq1, lk1, lq2, lk2, sg)


def _out_kernel(x_ref, yc_ref, yd_ref, w_ref, g_ref, o_ref):
    y = (jnp.dot(yc_ref[...], w_ref[0:CONV_WIDTH, :], preferred_element_type=jnp.float32)
         + jnp.dot(yd_ref[...], w_ref[CONV_WIDTH:, :], preferred_element_type=jnp.float32))
    ms = jnp.mean(y * y, axis=-1, keepdims=True)
    o_ref[...] = x_ref[...] + y * lax.rsqrt(ms + RMS_EPS) * g_ref[...]


def _out_proj(x, yc, yd, w_out_b, post_g):
    B, S, D = x.shape
    ts = SEQ_TILE
    row = lambda b, s: (b, s, 0)
    const2 = lambda b, s: (0, 0)
    return pl.pallas_call(
        _out_kernel,
        grid=(B, S // ts),
        in_specs=[
            pl.BlockSpec((None, ts, D), row),
            pl.BlockSpec((None, ts, CONV_WIDTH), row),
            pl.BlockSpec((None, ts, DIFF_WIDTH), row),
            pl.BlockSpec(w_out_b.shape, const2),
            pl.BlockSpec((1, D), const2),
        ],
        out_specs=pl.BlockSpec((None, ts, D), row),
        out_shape=jax.ShapeDtypeStruct((B, S, D), jnp.float32),
        compiler_params=pltpu.CompilerParams(
            dimension_semantics=("arbitrary", "arbitrary"),
            vmem_limit_bytes=VMEM_LIMIT),
        name="out_proj",
    )(x, yc, yd, w_out_b, post_g)


def kernel(x, pre_norm_g, w_in, conv_dw_w, conv_dw_b, conv_ln_g, conv_ln_b, conv_pw_w, conv_pw_b,
           lambda_q1, lambda_k1, lambda_q2, lambda_k2, diff_subln_g, w_out, post_norm_g):
    depth = pre_norm_g.shape[0]
    h = x
    for i in range(depth):
        w_in_b = w_in[i].astype(jnp.bfloat16)
        w_vt_b = w_in[i][:, C_V:C_V + DIFF_WIDTH].T.astype(jnp.bfloat16)
        yc, q, k, vt, gd = _proj_conv(
            h, pre_norm_g[i][None], w_in_b, w_vt_b, conv_dw_w[i], conv_dw_b[i][None],
            conv_ln_g[i][None], conv_ln_b[i][None], conv_pw_w[i].astype(jnp.bfloat16), conv_pw_b[i][None])
        yd = _attention(q, k, vt, gd, lambda_q1[i][None], lambda_k1[i][None], lambda_q2[i][None],
                        lambda_k2[i][None], diff_subln_g[i][None], _lambda_init(i))
        h = _out_proj(h, yc, yd, w_out[i].astype(jnp.bfloat16), post_norm_g[i][None])
    return h
```

```python
import functools
import math

import jax
import jax.numpy as jnp
from jax import lax
from jax.experimental import pallas as pl
from jax.experimental.pallas import tpu as pltpu

D_MODEL = 1024
CONV_WIDTH = 512
CONV_KERNEL = 31
DIFF_WIDTH = 512
DIFF_HEADS = 4
DIFF_VDIM = 128
DIFF_QKDIM = 64
RMS_EPS = 1e-6
LN_EPS = 1e-5


def _lambda_init(layer_idx):
    return 0.8 - 0.6 * math.exp(-0.3 * layer_idx)


LOG2E = math.log2(math.e)
Q_PRESCALE = (DIFF_QKDIM ** -0.5) * LOG2E

C_GLU = 0
C_CGATE = 1024
C_Q = 1536
C_K = 2048
C_V = 2560
C_DGATE = 3072

SEQ_TILE = 256
CONV_HALO = 32
CONV_CHUNK = 32
ATT_TQ = 512
ATT_TK = 512
VMEM_LIMIT = 48 * 1024 * 1024

_NT = (((1,), (1,)), ((), ()))


def _silu(x):
    return x * jax.nn.sigmoid(x)


def _proj_conv_kernel(x_ref, g_ref, w_ref, wvt_ref, dww_ref, dwb_ref, lng_ref, lnb_ref, pww_ref, pwb_ref,
                      yc_ref, q_ref, k_ref, vt_ref, gd_ref, hbuf, cbuf):
    ts = x_ref.shape[0]
    s_idx = pl.program_id(1)

    x = x_ref[...]
    ms = jnp.mean(x * x, axis=-1, keepdims=True)
    xb = (x * lax.rsqrt(ms + RMS_EPS) * g_ref[...]).astype(jnp.bfloat16)

    def proj(c0, width):
        return jnp.dot(xb, w_ref[:, c0:c0 + width], preferred_element_type=jnp.float32)

    q_ref[...] = (proj(C_Q, DIFF_WIDTH) * Q_PRESCALE).astype(jnp.bfloat16)
    k_ref[...] = proj(C_K, DIFF_WIDTH).astype(jnp.bfloat16)
    vt_ref[...] = lax.dot_general(wvt_ref[...], xb, _NT,
                                  preferred_element_type=jnp.float32).astype(jnp.bfloat16)
    gd_ref[...] = _silu(proj(C_DGATE, DIFF_WIDTH))

    a = proj(C_GLU, CONV_WIDTH)
    b = proj(C_GLU + CONV_WIDTH, CONV_WIDTH)

    @pl.when(s_idx == 0)
    def _():
        hbuf[0:CONV_HALO, :] = jnp.zeros((CONV_HALO, CONV_WIDTH), jnp.float32)

    hbuf[CONV_HALO:CONV_HALO + ts, :] = a * jax.nn.sigmoid(b)

    off = CONV_HALO - (CONV_KERNEL - 1)
    for c in range(ts // CONV_CHUNK):
        r0 = c * CONV_CHUNK
        acc = jnp.broadcast_to(dwb_ref[...], (CONV_CHUNK, CONV_WIDTH))
        for tau in range(CONV_KERNEL):
            acc = acc + dww_ref[tau:tau + 1, :] * hbuf[r0 + off + tau:r0 + off + tau + CONV_CHUNK, :]
        cbuf[r0:r0 + CONV_CHUNK, :] = acc

    hbuf[0:CONV_HALO, :] = hbuf[ts:ts + CONV_HALO, :]

    cv = cbuf[...]
    mu = jnp.mean(cv, axis=-1, keepdims=True)
    d = cv - mu
    var = jnp.mean(d * d, axis=-1, keepdims=True)
    hn = d * lax.rsqrt(var + LN_EPS) * lng_ref[...] + lnb_ref[...]
    hs = _silu(hn).astype(jnp.bfloat16)
    yc = jnp.dot(hs, pww_ref[...], preferred_element_type=jnp.float32) + pwb_ref[...]
    yc_ref[...] = (yc * _silu(proj(C_CGATE, CONV_WIDTH))).astype(jnp.bfloat16)


def _proj_conv(x, pre_g, w_in_b, w_vt_b, dw_w, dw_b, ln_g, ln_b, pw_w_b, pw_b):
    B, S, D = x.shape
    ts = SEQ_TILE
    grid = (B, S // ts)
    row = lambda b, s: (b, s, 0)
    const2 = lambda b, s: (0, 0)
    in_specs = [
        pl.BlockSpec((None, ts, D), row),
        pl.BlockSpec((1, D), const2),
        pl.BlockSpec(w_in_b.shape, const2),
        pl.BlockSpec(w_vt_b.shape, const2),
        pl.BlockSpec(dw_w.shape, const2),
        pl.BlockSpec((1, CONV_WIDTH), const2),
        pl.BlockSpec((1, CONV_WIDTH), const2),
        pl.BlockSpec((1, CONV_WIDTH), const2),
        pl.BlockSpec(pw_w_b.shape, const2),
        pl.BlockSpec((1, CONV_WIDTH), const2),
    ]
    out_specs = [
        pl.BlockSpec((None, ts, CONV_WIDTH), row),
        pl.BlockSpec((None, ts, DIFF_WIDTH), row),
        pl.BlockSpec((None, ts, DIFF_WIDTH), row),
        pl.BlockSpec((None, DIFF_WIDTH, ts), lambda b, s: (b, 0, s)),
        pl.BlockSpec((None, ts, DIFF_WIDTH), row),
    ]
    out_shape = [
        jax.ShapeDtypeStruct((B, S, CONV_WIDTH), jnp.bfloat16),
        jax.ShapeDtypeStruct((B, S, DIFF_WIDTH), jnp.bfloat16),
        jax.ShapeDtypeStruct((B, S, DIFF_WIDTH), jnp.bfloat16),
        jax.ShapeDtypeStruct((B, DIFF_WIDTH, S), jnp.bfloat16),
        jax.ShapeDtypeStruct((B, S, DIFF_WIDTH), jnp.float32),
    ]
    return pl.pallas_call(
        _proj_conv_kernel,
        grid=grid,
        in_specs=in_specs,
        out_specs=out_specs,
        out_shape=out_shape,
        scratch_shapes=[
            pltpu.VMEM((CONV_HALO + ts, CONV_WIDTH), jnp.float32),
            pltpu.VMEM((ts, CONV_WIDTH), jnp.float32),
        ],
        compiler_params=pltpu.CompilerParams(
            dimension_semantics=("arbitrary", "arbitrary"),
            vmem_limit_bytes=VMEM_LIMIT),
        name="proj_conv",
    )(x, pre_g, w_in_b, w_vt_b, dw_w, dw_b, ln_g, ln_b, pw_w_b, pw_b)


def _attn_kernel(q_ref, k_ref, vt_ref, gd_ref, lq1_ref, lk1_ref, lq2_ref, lk2_ref, sg_ref,
                 o_ref, acc_ref, m_ref, l_ref, bias_ref, t_ref, tmax_ref, *, lam_init):
    tq = q_ref.shape[0]
    tk = ATT_TK
    h = pl.program_id(1)
    qi = pl.program_id(2)

    hv = jnp.full((1, 1), h, jnp.int32).astype(jnp.float32)
    slope = jnp.exp2(-(8.0 / DIFF_HEADS) * (hv + 1.0)) * LOG2E

    @pl.when(qi == 0)
    def _():
        jj = lax.broadcasted_iota(jnp.int32, (tk, tq), 0)
        ii = lax.broadcasted_iota(jnp.int32, (tk, tq), 1)
        rel = (jj - ii).astype(jnp.float32) * slope
        bias_ref[0] = rel
        bias_ref[1] = jnp.where(jj > ii, -jnp.inf, rel)

    q = q_ref[...]
    lane = lax.broadcasted_iota(jnp.int32, q.shape, 1)
    zero = jnp.zeros_like(q)
    qz = (jnp.where(lane < DIFF_QKDIM, q, zero), jnp.where(lane >= DIFF_QKDIM, q, zero))

    m_ref[...] = jnp.full(m_ref.shape, -jnp.inf, jnp.float32)
    l_ref[...] = jnp.zeros(l_ref.shape, jnp.float32)
    acc_ref[...] = jnp.zeros(acc_ref.shape, jnp.float32)

    def scores(c, j, bias_idx):
        k0 = pl.multiple_of(j * tk, tk)
        t = lax.dot_general(k_ref[pl.ds(k0, tk), :], qz[c], _NT,
                            preferred_element_type=jnp.float32) + bias_ref[bias_idx]
        t_ref[c] = t
        tmax_ref[c] = jnp.max(t, axis=0, keepdims=True)

    def softmax_pv(c, j):
        k0 = pl.multiple_of(j * tk, tk)
        cblk = slope * (j * tk - qi * tq).astype(jnp.float32)
        m_old = m_ref[c]
        m_new = jnp.maximum(m_old, tmax_ref[c] + cblk)
        p = jnp.exp2(t_ref[c] - (m_new - cblk))
        alpha = jnp.exp2(m_old - m_new)
        l_ref[c] = alpha * l_ref[c] + jnp.sum(p, axis=0, keepdims=True)
        acc_ref[c] = alpha * acc_ref[c] + jnp.dot(vt_ref[:, pl.ds(k0, tk)], p.astype(jnp.bfloat16),
                                                  preferred_element_type=jnp.float32)
        m_ref[c] = m_new

    scores(0, 0, (qi == 0).astype(jnp.int32))

    def body(j, carry):
        scores(1, j, 0)
        softmax_pv(0, j)
        scores(0, j + 1, (j + 1 == qi).astype(jnp.int32))
        softmax_pv(1, j)
        return carry

    lax.fori_loop(0, qi, body, 0)
    scores(1, qi, 1)
    softmax_pv(0, qi)
    softmax_pv(1, qi)

    lam = (jnp.exp(jnp.sum(lq1_ref[...] * lk1_ref[...], axis=-1, keepdims=True))
           - jnp.exp(jnp.sum(lq2_ref[...] * lk2_ref[...], axis=-1, keepdims=True)) + lam_init)
    o = acc_ref[0] / l_ref[0] - lam * (acc_ref[1] / l_ref[1])
    ms = jnp.mean(o * o, axis=0, keepdims=True)
    on = (o * lax.rsqrt(ms + RMS_EPS)).T
    y = on * (sg_ref[...] * (1.0 - lam_init)) * gd_ref[...]
    o_ref[...] = y.astype(jnp.bfloat16)


def _attention(q, k, vt, gd, lq1, lk1, lq2, lk2, sg, lam_init):
    B, S, _ = q.shape
    tq = ATT_TQ
    grid = (B, DIFF_HEADS, S // tq)
    vec = lambda b, h, i: (0, 0)
    in_specs = [
        pl.BlockSpec((None, tq, DIFF_VDIM), lambda b, h, i: (b, i, h)),
        pl.BlockSpec((None, S, DIFF_VDIM), lambda b, h, i: (b, 0, h)),
        pl.BlockSpec((None, DIFF_VDIM, S), lambda b, h, i: (b, h, 0)),
        pl.BlockSpec((None, tq, DIFF_VDIM), lambda b, h, i: (b, i, h)),
        pl.BlockSpec((1, DIFF_QKDIM), vec),
        pl.BlockSpec((1, DIFF_QKDIM), vec),
        pl.BlockSpec((1, DIFF_QKDIM), vec),
        pl.BlockSpec((1, DIFF_QKDIM), vec),
        pl.BlockSpec((1, DIFF_VDIM), vec),
    ]
    return pl.pallas_call(
        functools.partial(_attn_kernel, lam_init=lam_init),
        grid=grid,
        in_specs=in_specs,
        out_specs=pl.BlockSpec((None, tq, DIFF_VDIM), lambda b, h, i: (b, i, h)),
        out_shape=jax.ShapeDtypeStruct((B, S, DIFF_WIDTH), jnp.bfloat16),
        scratch_shapes=[
            pltpu.VMEM((2, DIFF_VDIM, tq), jnp.float32),
            pltpu.VMEM((2, 1, tq), jnp.float32),
            pltpu.VMEM((2, 1, tq), jnp.float32),
            pltpu.VMEM((2, ATT_TK, tq), jnp.float32),
            pltpu.VMEM((2, ATT_TK, tq), jnp.float32),
            pltpu.VMEM((2, 1, tq), jnp.float32),
        ],
        compiler_params=pltpu.CompilerParams(
            dimension_semantics=("arbitrary", "arbitrary", "arbitrary"),
            vmem_limit_bytes=VMEM_LIMIT),
        name="diff_attn",
    )(q, k, vt, gd, lq1, lk1, lq2, lk2, sg)


def _out_kernel(x_ref, yc_ref, yd_ref, w_ref, g_ref, o_ref):
    y = (jnp.dot(yc_ref[...], w_ref[0:CONV_WIDTH, :], preferred_element_type=jnp.float32)
         + jnp.dot(yd_ref[...], w_ref[CONV_WIDTH:, :], preferred_element_type=jnp.float32))
    ms = jnp.mean(y * y, axis=-1, keepdims=True)
    o_ref[...] = x_ref[...] + y * lax.rsqrt(ms + RMS_EPS) * g_ref[...]


def _out_proj(x, yc, yd, w_out_b, post_g):
    B, S, D = x.shape
    ts = SEQ_TILE
    row = lambda b, s: (b, s, 0)
    const2 = lambda b, s: (0, 0)
    return pl.pallas_call(
        _out_kernel,
        grid=(B, S // ts),
        in_specs=[
            pl.BlockSpec((None, ts, D), row),
            pl.BlockSpec((None, ts, CONV_WIDTH), row),
            pl.BlockSpec((None, ts, DIFF_WIDTH), row),
            pl.BlockSpec(w_out_b.shape, const2),
            pl.BlockSpec((1, D), const2),
        ],
        out_specs=pl.BlockSpec((None, ts, D), row),
        out_shape=jax.ShapeDtypeStruct((B, S, D), jnp.float32),
        compiler_params=pltpu.CompilerParams(
            dimension_semantics=("arbitrary", "arbitrary"),
            vmem_limit_bytes=VMEM_LIMIT),
        name="out_proj",
    )(x, yc, yd, w_out_b, post_g)


def kernel(x, pre_norm_g, w_in, conv_dw_w, conv_dw_b, conv_ln_g, conv_ln_b, conv_pw_w, conv_pw_b,
           lambda_q1, lambda_k1, lambda_q2, lambda_k2, diff_subln_g, w_out, post_norm_g):
    depth = pre_norm_g.shape[0]
    h = x
    for i in range(depth):
        w_in_b = w_in[i].astype(jnp.bfloat16)
        w_vt_b = w_in[i][:, C_V:C_V + DIFF_WIDTH].T.astype(jnp.bfloat16)
        yc, q, k, vt, gd = _proj_conv(
            h, pre_norm_g[i][None], w_in_b, w_vt_b, conv_dw_w[i], conv_dw_b[i][None],
            conv_ln_g[i][None], conv_ln_b[i][None], conv_pw_w[i].astype(jnp.bfloat16), conv_pw_b[i][None])
        yd = _attention(q, k, vt, gd, lambda_q1[i][None], lambda_k1[i][None], lambda_q2[i][None],
                        lambda_k2[i][None], diff_subln_g[i][None], _lambda_init(i))
        h = _out_proj(h, yc, yd, w_out[i].astype(jnp.bfloat16), post_norm_g[i][None])
    return h
```

```python
import functools
import math

import jax
import jax.numpy as jnp
from jax import lax
from jax.experimental import pallas as pl
from jax.experimental.pallas import tpu as pltpu

D_MODEL = 1024
CONV_WIDTH = 512
CONV_KERNEL = 31
DIFF_WIDTH = 512
DIFF_HEADS = 4
DIFF_VDIM = 128
DIFF_QKDIM = 64
RMS_EPS = 1e-6
LN_EPS = 1e-5


def _lambda_init(layer_idx):
    return 0.8 - 0.6 * math.exp(-0.3 * layer_idx)


LOG2E = math.log2(math.e)
Q_PRESCALE = (DIFF_QKDIM ** -0.5) * LOG2E

C_GLU = 0
C_CGATE = 1024
C_Q = 1536
C_K = 2048
C_V = 2560
C_DGATE = 3072

SEQ_TILE = 256
CONV_HALO = 32
LANES = 128
ATT_TQ = 512
ATT_TK = 512
POS_SHIFT = 8
POS_RADIX = 1 << POS_SHIFT
VT_PAD = 16
VMEM_LIMIT = 48 * 1024 * 1024

_NT = (((1,), (1,)), ((), ()))


def _silu(x):
    return x * jax.nn.sigmoid(x)


def _proj_conv_kernel(x_ref, g_ref, w_ref, wvt_ref, dww_ref, dwb_ref, lng_ref, lnb_ref, pww_ref, pwb_ref,
                      yc_ref, q_ref, k_ref, vt_ref, gd_ref, hbuf, cbuf):
    ts = x_ref.shape[0]
    s_idx = pl.program_id(1)

    x = x_ref[...]
    ms = jnp.mean(x * x, axis=-1, keepdims=True)
    xb = (x * lax.rsqrt(ms + RMS_EPS) * g_ref[...]).astype(jnp.bfloat16)

    def proj(c0, width):
        return jnp.dot(xb, w_ref[:, c0:c0 + width], preferred_element_type=jnp.float32)

    n_slab = CONV_WIDTH // LANES

    @pl.when(s_idx == 0)
    def _():
        hbuf[:, 0:CONV_HALO, :] = jnp.zeros((n_slab, CONV_HALO, LANES), jnp.float32)

    a = proj(C_GLU, CONV_WIDTH)
    b = proj(C_GLU + CONV_WIDTH, CONV_WIDTH)
    h = a * jax.nn.sigmoid(b)
    for l in range(n_slab):
        hbuf[l, CONV_HALO:CONV_HALO + ts, :] = h[:, l * LANES:(l + 1) * LANES]

    q_ref[...] = (proj(C_Q, DIFF_WIDTH) * Q_PRESCALE).astype(jnp.bfloat16)
    k_ref[...] = proj(C_K, DIFF_WIDTH).astype(jnp.bfloat16)
    vt_ref[...] = lax.dot_general(wvt_ref[...], xb, _NT,
                                  preferred_element_type=jnp.float32).astype(jnp.bfloat16)
    gd_ref[...] = _silu(proj(C_DGATE, DIFF_WIDTH))
    gate_c = _silu(proj(C_CGATE, CONV_WIDTH))

    off = CONV_HALO - (CONV_KERNEL - 1)
    half = ts // 2
    for l in range(n_slab):
        lanes = slice(l * LANES, (l + 1) * LANES)
        acc = [jnp.broadcast_to(dwb_ref[:, lanes], (half, LANES)) for _ in range(2)]
        for tau in range(CONV_KERNEL):
            w = dww_ref[tau:tau + 1, lanes]
            for par in range(2):
                acc[par] = acc[par] + w * hbuf[l, pl.ds(off + tau + par, half, stride=2), :]
        for par in range(2):
            cbuf[l, pl.ds(par, half, stride=2), :] = acc[par]

    hbuf[:, 0:CONV_HALO, :] = hbuf[:, ts:ts + CONV_HALO, :]

    cv = jnp.concatenate([cbuf[l] for l in range(n_slab)], axis=-1)
    mu = jnp.mean(cv, axis=-1, keepdims=True)
    d = cv - mu
    var = jnp.mean(d * d, axis=-1, keepdims=True)
    hn = d * lax.rsqrt(var + LN_EPS) * lng_ref[...] + lnb_ref[...]
    hs = _silu(hn).astype(jnp.bfloat16)
    yc = jnp.dot(hs, pww_ref[...], preferred_element_type=jnp.float32) + pwb_ref[...]
    yc_ref[...] = (yc * gate_c).astype(jnp.bfloat16)


def _proj_conv(x, pre_g, w_in_b, w_vt_b, dw_w, dw_b, ln_g, ln_b, pw_w_b, pw_b):
    B, S, D = x.shape
    ts = SEQ_TILE
    grid = (B, S // ts)
    row = lambda b, s: (b, s, 0)
    const2 = lambda b, s: (0, 0)
    in_specs = [
        pl.BlockSpec((None, ts, D), row),
        pl.BlockSpec((1, D), const2),
        pl.BlockSpec(w_in_b.shape, const2),
        pl.BlockSpec(w_vt_b.shape, const2),
        pl.BlockSpec(dw_w.shape, const2),
        pl.BlockSpec((1, CONV_WIDTH), const2),
        pl.BlockSpec((1, CONV_WIDTH), const2),
        pl.BlockSpec((1, CONV_WIDTH), const2),
        pl.BlockSpec(pw_w_b.shape, const2),
        pl.BlockSpec((1, CONV_WIDTH), const2),
    ]
    out_specs = [
        pl.BlockSpec((None, ts, CONV_WIDTH), row),
        pl.BlockSpec((None, ts, DIFF_WIDTH), row),
        pl.BlockSpec((None, ts, DIFF_WIDTH), row),
        pl.BlockSpec((None, DIFF_WIDTH, ts), lambda b, s: (b, 0, s)),
        pl.BlockSpec((None, ts, DIFF_WIDTH), row),
    ]
    out_shape = [
        jax.ShapeDtypeStruct((B, S, CONV_WIDTH), jnp.bfloat16),
        jax.ShapeDtypeStruct((B, S, DIFF_WIDTH), jnp.bfloat16),
        jax.ShapeDtypeStruct((B, S, DIFF_WIDTH), jnp.bfloat16),
        jax.ShapeDtypeStruct((B, DIFF_WIDTH, S), jnp.bfloat16),
        jax.ShapeDtypeStruct((B, S, DIFF_WIDTH), jnp.float32),
    ]
    return pl.pallas_call(
        _proj_conv_kernel,
        grid=grid,
        in_specs=in_specs,
        out_specs=out_specs,
        out_shape=out_shape,
        scratch_shapes=[
            pltpu.VMEM((CONV_WIDTH // LANES, CONV_HALO + ts, LANES), jnp.float32),
            pltpu.VMEM((CONV_WIDTH // LANES, ts, LANES), jnp.float32),
        ],
        compiler_params=pltpu.CompilerParams(
            dimension_semantics=("arbitrary", "arbitrary"),
            vmem_limit_bytes=VMEM_LIMIT),
        name="proj_conv",
    )(x, pre_g, w_in_b, w_vt_b, dw_w, dw_b, ln_g, ln_b, pw_w_b, pw_b)


def _attn_kernel(q_ref, k_ref, vt_ref, gd_ref, lq1_ref, lk1_ref, lq2_ref, lk2_ref, sg_ref,
                 o_ref, acc_ref, m_ref, mask_ref, t_ref, tmax_ref, kaug_ref, vta_ref, *, lam_init):
    tq = q_ref.shape[0]
    tk = ATT_TK
    seq = k_ref.shape[0]
    h = pl.program_id(1)
    qi = pl.program_id(2)
    f32, bf16 = jnp.float32, jnp.bfloat16

    hv = jnp.full((1, 1), h, jnp.int32).astype(f32)
    slope = jnp.exp2(-(8.0 / DIFF_HEADS) * (hv + 1.0)) * LOG2E

    @pl.when(qi == 0)
    def _():
        kaug_ref[:, 0:DIFF_VDIM] = k_ref[...]
        jj = lax.broadcasted_iota(jnp.int32, (tk, LANES), 0)
        ln = lax.broadcasted_iota(jnp.int32, (tk, LANES), 1)
        feat = jnp.where(ln < 3, jj & (POS_RADIX - 1), jnp.where(ln < 6, jj >> POS_SHIFT, 0))
        feat = feat.astype(f32).astype(bf16)
        for blk in range(seq // tk):
            kaug_ref[blk * tk:(blk + 1) * tk, DIFF_VDIM:] = feat
        vta_ref[0:DIFF_VDIM, :] = vt_ref[...]
        row = lax.broadcasted_iota(jnp.int32, (VT_PAD, seq), 0)
        vta_ref[DIFF_VDIM:, :] = jnp.where(row == 0, 1.0, 0.0).astype(bf16)
        kk = lax.broadcasted_iota(jnp.int32, (tk, tq), 0)
        ii = lax.broadcasted_iota(jnp.int32, (tk, tq), 1)
        mask_ref[...] = jnp.where(kk > ii, -jnp.inf, 0.0).astype(f32)

    srow = jnp.broadcast_to(slope, (1, LANES))
    s_hi = srow.astype(bf16).astype(f32)
    s_mid = (srow - s_hi).astype(bf16).astype(f32)
    s_lo = (srow - s_hi - s_mid).astype(bf16).astype(f32)
    ln = lax.broadcasted_iota(jnp.int32, (1, LANES), 1)
    piece = jnp.where((ln == 0) | (ln == 3), s_hi, jnp.where((ln == 1) | (ln == 4), s_mid, s_lo))
    srow = jnp.where(ln < 3, piece, jnp.where(ln < 6, piece * POS_RADIX, 0.0))
    saug = jnp.broadcast_to(srow, (tq, LANES)).astype(bf16)
    q = q_ref[...]
    lane = lax.broadcasted_iota(jnp.int32, q.shape, 1)
    zero = jnp.zeros_like(q)
    qaug = (jnp.concatenate([jnp.where(lane < DIFF_QKDIM, q, zero), saug], axis=1),
            jnp.concatenate([jnp.where(lane >= DIFF_QKDIM, q, zero), saug], axis=1))

    m_ref[...] = jnp.full(m_ref.shape, -jnp.inf, f32)
    acc_ref[...] = jnp.zeros(acc_ref.shape, f32)
    qpos = lax.broadcasted_iota(jnp.int32, (1, tq), 1).astype(f32)

    def scores(c, j, masked):
        k0 = pl.multiple_of(j * tk, tk)
        t = lax.dot_general(kaug_ref[pl.ds(k0, tk), :], qaug[c], _NT, preferred_element_type=f32)
        if masked:
            t = t + mask_ref[...]
        t_ref[c] = t
        tmax_ref[c] = jnp.max(t, axis=0, keepdims=True)

    def softmax_pv(c, j):
        k0 = pl.multiple_of(j * tk, tk)
        cq = slope * ((j * tk - qi * tq).astype(f32) - qpos)
        m_old = m_ref[c]
        m_new = jnp.maximum(m_old, tmax_ref[c] + cq)
        p = jnp.exp2(t_ref[c] - (m_new - cq))
        alpha = jnp.exp2(m_old - m_new)
        acc_ref[c] = alpha * acc_ref[c] + jnp.dot(vta_ref[:, pl.ds(k0, tk)], p.astype(bf16),
                                                  preferred_element_type=f32)
        m_ref[c] = m_new

    def step(j, next_masked):
        scores(1, j, False)
        softmax_pv(0, j)
        scores(0, j + 1, next_masked)
        softmax_pv(1, j)

    @pl.when(qi == 0)
    def _():
        scores(0, 0, True)
        scores(1, 0, True)
        softmax_pv(0, 0)
        softmax_pv(1, 0)

    @pl.when(qi > 0)
    def _():
        scores(0, 0, False)

        def body(j, carry):
            step(j, False)
            return carry

        lax.fori_loop(0, qi - 1, body, 0)
        step(qi - 1, True)
        scores(1, qi, True)
        softmax_pv(0, qi)
        softmax_pv(1, qi)

    lam = (jnp.exp(jnp.sum(lq1_ref[...] * lk1_ref[...], axis=-1, keepdims=True))
           - jnp.exp(jnp.sum(lq2_ref[...] * lk2_ref[...], axis=-1, keepdims=True)) + lam_init)
    a0 = acc_ref[0]
    a1 = acc_ref[1]
    o = (a0[0:DIFF_VDIM] / a0[DIFF_VDIM:DIFF_VDIM + 1]
         - lam * (a1[0:DIFF_VDIM] / a1[DIFF_VDIM:DIFF_VDIM + 1]))
    ms = jnp.mean(o * o, axis=0, keepdims=True)
    on = (o * lax.rsqrt(ms + RMS_EPS)).T
    y = on * (sg_ref[...] * (1.0 - lam_init)) * gd_ref[...]
    o_ref[...] = y.astype(bf16)


def _attention(q, k, vt, gd, lq1, lk1, lq2, lk2, sg, lam_init):
    B, S, _ = q.shape
    tq = ATT_TQ
    grid = (B, DIFF_HEADS, S // tq)
    vec = lambda b, h, i: (0, 0)
    in_specs = [
        pl.BlockSpec((None, tq, DIFF_VDIM), lambda b, h, i: (b, i, h)),
        pl.BlockSpec((None, S, DIFF_VDIM), lambda b, h, i: (b, 0, h)),
        pl.BlockSpec((None, DIFF_VDIM, S), lambda b, h, i: (b, h, 0)),
        pl.BlockSpec((None, tq, DIFF_VDIM), lambda b, h, i: (b, i, h)),
        pl.BlockSpec((1, DIFF_QKDIM), vec),
        pl.BlockSpec((1, DIFF_QKDIM), vec),
        pl.BlockSpec((1, DIFF_QKDIM), vec),
        pl.BlockSpec((1, DIFF_QKDIM), vec),
        pl.BlockSpec((1, DIFF_VDIM), vec),
    ]
    return pl.pallas_call(
        functools.partial(_attn_kernel, lam_init=lam_init),
        grid=grid,
        in_specs=in_specs,
        out_specs=pl.BlockSpec((None, tq, DIFF_VDIM), lambda b, h, i: (b, i, h)),
        out_shape=jax.ShapeDtypeStruct((B, S, DIFF_WIDTH), jnp.bfloat16),
        scratch_shapes=[
            pltpu.VMEM((2, DIFF_VDIM + VT_PAD, tq), jnp.float32),
            pltpu.VMEM((2, 1, tq), jnp.float32),
            pltpu.VMEM((ATT_TK, tq), jnp.float32),
            pltpu.VMEM((2, ATT_TK, tq), jnp.float32),
            pltpu.VMEM((2, 1, tq), jnp.float32),
            pltpu.VMEM((S, 2 * LANES), jnp.bfloat16),
            pltpu.VMEM((DIFF_VDIM + VT_PAD, S), jnp.bfloat16),
        ],
        compiler_params=pltpu.CompilerParams(
            dimension_semantics=("arbitrary", "arbitrary", "arbitrary"),
            vmem_limit_bytes=VMEM_LIMIT),
        name="diff_attn",
    )(q, k, vt, gd, lq1, lk1, lq2, lk2, sg)


def _out_kernel(x_ref, yc_ref, yd_ref, w_ref, g_ref, o_ref):
    y = (jnp.dot(yc_ref[...], w_ref[0:CONV_WIDTH, :], preferred_element_type=jnp.float32)
         + jnp.dot(yd_ref[...], w_ref[CONV_WIDTH:, :], preferred_element_type=jnp.float32))
    ms = jnp.mean(y * y, axis=-1, keepdims=True)
    o_ref[...] = x_ref[...] + y * lax.rsqrt(ms + RMS_EPS) * g_ref[...]


def _out_proj(x, yc, yd, w_out_b, post_g):
    B, S, D = x.shape
    ts = SEQ_TILE
    row = lambda b, s: (b, s, 0)
    const2 = lambda b, s: (0, 0)
    return pl.pallas_call(
        _out_kernel,
        grid=(B, S // ts),
        in_specs=[
            pl.BlockSpec((None, ts, D), row),
            pl.BlockSpec((None, ts, CONV_WIDTH), row),
            pl.BlockSpec((None, ts, DIFF_WIDTH), row),
            pl.BlockSpec(w_out_b.shape, const2),
            pl.BlockSpec((1, D), const2),
        ],
        out_specs=pl.BlockSpec((None, ts, D), row),
        out_shape=jax.ShapeDtypeStruct((B, S, D), jnp.float32),
        compiler_params=pltpu.CompilerParams(
            dimension_semantics=("arbitrary", "arbitrary"),
            vmem_limit_bytes=VMEM_LIMIT),
        name="out_proj",
    )(x, yc, yd, w_out_b, post_g)


def kernel(x, pre_norm_g, w_in, conv_dw_w, conv_dw_b, conv_ln_g, conv_ln_b, conv_pw_w, conv_pw_b,
           lambda_q1, lambda_k1, lambda_q2, lambda_k2, diff_subln_g, w_out, post_norm_g):
    depth = pre_norm_g.shape[0]
    h = x
    for i in range(depth):
        w_in_b = w_in[i].astype(jnp.bfloat16)
        w_vt_b = w_in[i][:, C_V:C_V + DIFF_WIDTH].T.astype(jnp.bfloat16)
        yc, q, k, vt, gd = _proj_conv(
            h, pre_norm_g[i][None], w_in_b, w_vt_b, conv_dw_w[i], conv_dw_b[i][None],
            conv_ln_g[i][None], conv_ln_b[i][None], conv_pw_w[i].astype(jnp.bfloat16), conv_pw_b[i][None])
        yd = _attention(q, k, vt, gd, lambda_q1[i][None], lambda_k1[i][None], lambda_q2[i][None],
                        lambda_k2[i][None], diff_subln_g[i][None], _lambda_init(i))
        h = _out_proj(h, yc, yd, w_out[i].astype(jnp.bfloat16), post_norm_g[i][None])
    return h
```

```python
import functools
import math

import jax
import jax.numpy as jnp
from jax import lax
from jax.experimental import pallas as pl
from jax.experimental.pallas import tpu as pltpu

D_MODEL = 1024
CONV_WIDTH = 512
CONV_KERNEL = 31
DIFF_WIDTH = 512
DIFF_HEADS = 4
DIFF_VDIM = 128
DIFF_QKDIM = 64
RMS_EPS = 1e-6
LN_EPS = 1e-5


def _lambda_init(layer_idx):
    return 0.8 - 0.6 * math.exp(-0.3 * layer_idx)


LOG2E = math.log2(math.e)
Q_PRESCALE = (DIFF_QKDIM ** -0.5) * LOG2E

C_GLU = 0
C_CGATE = 1024
C_Q = 1536
C_K = 2048
C_V = 2560
C_DGATE = 3072

SEQ_TILE = 256
CONV_HALO = 32
LANES = 128
ATT_TQ = 512
ATT_TK = 512
ATT_ITEM = 1024
POS_SHIFT = 8
POS_RADIX = 1 << POS_SHIFT
VT_PAD = 16
VMEM_LIMIT = 48 * 1024 * 1024

_NT = (((1,), (1,)), ((), ()))


def _silu(x):
    return x * jax.nn.sigmoid(x)


def _aligned(start, multiple):
    return start if isinstance(start, int) else pl.multiple_of(start, multiple)


def _proj_conv_kernel(x_ref, g_ref, w_ref, wqt_ref, wvt_ref, dww_ref, dwb_ref, lng_ref, lnb_ref, pww_ref, pwb_ref,
                      yc_ref, qt_ref, k_ref, vt_ref, gd_ref, hbuf, cbuf):
    ts = x_ref.shape[0]
    s_idx = pl.program_id(1)

    x = x_ref[...]
    ms = jnp.mean(x * x, axis=-1, keepdims=True)
    xb = (x * lax.rsqrt(ms + RMS_EPS) * g_ref[...]).astype(jnp.bfloat16)

    def proj(c0, width):
        return jnp.dot(xb, w_ref[:, c0:c0 + width], preferred_element_type=jnp.float32)

    n_slab = CONV_WIDTH // LANES

    @pl.when(s_idx == 0)
    def _():
        hbuf[:, 0:CONV_HALO, :] = jnp.zeros((n_slab, CONV_HALO, LANES), jnp.float32)

    a = proj(C_GLU, CONV_WIDTH)
    b = proj(C_GLU + CONV_WIDTH, CONV_WIDTH)
    h = a * jax.nn.sigmoid(b)
    for l in range(n_slab):
        hbuf[l, CONV_HALO:CONV_HALO + ts, :] = h[:, l * LANES:(l + 1) * LANES]

    qt_ref[...] = (lax.dot_general(wqt_ref[...], xb, _NT, preferred_element_type=jnp.float32)
                   * Q_PRESCALE).astype(jnp.bfloat16)
    k_ref[...] = proj(C_K, DIFF_WIDTH).astype(jnp.bfloat16)
    vt_ref[...] = lax.dot_general(wvt_ref[...], xb, _NT,
                                  preferred_element_type=jnp.float32).astype(jnp.bfloat16)
    gd_ref[...] = _silu(proj(C_DGATE, DIFF_WIDTH))
    gate_c = _silu(proj(C_CGATE, CONV_WIDTH))

    off = CONV_HALO - (CONV_KERNEL - 1)
    half = ts // 2
    for l in range(n_slab):
        lanes = slice(l * LANES, (l + 1) * LANES)
        acc = [jnp.broadcast_to(dwb_ref[:, lanes], (half, LANES)) for _ in range(2)]
        for tau in range(CONV_KERNEL):
            w = dww_ref[tau:tau + 1, lanes]
            for par in range(2):
                acc[par] = acc[par] + w * hbuf[l, pl.ds(off + tau + par, half, stride=2), :]
        for par in range(2):
            cbuf[l, pl.ds(par, half, stride=2), :] = acc[par]

    hbuf[:, 0:CONV_HALO, :] = hbuf[:, ts:ts + CONV_HALO, :]

    cv = jnp.concatenate([cbuf[l] for l in range(n_slab)], axis=-1)
    mu = jnp.mean(cv, axis=-1, keepdims=True)
    d = cv - mu
    var = jnp.mean(d * d, axis=-1, keepdims=True)
    hn = d * lax.rsqrt(var + LN_EPS) * lng_ref[...] + lnb_ref[...]
    hs = _silu(hn).astype(jnp.bfloat16)
    yc = jnp.dot(hs, pww_ref[...], preferred_element_type=jnp.float32) + pwb_ref[...]
    yc_ref[...] = (yc * gate_c).astype(jnp.bfloat16)


def _proj_conv(x, pre_g, w_in_b, w_qt_b, w_vt_b, dw_w, dw_b, ln_g, ln_b, pw_w_b, pw_b):
    B, S, D = x.shape
    ts = SEQ_TILE
    grid = (B, S // ts)
    row = lambda b, s: (b, s, 0)
    const2 = lambda b, s: (0, 0)
    in_specs = [
        pl.BlockSpec((None, ts, D), row),
        pl.BlockSpec((1, D), const2),
        pl.BlockSpec(w_in_b.shape, const2),
        pl.BlockSpec(w_qt_b.shape, const2),
        pl.BlockSpec(w_vt_b.shape, const2),
        pl.BlockSpec(dw_w.shape, const2),
        pl.BlockSpec((1, CONV_WIDTH), const2),
        pl.BlockSpec((1, CONV_WIDTH), const2),
        pl.BlockSpec((1, CONV_WIDTH), const2),
        pl.BlockSpec(pw_w_b.shape, const2),
        pl.BlockSpec((1, CONV_WIDTH), const2),
    ]
    out_specs = [
        pl.BlockSpec((None, ts, CONV_WIDTH), row),
        pl.BlockSpec((None, DIFF_WIDTH, ts), lambda b, s: (b, 0, s)),
        pl.BlockSpec((None, ts, DIFF_WIDTH), row),
        pl.BlockSpec((None, DIFF_WIDTH, ts), lambda b, s: (b, 0, s)),
        pl.BlockSpec((None, ts, DIFF_WIDTH), row),
    ]
    out_shape = [
        jax.ShapeDtypeStruct((B, S, CONV_WIDTH), jnp.bfloat16),
        jax.ShapeDtypeStruct((B, DIFF_WIDTH, S), jnp.bfloat16),
        jax.ShapeDtypeStruct((B, S, DIFF_WIDTH), jnp.bfloat16),
        jax.ShapeDtypeStruct((B, DIFF_WIDTH, S), jnp.bfloat16),
        jax.ShapeDtypeStruct((B, S, DIFF_WIDTH), jnp.float32),
    ]
    return pl.pallas_call(
        _proj_conv_kernel,
        grid=grid,
        in_specs=in_specs,
        out_specs=out_specs,
        out_shape=out_shape,
        scratch_shapes=[
            pltpu.VMEM((CONV_WIDTH // LANES, CONV_HALO + ts, LANES), jnp.float32),
            pltpu.VMEM((CONV_WIDTH // LANES, ts, LANES), jnp.float32),
        ],
        compiler_params=pltpu.CompilerParams(
            dimension_semantics=("arbitrary", "arbitrary"),
            vmem_limit_bytes=VMEM_LIMIT),
        name="proj_conv",
    )(x, pre_g, w_in_b, w_qt_b, w_vt_b, dw_w, dw_b, ln_g, ln_b, pw_w_b, pw_b)


def _attn_kernel(qt_ref, k_ref, vt_ref, gd_ref, lq1_ref, lk1_ref, lq2_ref, lk2_ref, sg_ref,
                 o_ref, acc_ref, m_ref, mask_ref, t_ref, tmax_ref, kaug_ref, vta_ref, *, lam_init):
    tq = qt_ref.shape[1]
    tk = ATT_TK
    seq = k_ref.shape[0]
    h = pl.program_id(1)
    qi = pl.program_id(2)
    f32, bf16 = jnp.float32, jnp.bfloat16

    hv = jnp.full((1, 1), h, jnp.int32).astype(f32)
    slope = jnp.exp2(-(8.0 / DIFF_HEADS) * (hv + 1.0)) * LOG2E

    @pl.when(qi == 0)
    def _():
        kaug_ref[:, 0:DIFF_VDIM] = k_ref[...]
        jj = lax.broadcasted_iota(jnp.int32, (ATT_ITEM, LANES), 0)
        ln = lax.broadcasted_iota(jnp.int32, (ATT_ITEM, LANES), 1)
        feat = jnp.where(ln < 3, jj & (POS_RADIX - 1), jnp.where(ln < 6, jj >> POS_SHIFT, 0))
        feat = feat.astype(f32).astype(bf16)
        for blk in range(seq // ATT_ITEM):
            kaug_ref[blk * ATT_ITEM:(blk + 1) * ATT_ITEM, DIFF_VDIM:] = feat
        vta_ref[0:DIFF_VDIM, :] = vt_ref[...]
        row = lax.broadcasted_iota(jnp.int32, (VT_PAD, seq), 0)
        vta_ref[DIFF_VDIM:, :] = jnp.where(row == 0, 1.0, 0.0).astype(bf16)
        kk = lax.broadcasted_iota(jnp.int32, (tk, tq), 0)
        ii = lax.broadcasted_iota(jnp.int32, (tk, tq), 1)
        mask_ref[...] = jnp.where(kk > ii, -jnp.inf, 0.0).astype(f32)

    sl = jnp.broadcast_to(slope, (1, tq))
    s_hi = sl.astype(bf16).astype(f32)
    s_mid = (sl - s_hi).astype(bf16).astype(f32)
    s_lo = (sl - s_hi - s_mid).astype(bf16).astype(f32)
    r = lax.broadcasted_iota(jnp.int32, (VT_PAD, tq), 0)
    piece = jnp.where((r == 0) | (r == 3), s_hi, jnp.where((r == 1) | (r == 4), s_mid, s_lo))
    srows = jnp.where(r < 3, piece, jnp.where(r < 6, piece * POS_RADIX, 0.0)).astype(bf16)
    saug = jnp.concatenate([srows, jnp.zeros((LANES - VT_PAD, tq), bf16)], axis=0)
    qt = qt_ref[...]
    zhalf = jnp.zeros((DIFF_QKDIM, tq), bf16)
    qaug = (jnp.concatenate([qt[0:DIFF_QKDIM], zhalf, saug], axis=0),
            jnp.concatenate([zhalf, qt[DIFF_QKDIM:], saug], axis=0))

    m_ref[...] = jnp.full(m_ref.shape, -jnp.inf, f32)
    acc_ref[...] = jnp.zeros(acc_ref.shape, f32)
    qpos = lax.broadcasted_iota(jnp.int32, (1, tq), 1).astype(f32)

    def scores(c, u0, n_units, mask_last):
        mx = None
        for u in range(n_units):
            k0 = _aligned((u0 + u) * tk, tk)
            t = jnp.dot(kaug_ref[pl.ds(k0, tk), :], qaug[c], preferred_element_type=f32)
            if mask_last and u == n_units - 1:
                t = t + mask_ref[...]
            t_ref[c, u * tk:(u + 1) * tk, :] = t
            tm = jnp.max(t, axis=0, keepdims=True)
            mx = tm if mx is None else jnp.maximum(mx, tm)
        tmax_ref[c] = mx

    def softmax_pv(c, u0, n_units):
        k0 = _aligned(u0 * tk, tk)
        nk = n_units * tk
        cq = slope * ((u0 * tk - qi * tq).astype(f32) - qpos)
        m_old = m_ref[c]
        m_new = jnp.maximum(m_old, tmax_ref[c] + cq)
        p = jnp.exp2(t_ref[c, 0:nk, :] - (m_new - cq))
        alpha = jnp.exp2(m_old - m_new)
        acc_ref[c] = alpha * acc_ref[c] + jnp.dot(vta_ref[:, pl.ds(k0, nk)], p.astype(bf16),
                                                  preferred_element_type=f32)
        m_ref[c] = m_new

    units = ATT_ITEM // tk
    nb = qi // units
    tail0 = nb * units

    def run(n_tail):
        @pl.when((qi % units == n_tail - 1) & (nb == 0))
        def _():
            scores(0, 0, n_tail, True)
            scores(1, 0, n_tail, True)
            softmax_pv(0, 0, n_tail)
            softmax_pv(1, 0, n_tail)

        @pl.when((qi % units == n_tail - 1) & (nb > 0))
        def _():
            scores(0, 0, units, False)

            def body(j, carry):
                scores(1, j * units, units, False)
                softmax_pv(0, j * units, units)
                scores(0, (j + 1) * units, units, False)
                softmax_pv(1, j * units, units)
                return carry

            lax.fori_loop(0, nb - 1, body, 0)
            last = tail0 - units
            scores(1, last, units, False)
            softmax_pv(0, last, units)
            scores(0, tail0, n_tail, True)
            softmax_pv(1, last, units)
            scores(1, tail0, n_tail, True)
            softmax_pv(0, tail0, n_tail)
            softmax_pv(1, tail0, n_tail)

    for n_tail in range(1, units + 1):
        run(n_tail)

    lam = (jnp.exp(jnp.sum(lq1_ref[...] * lk1_ref[...], axis=-1, keepdims=True))
           - jnp.exp(jnp.sum(lq2_ref[...] * lk2_ref[...], axis=-1, keepdims=True)) + lam_init)
    a0 = acc_ref[0]
    a1 = acc_ref[1]
    o = (a0[0:DIFF_VDIM] / a0[DIFF_VDIM:DIFF_VDIM + 1]
         - lam * (a1[0:DIFF_VDIM] / a1[DIFF_VDIM:DIFF_VDIM + 1]))
    ms = jnp.mean(o * o, axis=0, keepdims=True)
    on = (o * lax.rsqrt(ms + RMS_EPS)).T
    y = on * (sg_ref[...] * (1.0 - lam_init)) * gd_ref[...]
    o_ref[...] = y.astype(bf16)


def _attention(qt, k, vt, gd, lq1, lk1, lq2, lk2, sg, lam_init):
    B, S, _ = k.shape
    assert ATT_TQ == ATT_TK and ATT_ITEM % ATT_TK == 0 and S % ATT_ITEM == 0
    tq = ATT_TQ
    grid = (B, DIFF_HEADS, S // tq)
    vec = lambda b, h, i: (0, 0)
    in_specs = [
        pl.BlockSpec((None, DIFF_VDIM, tq), lambda b, h, i: (b, h, i)),
        pl.BlockSpec((None, S, DIFF_VDIM), lambda b, h, i: (b, 0, h)),
        pl.BlockSpec((None, DIFF_VDIM, S), lambda b, h, i: (b, h, 0)),
        pl.BlockSpec((None, tq, DIFF_VDIM), lambda b, h, i: (b, i, h)),
        pl.BlockSpec((1, DIFF_QKDIM), vec),
        pl.BlockSpec((1, DIFF_QKDIM), vec),
        pl.BlockSpec((1, DIFF_QKDIM), vec),
        pl.BlockSpec((1, DIFF_QKDIM), vec),
        pl.BlockSpec((1, DIFF_VDIM), vec),
    ]
    return pl.pallas_call(
        functools.partial(_attn_kernel, lam_init=lam_init),
        grid=grid,
        in_specs=in_specs,
        out_specs=pl.BlockSpec((None, tq, DIFF_VDIM), lambda b, h, i: (b, i, h)),
        out_shape=jax.ShapeDtypeStruct((B, S, DIFF_WIDTH), jnp.bfloat16),
        scratch_shapes=[
            pltpu.VMEM((2, DIFF_VDIM + VT_PAD, tq), jnp.float32),
            pltpu.VMEM((2, 1, tq), jnp.float32),
            pltpu.VMEM((ATT_TK, tq), jnp.float32),
            pltpu.VMEM((2, ATT_ITEM, tq), jnp.float32),
            pltpu.VMEM((2, 1, tq), jnp.float32),
            pltpu.VMEM((S, 2 * LANES), jnp.bfloat16),
            pltpu.VMEM((DIFF_VDIM + VT_PAD, S), jnp.bfloat16),
        ],
        compiler_params=pltpu.CompilerParams(
            dimension_semantics=("arbitrary", "arbitrary", "arbitrary"),
            vmem_limit_bytes=VMEM_LIMIT),
        name="diff_attn",
    )(qt, k, vt, gd, lq1, lk1, lq2, lk2, sg)


def _out_kernel(x_ref, yc_ref, yd_ref, w_ref, g_ref, o_ref):
    y = (jnp.dot(yc_ref[...], w_ref[0:CONV_WIDTH, :], preferred_element_type=jnp.float32)
         + jnp.dot(yd_ref[...], w_ref[CONV_WIDTH:, :], preferred_element_type=jnp.float32))
    ms = jnp.mean(y * y, axis=-1, keepdims=True)
    o_ref[...] = x_ref[...] + y * lax.rsqrt(ms + RMS_EPS) * g_ref[...]


def _out_proj(x, yc, yd, w_out_b, post_g):
    B, S, D = x.shape
    ts = SEQ_TILE
    row = lambda b, s: (b, s, 0)
    const2 = lambda b, s: (0, 0)
    return pl.pallas_call(
        _out_kernel,
        grid=(B, S // ts),
        in_specs=[
            pl.BlockSpec((None, ts, D), row),
            pl.BlockSpec((None, ts, CONV_WIDTH), row),
            pl.BlockSpec((None, ts, DIFF_WIDTH), row),
            pl.BlockSpec(w_out_b.shape, const2),
            pl.BlockSpec((1, D), const2),
        ],
        out_specs=pl.BlockSpec((None, ts, D), row),
        out_shape=jax.ShapeDtypeStruct((B, S, D), jnp.float32),
        compiler_params=pltpu.CompilerParams(
            dimension_semantics=("arbitrary", "arbitrary"),
            vmem_limit_bytes=VMEM_LIMIT),
        name="out_proj",
    )(x, yc, yd, w_out_b, post_g)


def kernel(x, pre_norm_g, w_in, conv_dw_w, conv_dw_b, conv_ln_g, conv_ln_b, conv_pw_w, conv_pw_b,
           lambda_q1, lambda_k1, lambda_q2, lambda_k2, diff_subln_g, w_out, post_norm_g):
    depth = pre_norm_g.shape[0]
    h = x
    for i in range(depth):
        w_in_b = w_in[i].astype(jnp.bfloat16)
        w_qt_b = w_in[i][:, C_Q:C_Q + DIFF_WIDTH].T.astype(jnp.bfloat16)
        w_vt_b = w_in[i][:, C_V:C_V + DIFF_WIDTH].T.astype(jnp.bfloat16)
        yc, qt, k, vt, gd = _proj_conv(
            h, pre_norm_g[i][None], w_in_b, w_qt_b, w_vt_b, conv_dw_w[i], conv_dw_b[i][None],
            conv_ln_g[i][None], conv_ln_b[i][None], conv_pw_w[i].astype(jnp.bfloat16), conv_pw_b[i][None])
        yd = _attention(qt, k, vt, gd, lambda_q1[i][None], lambda_k1[i][None], lambda_q2[i][None],
                        lambda_k2[i][None], diff_subln_g[i][None], _lambda_init(i))
        h = _out_proj(h, yc, yd, w_out[i].astype(jnp.bfloat16), post_norm_g[i][None])
    return h
```

```python
import functools
import math

import jax
import jax.numpy as jnp
from jax import lax
from jax.experimental import pallas as pl
from jax.experimental.pallas import tpu as pltpu

D_MODEL = 1024
CONV_WIDTH = 512
CONV_KERNEL = 31
DIFF_WIDTH = 512
DIFF_HEADS = 4
DIFF_VDIM = 128
DIFF_QKDIM = 64
RMS_EPS = 1e-6
LN_EPS = 1e-5


def _lambda_init(layer_idx):
    return 0.8 - 0.6 * math.exp(-0.3 * layer_idx)


LOG2E = math.log2(math.e)
Q_PRESCALE = (DIFF_QKDIM ** -0.5) * LOG2E

C_GLU = 0
C_CGATE = 1024
C_Q = 1536
C_K = 2048
C_V = 2560
C_DGATE = 3072

PROJ_TILE = 512
OUT_TILE = 512
CONV_HALO = 32
CONV_ROWS = 256
LANES = 128
ATT_TQ = 512
ATT_TK = 512
ATT_ITEM = 1024
POS_SHIFT = 8
POS_RADIX = 1 << POS_SHIFT
VT_PAD = 16
VMEM_LIMIT = 48 * 1024 * 1024

_NT = (((1,), (1,)), ((), ()))


def _silu(x):
    return x * jax.nn.sigmoid(x)


def _aligned(start, multiple):
    return start if isinstance(start, int) else pl.multiple_of(start, multiple)


def _proj_conv_kernel(x_ref, g_ref, w_ref, wqt_ref, wvt_ref, dww_ref, dwb_ref, lng_ref, lnb_ref, pww_ref, pwb_ref,
                      yc_ref, qt_ref, k_ref, vt_ref, gd_ref, hbuf, cbuf):
    ts = x_ref.shape[0]
    s_idx = pl.program_id(1)

    x = x_ref[...]
    ms = jnp.mean(x * x, axis=-1, keepdims=True)
    xb = (x * lax.rsqrt(ms + RMS_EPS) * g_ref[...]).astype(jnp.bfloat16)

    def proj(c0, width):
        return jnp.dot(xb, w_ref[:, c0:c0 + width], preferred_element_type=jnp.float32)

    n_slab = CONV_WIDTH // LANES

    @pl.when(s_idx == 0)
    def _():
        hbuf[:, 0:CONV_HALO, :] = jnp.zeros((n_slab, CONV_HALO, LANES), jnp.float32)

    a = proj(C_GLU, CONV_WIDTH)
    b = proj(C_GLU + CONV_WIDTH, CONV_WIDTH)
    h = a * jax.nn.sigmoid(b)
    for l in range(n_slab):
        hbuf[l, CONV_HALO:CONV_HALO + ts, :] = h[:, l * LANES:(l + 1) * LANES]

    off = CONV_HALO - (CONV_KERNEL - 1)
    half = CONV_ROWS // 2

    def conv_block(l, r0):
        lanes = slice(l * LANES, (l + 1) * LANES)
        acc = [jnp.broadcast_to(dwb_ref[:, lanes], (half, LANES)) for _ in range(2)]
        for tau in range(CONV_KERNEL):
            w = dww_ref[tau:tau + 1, lanes]
            for par in range(2):
                acc[par] = acc[par] + w * hbuf[l, pl.ds(r0 + off + tau + par, half, stride=2), :]
        for par in range(2):
            cbuf[l, pl.ds(r0 + par, half, stride=2), :] = acc[par]

    qt_ref[...] = (lax.dot_general(wqt_ref[...], xb, _NT, preferred_element_type=jnp.float32)
                   * Q_PRESCALE).astype(jnp.bfloat16)
    k_ref[...] = proj(C_K, DIFF_WIDTH).astype(jnp.bfloat16)
    vt_ref[...] = lax.dot_general(wvt_ref[...], xb, _NT,
                                  preferred_element_type=jnp.float32).astype(jnp.bfloat16)
    gd_ref[...] = _silu(proj(C_DGATE, DIFF_WIDTH))
    gate_c = _silu(proj(C_CGATE, CONV_WIDTH))

    for r0 in range(0, ts, CONV_ROWS):
        for l in range(n_slab):
            conv_block(l, r0)

    hbuf[:, 0:CONV_HALO, :] = hbuf[:, ts:ts + CONV_HALO, :]

    cv = jnp.concatenate([cbuf[l] for l in range(n_slab)], axis=-1)
    mu = jnp.mean(cv, axis=-1, keepdims=True)
    d = cv - mu
    var = jnp.mean(d * d, axis=-1, keepdims=True)
    hn = d * lax.rsqrt(var + LN_EPS) * lng_ref[...] + lnb_ref[...]
    hs = _silu(hn).astype(jnp.bfloat16)
    yc = jnp.dot(hs, pww_ref[...], preferred_element_type=jnp.float32) + pwb_ref[...]
    yc_ref[...] = (yc * gate_c).astype(jnp.bfloat16)


def _proj_conv(x, pre_g, w_in_b, w_qt_b, w_vt_b, dw_w, dw_b, ln_g, ln_b, pw_w_b, pw_b):
    B, S, D = x.shape
    ts = PROJ_TILE
    assert S % ts == 0 and ts % CONV_ROWS == 0
    grid = (B, S // ts)
    row = lambda b, s: (b, s, 0)
    const2 = lambda b, s: (0, 0)
    in_specs = [
        pl.BlockSpec((None, ts, D), row),
        pl.BlockSpec((1, D), const2),
        pl.BlockSpec(w_in_b.shape, const2),
        pl.BlockSpec(w_qt_b.shape, const2),
        pl.BlockSpec(w_vt_b.shape, const2),
        pl.BlockSpec(dw_w.shape, const2),
        pl.BlockSpec((1, CONV_WIDTH), const2),
        pl.BlockSpec((1, CONV_WIDTH), const2),
        pl.BlockSpec((1, CONV_WIDTH), const2),
        pl.BlockSpec(pw_w_b.shape, const2),
        pl.BlockSpec((1, CONV_WIDTH), const2),
    ]
    out_specs = [
        pl.BlockSpec((None, ts, CONV_WIDTH), row),
        pl.BlockSpec((None, DIFF_WIDTH, ts), lambda b, s: (b, 0, s)),
        pl.BlockSpec((None, ts, DIFF_WIDTH), row),
        pl.BlockSpec((None, DIFF_WIDTH, ts), lambda b, s: (b, 0, s)),
        pl.BlockSpec((None, ts, DIFF_WIDTH), row),
    ]
    out_shape = [
        jax.ShapeDtypeStruct((B, S, CONV_WIDTH), jnp.bfloat16),
        jax.ShapeDtypeStruct((B, DIFF_WIDTH, S), jnp.bfloat16),
        jax.ShapeDtypeStruct((B, S, DIFF_WIDTH), jnp.bfloat16),
        jax.ShapeDtypeStruct((B, DIFF_WIDTH, S), jnp.bfloat16),
        jax.ShapeDtypeStruct((B, S, DIFF_WIDTH), jnp.float32),
    ]
    return pl.pallas_call(
        _proj_conv_kernel,
        grid=grid,
        in_specs=in_specs,
        out_specs=out_specs,
        out_shape=out_shape,
        scratch_shapes=[
            pltpu.VMEM((CONV_WIDTH // LANES, CONV_HALO + ts, LANES), jnp.float32),
            pltpu.VMEM((CONV_WIDTH // LANES, ts, LANES), jnp.float32),
        ],
        compiler_params=pltpu.CompilerParams(
            dimension_semantics=("arbitrary", "arbitrary"),
            vmem_limit_bytes=VMEM_LIMIT),
        name="proj_conv",
    )(x, pre_g, w_in_b, w_qt_b, w_vt_b, dw_w, dw_b, ln_g, ln_b, pw_w_b, pw_b)


def _attn_kernel(qt_ref, qtn_ref, k_ref, vt_ref, gd_ref, lq1_ref, lk1_ref, lq2_ref, lk2_ref, sg_ref,
                 o_ref, acc_ref, m_ref, mask_ref, t_ref, tmax_ref, kaug_ref, vta_ref, qaug_ref,
                 *, lam_init):
    tq = qt_ref.shape[1]
    tk = ATT_TK
    seq = k_ref.shape[0]
    h = pl.program_id(1)
    qi = pl.program_id(2)
    f32, bf16 = jnp.float32, jnp.bfloat16

    hv = jnp.full((1, 1), h, jnp.int32).astype(f32)
    slope = jnp.exp2(-(8.0 / DIFF_HEADS) * (hv + 1.0)) * LOG2E

    @pl.when(qi == 0)
    def _():
        kaug_ref[:, 0:DIFF_VDIM] = k_ref[...]
        jj = lax.broadcasted_iota(jnp.int32, (ATT_ITEM, LANES), 0)
        ln = lax.broadcasted_iota(jnp.int32, (ATT_ITEM, LANES), 1)
        feat = jnp.where(ln < 3, jj & (POS_RADIX - 1), jnp.where(ln < 6, jj >> POS_SHIFT, 0))
        feat = feat.astype(f32).astype(bf16)
        for blk in range(seq // ATT_ITEM):
            kaug_ref[blk * ATT_ITEM:(blk + 1) * ATT_ITEM, DIFF_VDIM:] = feat
        vta_ref[0:DIFF_VDIM, :] = vt_ref[...]
        row = lax.broadcasted_iota(jnp.int32, (VT_PAD, seq), 0)
        vta_ref[DIFF_VDIM:, :] = jnp.where(row == 0, 1.0, 0.0).astype(bf16)
        kk = lax.broadcasted_iota(jnp.int32, (tk, tq), 0)
        ii = lax.broadcasted_iota(jnp.int32, (tk, tq), 1)
        mask_ref[...] = jnp.where(kk > ii, -jnp.inf, 0.0).astype(f32)

    sl = jnp.broadcast_to(slope, (1, tq))
    s_hi = sl.astype(bf16).astype(f32)
    s_mid = (sl - s_hi).astype(bf16).astype(f32)
    s_lo = (sl - s_hi - s_mid).astype(bf16).astype(f32)
    r = lax.broadcasted_iota(jnp.int32, (VT_PAD, tq), 0)
    piece = jnp.where((r == 0) | (r == 3), s_hi, jnp.where((r == 1) | (r == 4), s_mid, s_lo))
    srows = jnp.where(r < 3, piece, jnp.where(r < 6, piece * POS_RADIX, 0.0)).astype(bf16)
    saug = jnp.concatenate([srows, jnp.zeros((LANES - VT_PAD, tq), bf16)], axis=0)
    zhalf = jnp.zeros((DIFF_QKDIM, tq), bf16)

    def query_operands(qt):
        return (jnp.concatenate([qt[0:DIFF_QKDIM], zhalf, saug], axis=0),
                jnp.concatenate([zhalf, qt[DIFF_QKDIM:], saug], axis=0))

    @pl.when(qi == 0)
    def _():
        own = query_operands(qt_ref[...])
        qaug_ref[0] = own[0]
        qaug_ref[1] = own[1]

    qaug = (qaug_ref[0], qaug_ref[1])
    qaug_next = query_operands(qtn_ref[...])

    m_ref[...] = jnp.full(m_ref.shape, -jnp.inf, f32)
    acc_ref[...] = jnp.zeros(acc_ref.shape, f32)
    qpos = lax.broadcasted_iota(jnp.int32, (1, tq), 1).astype(f32)

    def scores(c, u0, n_units, mask_last, next_tile=False):
        qop = qaug_next[c] if next_tile else qaug[c]
        mx = None
        for u in range(n_units):
            k0 = _aligned((u0 + u) * tk, tk)
            t = jnp.dot(kaug_ref[pl.ds(k0, tk), :], qop, preferred_element_type=f32)
            if mask_last and u == n_units - 1:
                t = t + mask_ref[...]
            t_ref[c, u * tk:(u + 1) * tk, :] = t
            tm = jnp.max(t, axis=0, keepdims=True)
            mx = tm if mx is None else jnp.maximum(mx, tm)
        tmax_ref[c] = mx

    def softmax_pv(c, u0, n_units):
        k0 = _aligned(u0 * tk, tk)
        nk = n_units * tk
        cq = slope * ((u0 * tk - qi * tq).astype(f32) - qpos)
        m_old = m_ref[c]
        m_new = jnp.maximum(m_old, tmax_ref[c] + cq)
        p = jnp.exp2(t_ref[c, 0:nk, :] - (m_new - cq))
        alpha = jnp.exp2(m_old - m_new)
        acc_ref[c] = alpha * acc_ref[c] + jnp.dot(vta_ref[:, pl.ds(k0, nk)], p.astype(bf16),
                                                  preferred_element_type=f32)
        m_ref[c] = m_new

    units = ATT_ITEM // tk
    nb = qi // units
    tail0 = nb * units

    def run(n_tail):
        @pl.when((qi % units == n_tail - 1) & (nb == 0))
        def _():
            first_tile = n_tail == 1
            if first_tile:
                scores(0, 0, n_tail, True)
            scores(1, 0, n_tail, True)
            softmax_pv(0, 0, n_tail)
            scores(0, 0, units, first_tile, next_tile=True)
            softmax_pv(1, 0, n_tail)

        @pl.when((qi % units == n_tail - 1) & (nb > 0))
        def _():
            def body(j, carry):
                scores(1, j * units, units, False)
                softmax_pv(0, j * units, units)
                scores(0, (j + 1) * units, units, False)
                softmax_pv(1, j * units, units)
                return carry

            lax.fori_loop(0, nb - 1, body, 0)
            last = tail0 - units
            scores(1, last, units, False)
            softmax_pv(0, last, units)
            scores(0, tail0, n_tail, True)
            softmax_pv(1, last, units)
            scores(1, tail0, n_tail, True)
            softmax_pv(0, tail0, n_tail)
            scores(0, 0, units, False, next_tile=True)
            softmax_pv(1, tail0, n_tail)

    for n_tail in range(1, units + 1):
        run(n_tail)

    qaug_ref[0] = qaug_next[0]
    qaug_ref[1] = qaug_next[1]

    lam = (jnp.exp(jnp.sum(lq1_ref[...] * lk1_ref[...], axis=-1, keepdims=True))
           - jnp.exp(jnp.sum(lq2_ref[...] * lk2_ref[...], axis=-1, keepdims=True)) + lam_init)
    a0 = acc_ref[0]
    a1 = acc_ref[1]
    o = (a0[0:DIFF_VDIM] / a0[DIFF_VDIM:DIFF_VDIM + 1]
         - lam * (a1[0:DIFF_VDIM] / a1[DIFF_VDIM:DIFF_VDIM + 1]))
    ms = jnp.mean(o * o, axis=0, keepdims=True)
    on = (o * lax.rsqrt(ms + RMS_EPS)).T
    y = on * (sg_ref[...] * (1.0 - lam_init)) * gd_ref[...]
    o_ref[...] = y.astype(bf16)


def _attention(qt, k, vt, gd, lq1, lk1, lq2, lk2, sg, lam_init):
    B, S, _ = k.shape
    assert ATT_TQ == ATT_TK and ATT_ITEM == 2 * ATT_TK and S % ATT_ITEM == 0
    tq = ATT_TQ
    n_q = S // tq
    grid = (B, DIFF_HEADS, n_q)
    vec = lambda b, h, i: (0, 0)
    in_specs = [
        pl.BlockSpec((None, DIFF_VDIM, tq), lambda b, h, i: (b, h, i)),
        pl.BlockSpec((None, DIFF_VDIM, tq), lambda b, h, i: (b, h, jnp.minimum(i + 1, n_q - 1))),
        pl.BlockSpec((None, S, DIFF_VDIM), lambda b, h, i: (b, 0, h)),
        pl.BlockSpec((None, DIFF_VDIM, S), lambda b, h, i: (b, h, 0)),
        pl.BlockSpec((None, tq, DIFF_VDIM), lambda b, h, i: (b, i, h)),
        pl.BlockSpec((1, DIFF_QKDIM), vec),
        pl.BlockSpec((1, DIFF_QKDIM), vec),
        pl.BlockSpec((1, DIFF_QKDIM), vec),
        pl.BlockSpec((1, DIFF_QKDIM), vec),
        pl.BlockSpec((1, DIFF_VDIM), vec),
    ]
    return pl.pallas_call(
        functools.partial(_attn_kernel, lam_init=lam_init),
        grid=grid,
        in_specs=in_specs,
        out_specs=pl.BlockSpec((None, tq, DIFF_VDIM), lambda b, h, i: (b, i, h)),
        out_shape=jax.ShapeDtypeStruct((B, S, DIFF_WIDTH), jnp.bfloat16),
        scratch_shapes=[
            pltpu.VMEM((2, DIFF_VDIM + VT_PAD, tq), jnp.float32),
            pltpu.VMEM((2, 1, tq), jnp.float32),
            pltpu.VMEM((ATT_TK, tq), jnp.float32),
            pltpu.VMEM((2, ATT_ITEM, tq), jnp.float32),
            pltpu.VMEM((2, 1, tq), jnp.float32),
            pltpu.VMEM((S, 2 * LANES), jnp.bfloat16),
            pltpu.VMEM((DIFF_VDIM + VT_PAD, S), jnp.bfloat16),
            pltpu.VMEM((2, 2 * LANES, tq), jnp.bfloat16),
        ],
        compiler_params=pltpu.CompilerParams(
            dimension_semantics=("arbitrary", "arbitrary", "arbitrary"),
            vmem_limit_bytes=VMEM_LIMIT),
        name="diff_attn",
    )(qt, qt, k, vt, gd, lq1, lk1, lq2, lk2, sg)


def _out_kernel(x_ref, yc_ref, yd_ref, w_ref, g_ref, o_ref):
    y = (jnp.dot(yc_ref[...], w_ref[0:CONV_WIDTH, :], preferred_element_type=jnp.float32)
         + jnp.dot(yd_ref[...], w_ref[CONV_WIDTH:, :], preferred_element_type=jnp.float32))
    ms = jnp.mean(y * y, axis=-1, keepdims=True)
    o_ref[...] = x_ref[...] + y * lax.rsqrt(ms + RMS_EPS) * g_ref[...]


def _out_proj(x, yc, yd, w_out_b, post_g):
    B, S, D = x.shape
    ts = OUT_TILE
    assert S % ts == 0
    row = lambda b, s: (b, s, 0)
    const2 = lambda b, s: (0, 0)
    return pl.pallas_call(
        _out_kernel,
        grid=(B, S // ts),
        in_specs=[
            pl.BlockSpec((None, ts, D), row),
            pl.BlockSpec((None, ts, CONV_WIDTH), row),
            pl.BlockSpec((None, ts, DIFF_WIDTH), row),
            pl.BlockSpec(w_out_b.shape, const2),
            pl.BlockSpec((1, D), const2),
        ],
        out_specs=pl.BlockSpec((None, ts, D), row),
        out_shape=jax.ShapeDtypeStruct((B, S, D), jnp.float32),
        compiler_params=pltpu.CompilerParams(
            dimension_semantics=("arbitrary", "arbitrary"),
            vmem_limit_bytes=VMEM_LIMIT),
        name="out_proj",
    )(x, yc, yd, w_out_b, post_g)


def kernel(x, pre_norm_g, w_in, conv_dw_w, conv_dw_b, conv_ln_g, conv_ln_b, conv_pw_w, conv_pw_b,
           lambda_q1, lambda_k1, lambda_q2, lambda_k2, diff_subln_g, w_out, post_norm_g):
    depth = pre_norm_g.shape[0]
    h = x
    for i in range(depth):
        w_in_b = w_in[i].astype(jnp.bfloat16)
        w_qt_b = w_in[i][:, C_Q:C_Q + DIFF_WIDTH].T.astype(jnp.bfloat16)
        w_vt_b = w_in[i][:, C_V:C_V + DIFF_WIDTH].T.astype(jnp.bfloat16)
        yc, qt, k, vt, gd = _proj_conv(
            h, pre_norm_g[i][None], w_in_b, w_qt_b, w_vt_b, conv_dw_w[i], conv_dw_b[i][None],
            conv_ln_g[i][None], conv_ln_b[i][None], conv_pw_w[i].astype(jnp.bfloat16), conv_pw_b[i][None])
        yd = _attention(qt, k, vt, gd, lambda_q1[i][None], lambda_k1[i][None], lambda_q2[i][None],
                        lambda_k2[i][None], diff_subln_g[i][None], _lambda_init(i))
        h = _out_proj(h, yc, yd, w_out[i].astype(jnp.bfloat16), post_norm_g[i][None])
    return h
```

```python
import functools
import math

import jax
import jax.numpy as jnp
from jax import lax
from jax.experimental import pallas as pl
from jax.experimental.pallas import tpu as pltpu

D_MODEL = 1024
CONV_WIDTH = 512
CONV_KERNEL = 31
DIFF_WIDTH = 512
DIFF_HEADS = 4
DIFF_VDIM = 128
DIFF_QKDIM = 64
RMS_EPS = 1e-6
LN_EPS = 1e-5


def _lambda_init(layer_idx):
    return 0.8 - 0.6 * math.exp(-0.3 * layer_idx)


LOG2E = math.log2(math.e)
Q_PRESCALE = (DIFF_QKDIM ** -0.5) * LOG2E

C_GLU = 0
C_CGATE = 1024
C_Q = 1536
C_K = 2048
C_V = 2560
C_DGATE = 3072

PROJ_TILE = 512
OUT_TILE = 512
CONV_HALO = 32
CONV_ROWS = 256
LANES = 128
ATT_TQ = 512
ATT_TK = 512
ATT_ITEM = 1024
POS_SHIFT = 8
POS_RADIX = 1 << POS_SHIFT
VT_PAD = 16
VMEM_LIMIT = 48 * 1024 * 1024

_NT = (((1,), (1,)), ((), ()))


def _silu(x):
    return x * jax.nn.sigmoid(x)


def _aligned(start, multiple):
    return start if isinstance(start, int) else pl.multiple_of(start, multiple)


def _proj_conv_kernel(x_ref, g_ref, w_ref, wqt_ref, wvt_ref, dww_ref, dwb_ref, lng_ref, lnb_ref, pww_ref, pwb_ref,
                      yc_ref, qt_ref, k_ref, vt_ref, gd_ref, hbuf, cbuf):
    ts = x_ref.shape[0]
    s_idx = pl.program_id(1)

    x = x_ref[...]
    ms = jnp.mean(x * x, axis=-1, keepdims=True)
    xb = (x * lax.rsqrt(ms + RMS_EPS) * g_ref[...]).astype(jnp.bfloat16)

    def proj(c0, width):
        return jnp.dot(xb, w_ref[:, c0:c0 + width], preferred_element_type=jnp.float32)

    n_slab = CONV_WIDTH // LANES

    @pl.when(s_idx == 0)
    def _():
        hbuf[:, 0:CONV_HALO, :] = jnp.zeros((n_slab, CONV_HALO, LANES), jnp.float32)

    a = proj(C_GLU, CONV_WIDTH)
    b = proj(C_GLU + CONV_WIDTH, CONV_WIDTH)
    h = a * jax.nn.sigmoid(b)
    for l in range(n_slab):
        hbuf[l, CONV_HALO:CONV_HALO + ts, :] = h[:, l * LANES:(l + 1) * LANES]

    off = CONV_HALO - (CONV_KERNEL - 1)
    half = CONV_ROWS // 2

    def conv_block(l, r0):
        lanes = slice(l * LANES, (l + 1) * LANES)
        acc = [jnp.broadcast_to(dwb_ref[:, lanes], (half, LANES)) for _ in range(2)]
        for tau in range(CONV_KERNEL):
            w = dww_ref[tau:tau + 1, lanes]
            for par in range(2):
                acc[par] = acc[par] + w * hbuf[l, pl.ds(r0 + off + tau + par, half, stride=2), :]
        for par in range(2):
            cbuf[l, pl.ds(r0 + par, half, stride=2), :] = acc[par]

    qt_ref[...] = (lax.dot_general(wqt_ref[...], xb, _NT, preferred_element_type=jnp.float32)
                   * Q_PRESCALE).astype(jnp.bfloat16)
    k_ref[...] = proj(C_K, DIFF_WIDTH).astype(jnp.bfloat16)
    vt_ref[...] = lax.dot_general(wvt_ref[...], xb, _NT,
                                  preferred_element_type=jnp.float32).astype(jnp.bfloat16)
    gd_ref[...] = _silu(proj(C_DGATE, DIFF_WIDTH))
    gate_c = _silu(proj(C_CGATE, CONV_WIDTH))

    for r0 in range(0, ts, CONV_ROWS):
        for l in range(n_slab):
            conv_block(l, r0)

    hbuf[:, 0:CONV_HALO, :] = hbuf[:, ts:ts + CONV_HALO, :]

    cv = jnp.concatenate([cbuf[l] for l in range(n_slab)], axis=-1)
    mu = jnp.mean(cv, axis=-1, keepdims=True)
    d = cv - mu
    var = jnp.mean(d * d, axis=-1, keepdims=True)
    hn = d * lax.rsqrt(var + LN_EPS) * lng_ref[...] + lnb_ref[...]
    hs = _silu(hn).astype(jnp.bfloat16)
    yc = jnp.dot(hs, pww_ref[...], preferred_element_type=jnp.float32) + pwb_ref[...]
    yc_ref[...] = (yc * gate_c).astype(jnp.bfloat16)


def _proj_conv(x, pre_g, w_in_b, w_qt_b, w_vt_b, dw_w, dw_b, ln_g, ln_b, pw_w_b, pw_b):
    B, S, D = x.shape
    ts = PROJ_TILE
    assert S % ts == 0 and ts % CONV_ROWS == 0
    grid = (B, S // ts)
    row = lambda b, s: (b, s, 0)
    const2 = lambda b, s: (0, 0)
    in_specs = [
        pl.BlockSpec((None, ts, D), row),
        pl.BlockSpec((1, D), const2),
        pl.BlockSpec(w_in_b.shape, const2),
        pl.BlockSpec(w_qt_b.shape, const2),
        pl.BlockSpec(w_vt_b.shape, const2),
        pl.BlockSpec(dw_w.shape, const2),
        pl.BlockSpec((1, CONV_WIDTH), const2),
        pl.BlockSpec((1, CONV_WIDTH), const2),
        pl.BlockSpec((1, CONV_WIDTH), const2),
        pl.BlockSpec(pw_w_b.shape, const2),
        pl.BlockSpec((1, CONV_WIDTH), const2),
    ]
    out_specs = [
        pl.BlockSpec((None, ts, CONV_WIDTH), row),
        pl.BlockSpec((None, DIFF_WIDTH, ts), lambda b, s: (b, 0, s)),
        pl.BlockSpec((None, ts, DIFF_WIDTH), row),
        pl.BlockSpec((None, DIFF_WIDTH, ts), lambda b, s: (b, 0, s)),
        pl.BlockSpec((None, ts, DIFF_WIDTH), row),
    ]
    out_shape = [
        jax.ShapeDtypeStruct((B, S, CONV_WIDTH), jnp.bfloat16),
        jax.ShapeDtypeStruct((B, DIFF_WIDTH, S), jnp.bfloat16),
        jax.ShapeDtypeStruct((B, S, DIFF_WIDTH), jnp.bfloat16),
        jax.ShapeDtypeStruct((B, DIFF_WIDTH, S), jnp.bfloat16),
        jax.ShapeDtypeStruct((B, S, DIFF_WIDTH), jnp.float32),
    ]
    return pl.pallas_call(
        _proj_conv_kernel,
        grid=grid,
        in_specs=in_specs,
        out_specs=out_specs,
        out_shape=out_shape,
        scratch_shapes=[
            pltpu.VMEM((CONV_WIDTH // LANES, CONV_HALO + ts, LANES), jnp.float32),
            pltpu.VMEM((CONV_WIDTH // LANES, ts, LANES), jnp.float32),
        ],
        compiler_params=pltpu.CompilerParams(
            dimension_semantics=("arbitrary", "arbitrary"),
            vmem_limit_bytes=VMEM_LIMIT),
        name="proj_conv",
    )(x, pre_g, w_in_b, w_qt_b, w_vt_b, dw_w, dw_b, ln_g, ln_b, pw_w_b, pw_b)


def _attn_kernel(qt_ref, k_ref, vt_ref, gd_ref, lq1_ref, lk1_ref, lq2_ref, lk2_ref, sg_ref,
                 o_ref, acc_ref, m_ref, mask_ref, t_ref, tmax_ref, kaug_ref, vta_ref, qaug_ref,
                 *, lam_init):
    tq = ATT_TQ
    tk = ATT_TK
    seq = k_ref.shape[0]
    n_q = seq // tq
    h = pl.program_id(1)
    f32, bf16 = jnp.float32, jnp.bfloat16

    hv = jnp.full((1, 1), h, jnp.int32).astype(f32)
    slope = jnp.exp2(-(8.0 / DIFF_HEADS) * (hv + 1.0)) * LOG2E

    kaug_ref[:, 0:DIFF_VDIM] = k_ref[...]
    jj = lax.broadcasted_iota(jnp.int32, (ATT_ITEM, LANES), 0)
    ln = lax.broadcasted_iota(jnp.int32, (ATT_ITEM, LANES), 1)
    feat = jnp.where(ln < 3, jj & (POS_RADIX - 1), jnp.where(ln < 6, jj >> POS_SHIFT, 0))
    feat = feat.astype(f32).astype(bf16)
    for blk in range(seq // ATT_ITEM):
        kaug_ref[blk * ATT_ITEM:(blk + 1) * ATT_ITEM, DIFF_VDIM:] = feat
    vta_ref[0:DIFF_VDIM, :] = vt_ref[...]
    row = lax.broadcasted_iota(jnp.int32, (VT_PAD, seq), 0)
    vta_ref[DIFF_VDIM:, :] = jnp.where(row == 0, 1.0, 0.0).astype(bf16)
    kk = lax.broadcasted_iota(jnp.int32, (tk, tq), 0)
    ii = lax.broadcasted_iota(jnp.int32, (tk, tq), 1)
    mask_ref[...] = jnp.where(kk > ii, -jnp.inf, 0.0).astype(f32)

    sl = jnp.broadcast_to(slope, (1, tq))
    s_hi = sl.astype(bf16).astype(f32)
    s_mid = (sl - s_hi).astype(bf16).astype(f32)
    s_lo = (sl - s_hi - s_mid).astype(bf16).astype(f32)
    r = lax.broadcasted_iota(jnp.int32, (VT_PAD, tq), 0)
    piece = jnp.where((r == 0) | (r == 3), s_hi, jnp.where((r == 1) | (r == 4), s_mid, s_lo))
    srows = jnp.where(r < 3, piece, jnp.where(r < 6, piece * POS_RADIX, 0.0)).astype(bf16)
    saug = jnp.concatenate([srows, jnp.zeros((LANES - VT_PAD, tq), bf16)], axis=0)
    zhalf = jnp.zeros((DIFF_QKDIM, tq), bf16)

    def query_operands(qt):
        return (jnp.concatenate([qt[0:DIFF_QKDIM], zhalf, saug], axis=0),
                jnp.concatenate([zhalf, qt[DIFF_QKDIM:], saug], axis=0))

    own = query_operands(qt_ref[:, 0:tq])
    qaug_ref[0] = own[0]
    qaug_ref[1] = own[1]
    qpos = lax.broadcasted_iota(jnp.int32, (1, tq), 1).astype(f32)
    lam = (jnp.exp(jnp.sum(lq1_ref[...] * lk1_ref[...], axis=-1, keepdims=True))
           - jnp.exp(jnp.sum(lq2_ref[...] * lk2_ref[...], axis=-1, keepdims=True)) + lam_init)

    def tile(qi, carry):
        _tile(qi)
        return carry

    def _tile(qi):
        qaug = (qaug_ref[0], qaug_ref[1])
        q_next0 = _aligned(jnp.minimum(qi + 1, n_q - 1) * tq, tq)
        qaug_next = query_operands(qt_ref[:, pl.ds(q_next0, tq)])

        m_ref[...] = jnp.full(m_ref.shape, -jnp.inf, f32)
        acc_ref[...] = jnp.zeros(acc_ref.shape, f32)

        def scores(c, u0, n_units, mask_last, next_tile=False):
            qop = qaug_next[c] if next_tile else qaug[c]
            mx = None
            for u in range(n_units):
                k0 = _aligned((u0 + u) * tk, tk)
                t = jnp.dot(kaug_ref[pl.ds(k0, tk), :], qop, preferred_element_type=f32)
                if mask_last and u == n_units - 1:
                    t = t + mask_ref[...]
                t_ref[c, u * tk:(u + 1) * tk, :] = t
                tm = jnp.max(t, axis=0, keepdims=True)
                mx = tm if mx is None else jnp.maximum(mx, tm)
            tmax_ref[c] = mx

        def softmax_pv(c, u0, n_units):
            k0 = _aligned(u0 * tk, tk)
            nk = n_units * tk
            cq = slope * ((u0 * tk - qi * tq).astype(f32) - qpos)
            m_old = m_ref[c]
            m_new = jnp.maximum(m_old, tmax_ref[c] + cq)
            p = jnp.exp2(t_ref[c, 0:nk, :] - (m_new - cq))
            alpha = jnp.exp2(m_old - m_new)
            acc_ref[c] = alpha * acc_ref[c] + jnp.dot(vta_ref[:, pl.ds(k0, nk)], p.astype(bf16),
                                                      preferred_element_type=f32)
            m_ref[c] = m_new

        units = ATT_ITEM // tk
        nb = qi // units
        tail0 = nb * units

        def run(n_tail):
            @pl.when((qi % units == n_tail - 1) & (nb == 0))
            def _():
                first_tile = n_tail == 1
                if first_tile:
                    scores(0, 0, n_tail, True)
                scores(1, 0, n_tail, True)
                softmax_pv(0, 0, n_tail)
                scores(0, 0, units, first_tile, next_tile=True)
                softmax_pv(1, 0, n_tail)

            @pl.when((qi % units == n_tail - 1) & (nb > 0))
            def _():
                def body(j, carry):
                    scores(1, j * units, units, False)
                    softmax_pv(0, j * units, units)
                    scores(0, (j + 1) * units, units, False)
                    softmax_pv(1, j * units, units)
                    return carry

                lax.fori_loop(0, nb - 1, body, 0)
                last = tail0 - units
                scores(1, last, units, False)
                softmax_pv(0, last, units)
                scores(0, tail0, n_tail, True)
                softmax_pv(1, last, units)
                scores(1, tail0, n_tail, True)
                softmax_pv(0, tail0, n_tail)
                scores(0, 0, units, False, next_tile=True)
                softmax_pv(1, tail0, n_tail)

        for n_tail in range(1, units + 1):
            run(n_tail)

        qaug_ref[0] = qaug_next[0]
        qaug_ref[1] = qaug_next[1]

        a0 = acc_ref[0]
        a1 = acc_ref[1]
        o = (a0[0:DIFF_VDIM] / a0[DIFF_VDIM:DIFF_VDIM + 1]
             - lam * (a1[0:DIFF_VDIM] / a1[DIFF_VDIM:DIFF_VDIM + 1]))
        ms = jnp.mean(o * o, axis=0, keepdims=True)
        on = (o * lax.rsqrt(ms + RMS_EPS)).T
        q0 = _aligned(qi * tq, tq)
        y = on * (sg_ref[...] * (1.0 - lam_init)) * gd_ref[pl.ds(q0, tq), :]
        o_ref[pl.ds(q0, tq), :] = y.astype(bf16)

    lax.fori_loop(0, n_q, tile, 0)


def _attention(qt, k, vt, gd, lq1, lk1, lq2, lk2, sg, lam_init):
    B, S, _ = k.shape
    assert ATT_TQ == ATT_TK and ATT_ITEM == 2 * ATT_TK and S % ATT_ITEM == 0
    tq = ATT_TQ
    grid = (B, DIFF_HEADS)
    vec = lambda b, h: (0, 0)
    rows = lambda b, h: (b, 0, h)
    cols = lambda b, h: (b, h, 0)
    in_specs = [
        pl.BlockSpec((None, DIFF_VDIM, S), cols),
        pl.BlockSpec((None, S, DIFF_VDIM), rows),
        pl.BlockSpec((None, DIFF_VDIM, S), cols),
        pl.BlockSpec((None, S, DIFF_VDIM), rows),
        pl.BlockSpec((1, DIFF_QKDIM), vec),
        pl.BlockSpec((1, DIFF_QKDIM), vec),
        pl.BlockSpec((1, DIFF_QKDIM), vec),
        pl.BlockSpec((1, DIFF_QKDIM), vec),
        pl.BlockSpec((1, DIFF_VDIM), vec),
    ]
    return pl.pallas_call(
        functools.partial(_attn_kernel, lam_init=lam_init),
        grid=grid,
        in_specs=in_specs,
        out_specs=pl.BlockSpec((None, S, DIFF_VDIM), rows),
        out_shape=jax.ShapeDtypeStruct((B, S, DIFF_WIDTH), jnp.bfloat16),
        scratch_shapes=[
            pltpu.VMEM((2, DIFF_VDIM + VT_PAD, tq), jnp.float32),
            pltpu.VMEM((2, 1, tq), jnp.float32),
            pltpu.VMEM((ATT_TK, tq), jnp.float32),
            pltpu.VMEM((2, ATT_ITEM, tq), jnp.float32),
            pltpu.VMEM((2, 1, tq), jnp.float32),
            pltpu.VMEM((S, 2 * LANES), jnp.bfloat16),
            pltpu.VMEM((DIFF_VDIM + VT_PAD, S), jnp.bfloat16),
            pltpu.VMEM((2, 2 * LANES, tq), jnp.bfloat16),
        ],
        compiler_params=pltpu.CompilerParams(
            dimension_semantics=("arbitrary", "arbitrary"),
            vmem_limit_bytes=VMEM_LIMIT),
        name="diff_attn",
    )(qt, k, vt, gd, lq1, lk1, lq2, lk2, sg)


def _out_kernel(x_ref, yc_ref, yd_ref, w_ref, g_ref, o_ref):
    y = (jnp.dot(yc_ref[...], w_ref[0:CONV_WIDTH, :], preferred_element_type=jnp.float32)
         + jnp.dot(yd_ref[...], w_ref[CONV_WIDTH:, :], preferred_element_type=jnp.float32))
    ms = jnp.mean(y * y, axis=-1, keepdims=True)
    o_ref[...] = x_ref[...] + y * lax.rsqrt(ms + RMS_EPS) * g_ref[...]


def _out_proj(x, yc, yd, w_out_b, post_g):
    B, S, D = x.shape
    ts = OUT_TILE
    assert S % ts == 0
    row = lambda b, s: (b, s, 0)
    const2 = lambda b, s: (0, 0)
    return pl.pallas_call(
        _out_kernel,
        grid=(B, S // ts),
        in_specs=[
            pl.BlockSpec((None, ts, D), row),
            pl.BlockSpec((None, ts, CONV_WIDTH), row),
            pl.BlockSpec((None, ts, DIFF_WIDTH), row),
            pl.BlockSpec(w_out_b.shape, const2),
            pl.BlockSpec((1, D), const2),
        ],
        out_specs=pl.BlockSpec((None, ts, D), row),
        out_shape=jax.ShapeDtypeStruct((B, S, D), jnp.float32),
        compiler_params=pltpu.CompilerParams(
            dimension_semantics=("arbitrary", "arbitrary"),
            vmem_limit_bytes=VMEM_LIMIT),
        name="out_proj",
    )(x, yc, yd, w_out_b, post_g)


def kernel(x, pre_norm_g, w_in, conv_dw_w, conv_dw_b, conv_ln_g, conv_ln_b, conv_pw_w, conv_pw_b,
           lambda_q1, lambda_k1, lambda_q2, lambda_k2, diff_subln_g, w_out, post_norm_g):
    depth = pre_norm_g.shape[0]
    h = x
    for i in range(depth):
        w_in_b = w_in[i].astype(jnp.bfloat16)
        w_qt_b = w_in[i][:, C_Q:C_Q + DIFF_WIDTH].T.astype(jnp.bfloat16)
        w_vt_b = w_in[i][:, C_V:C_V + DIFF_WIDTH].T.astype(jnp.bfloat16)
        yc, qt, k, vt, gd = _proj_conv(
            h, pre_norm_g[i][None], w_in_b, w_qt_b, w_vt_b, conv_dw_w[i], conv_dw_b[i][None],
            conv_ln_g[i][None], conv_ln_b[i][None], conv_pw_w[i].astype(jnp.bfloat16), conv_pw_b[i][None])
        yd = _attention(qt, k, vt, gd, lambda_q1[i][None], lambda_k1[i][None], lambda_q2[i][None],
                        lambda_k2[i][None], diff_subln_g[i][None], _lambda_init(i))
        h = _out_proj(h, yc, yd, w_out[i].astype(jnp.bfloat16), post_norm_g[i][None])
    return h
```

```python
import functools
import math

import jax
import jax.numpy as jnp
from jax import lax
from jax.experimental import pallas as pl
from jax.experimental.pallas import tpu as pltpu

D_MODEL = 1024
CONV_WIDTH = 512
CONV_KERNEL = 31
DIFF_WIDTH = 512
DIFF_HEADS = 4
DIFF_VDIM = 128
DIFF_QKDIM = 64
RMS_EPS = 1e-6
LN_EPS = 1e-5


def _lambda_init(layer_idx):
    return 0.8 - 0.6 * math.exp(-0.3 * layer_idx)


LOG2E = math.log2(math.e)
Q_PRESCALE = (DIFF_QKDIM ** -0.5) * LOG2E

C_GLU = 0
C_CGATE = 1024
C_Q = 1536
C_K = 2048
C_V = 2560
C_DGATE = 3072

PROJ_TILE = 512
OUT_TILE = 512
CONV_HALO = 32
CONV_ROWS = 256
LANES = 128
ATT_TQ = 512
ATT_TK = 512
ATT_ITEM = 1024
POS_SHIFT = 8
POS_RADIX = 1 << POS_SHIFT
VT_PAD = 16
VMEM_LIMIT = 48 * 1024 * 1024

_NT = (((1,), (1,)), ((), ()))


def _silu(x):
    return x * jax.nn.sigmoid(x)


def _aligned(start, multiple):
    return start if isinstance(start, int) else pl.multiple_of(start, multiple)


def _proj_conv_kernel(x_ref, g_ref, w_ref, wqt_ref, wvt_ref, dww_ref, dwb_ref, lng_ref, lnb_ref, pww_ref, pwb_ref,
                      yc_ref, qt_ref, k_ref, vt_ref, gd_ref, hbuf, cbuf):
    ts = x_ref.shape[0]
    s_idx = pl.program_id(1)

    x = x_ref[...]
    ms = jnp.mean(x * x, axis=-1, keepdims=True)
    xb = (x * lax.rsqrt(ms + RMS_EPS) * g_ref[...]).astype(jnp.bfloat16)

    def proj(c0, width):
        return jnp.dot(xb, w_ref[:, c0:c0 + width], preferred_element_type=jnp.float32)

    n_slab = CONV_WIDTH // LANES

    @pl.when(s_idx == 0)
    def _():
        hbuf[:, 0:CONV_HALO, :] = jnp.zeros((n_slab, CONV_HALO, LANES), jnp.float32)

    a = proj(C_GLU, CONV_WIDTH)
    b = proj(C_GLU + CONV_WIDTH, CONV_WIDTH)
    h = a * jax.nn.sigmoid(b)
    for l in range(n_slab):
        hbuf[l, CONV_HALO:CONV_HALO + ts, :] = h[:, l * LANES:(l + 1) * LANES]

    off = CONV_HALO - (CONV_KERNEL - 1)
    half = CONV_ROWS // 2

    def conv_block(l, r0):
        lanes = slice(l * LANES, (l + 1) * LANES)
        acc = [jnp.broadcast_to(dwb_ref[:, lanes], (half, LANES)) for _ in range(2)]
        for tau in range(CONV_KERNEL):
            w = dww_ref[tau:tau + 1, lanes]
            for par in range(2):
                acc[par] = acc[par] + w * hbuf[l, pl.ds(r0 + off + tau + par, half, stride=2), :]
        for par in range(2):
            cbuf[l, pl.ds(r0 + par, half, stride=2), :] = acc[par]

    qt_ref[...] = (lax.dot_general(wqt_ref[...], xb, _NT, preferred_element_type=jnp.float32)
                   * Q_PRESCALE).astype(jnp.bfloat16)
    k_ref[...] = proj(C_K, DIFF_WIDTH).astype(jnp.bfloat16)
    vt_ref[...] = lax.dot_general(wvt_ref[...], xb, _NT,
                                  preferred_element_type=jnp.float32).astype(jnp.bfloat16)
    gd_ref[...] = _silu(proj(C_DGATE, DIFF_WIDTH))
    gate_c = _silu(proj(C_CGATE, CONV_WIDTH))

    for r0 in range(0, ts, CONV_ROWS):
        for l in range(n_slab):
            conv_block(l, r0)

    hbuf[:, 0:CONV_HALO, :] = hbuf[:, ts:ts + CONV_HALO, :]

    cv = jnp.concatenate([cbuf[l] for l in range(n_slab)], axis=-1)
    mu = jnp.mean(cv, axis=-1, keepdims=True)
    d = cv - mu
    var = jnp.mean(d * d, axis=-1, keepdims=True)
    hn = d * lax.rsqrt(var + LN_EPS) * lng_ref[...] + lnb_ref[...]
    hs = _silu(hn).astype(jnp.bfloat16)
    yc = jnp.dot(hs, pww_ref[...], preferred_element_type=jnp.float32) + pwb_ref[...]
    yc_ref[...] = (yc * gate_c).astype(jnp.bfloat16)


def _proj_conv(x, pre_g, w_in_b, w_qt_b, w_vt_b, dw_w, dw_b, ln_g, ln_b, pw_w_b, pw_b):
    B, S, D = x.shape
    ts = PROJ_TILE
    assert S % ts == 0 and ts % CONV_ROWS == 0
    grid = (B, S // ts)
    row = lambda b, s: (b, s, 0)
    const2 = lambda b, s: (0, 0)
    in_specs = [
        pl.BlockSpec((None, ts, D), row),
        pl.BlockSpec((1, D), const2),
        pl.BlockSpec(w_in_b.shape, const2),
        pl.BlockSpec(w_qt_b.shape, const2),
        pl.BlockSpec(w_vt_b.shape, const2),
        pl.BlockSpec(dw_w.shape, const2),
        pl.BlockSpec((1, CONV_WIDTH), const2),
        pl.BlockSpec((1, CONV_WIDTH), const2),
        pl.BlockSpec((1, CONV_WIDTH), const2),
        pl.BlockSpec(pw_w_b.shape, const2),
        pl.BlockSpec((1, CONV_WIDTH), const2),
    ]
    out_specs = [
        pl.BlockSpec((None, ts, CONV_WIDTH), row),
        pl.BlockSpec((None, DIFF_WIDTH, ts), lambda b, s: (b, 0, s)),
        pl.BlockSpec((None, ts, DIFF_WIDTH), row),
        pl.BlockSpec((None, DIFF_WIDTH, ts), lambda b, s: (b, 0, s)),
        pl.BlockSpec((None, ts, DIFF_WIDTH), row),
    ]
    out_shape = [
        jax.ShapeDtypeStruct((B, S, CONV_WIDTH), jnp.bfloat16),
        jax.ShapeDtypeStruct((B, DIFF_WIDTH, S), jnp.bfloat16),
        jax.ShapeDtypeStruct((B, S, DIFF_WIDTH), jnp.bfloat16),
        jax.ShapeDtypeStruct((B, DIFF_WIDTH, S), jnp.bfloat16),
        jax.ShapeDtypeStruct((B, S, DIFF_WIDTH), jnp.float32),
    ]
    return pl.pallas_call(
        _proj_conv_kernel,
        grid=grid,
        in_specs=in_specs,
        out_specs=out_specs,
        out_shape=out_shape,
        scratch_shapes=[
            pltpu.VMEM((CONV_WIDTH // LANES, CONV_HALO + ts, LANES), jnp.float32),
            pltpu.VMEM((CONV_WIDTH // LANES, ts, LANES), jnp.float32),
        ],
        compiler_params=pltpu.CompilerParams(
            dimension_semantics=("arbitrary", "arbitrary"),
            vmem_limit_bytes=VMEM_LIMIT),
        name="proj_conv",
    )(x, pre_g, w_in_b, w_qt_b, w_vt_b, dw_w, dw_b, ln_g, ln_b, pw_w_b, pw_b)


def _attn_kernel(qt_ref, k_ref, vt_ref, gd_ref, lq1_ref, lk1_ref, lq2_ref, lk2_ref, sg_ref,
                 o_ref, acc_ref, m_ref, mask_ref, t_ref, tmax_ref, kaug_ref, vta_ref, qaug_ref,
                 *, lam_init):
    tq = ATT_TQ
    tk = ATT_TK
    seq = k_ref.shape[0]
    n_q = seq // tq
    h = pl.program_id(1)
    f32, bf16 = jnp.float32, jnp.bfloat16

    hv = jnp.full((1, 1), h, jnp.int32).astype(f32)
    slope = jnp.exp2(-(8.0 / DIFF_HEADS) * (hv + 1.0)) * LOG2E

    kaug_ref[:, 0:DIFF_VDIM] = k_ref[...]
    jj = lax.broadcasted_iota(jnp.int32, (ATT_ITEM, LANES), 0)
    ln = lax.broadcasted_iota(jnp.int32, (ATT_ITEM, LANES), 1)
    feat = jnp.where(ln < 3, jj & (POS_RADIX - 1), jnp.where(ln < 6, jj >> POS_SHIFT, 0))
    feat = feat.astype(f32).astype(bf16)
    for blk in range(seq // ATT_ITEM):
        kaug_ref[blk * ATT_ITEM:(blk + 1) * ATT_ITEM, DIFF_VDIM:] = feat
    vta_ref[0:DIFF_VDIM, :] = vt_ref[...]
    row = lax.broadcasted_iota(jnp.int32, (VT_PAD, seq), 0)
    vta_ref[DIFF_VDIM:, :] = jnp.where(row == 0, 1.0, 0.0).astype(bf16)
    kk = lax.broadcasted_iota(jnp.int32, (tk, tq), 0)
    ii = lax.broadcasted_iota(jnp.int32, (tk, tq), 1)
    mask_ref[...] = jnp.where(kk > ii, -jnp.inf, 0.0).astype(f32)

    sl = jnp.broadcast_to(slope, (1, tq))
    s_hi = sl.astype(bf16).astype(f32)
    s_mid = (sl - s_hi).astype(bf16).astype(f32)
    s_lo = (sl - s_hi - s_mid).astype(bf16).astype(f32)
    r = lax.broadcasted_iota(jnp.int32, (VT_PAD, tq), 0)
    piece = jnp.where((r == 0) | (r == 3), s_hi, jnp.where((r == 1) | (r == 4), s_mid, s_lo))
    srows = jnp.where(r < 3, piece, jnp.where(r < 6, piece * POS_RADIX, 0.0)).astype(bf16)
    saug = jnp.concatenate([srows, jnp.zeros((LANES - VT_PAD, tq), bf16)], axis=0)
    zhalf = jnp.zeros((DIFF_QKDIM, tq), bf16)

    def query_operands(qt):
        return (jnp.concatenate([qt[0:DIFF_QKDIM], zhalf, saug], axis=0),
                jnp.concatenate([zhalf, qt[DIFF_QKDIM:], saug], axis=0))

    own = query_operands(qt_ref[:, 0:tq])
    qaug_ref[0] = own[0]
    qaug_ref[1] = own[1]
    qpos = lax.broadcasted_iota(jnp.int32, (1, tq), 1).astype(f32)
    lam = (jnp.exp(jnp.sum(lq1_ref[...] * lk1_ref[...], axis=-1, keepdims=True))
           - jnp.exp(jnp.sum(lq2_ref[...] * lk2_ref[...], axis=-1, keepdims=True)) + lam_init)

    def tile(qi, carry):
        _tile(qi)
        return carry

    def _tile(qi):
        qaug = (qaug_ref[0], qaug_ref[1])
        q_next0 = _aligned(jnp.minimum(qi + 1, n_q - 1) * tq, tq)
        qaug_next = query_operands(qt_ref[:, pl.ds(q_next0, tq)])

        m_ref[...] = jnp.full(m_ref.shape, -jnp.inf, f32)

        def scores(c, u0, n_units, mask_last, next_tile=False):
            qop = qaug_next[c] if next_tile else qaug[c]
            mx = None
            for u in range(n_units):
                k0 = _aligned((u0 + u) * tk, tk)
                t = jnp.dot(kaug_ref[pl.ds(k0, tk), :], qop, preferred_element_type=f32)
                if mask_last and u == n_units - 1:
                    t = t + mask_ref[...]
                t_ref[c, u * tk:(u + 1) * tk, :] = t
                tm = jnp.max(t, axis=0, keepdims=True)
                mx = tm if mx is None else jnp.maximum(mx, tm)
            tmax_ref[c] = mx

        def softmax_pv(par, c, u0, n_units):
            k0 = _aligned(u0 * tk, tk)
            nk = n_units * tk
            cq = slope * (jnp.asarray(u0 * tk - qi * tq, jnp.int32).astype(f32) - qpos)
            m_old = m_ref[c]
            m_new = jnp.maximum(m_old, tmax_ref[c] + cq)
            p = jnp.exp2(t_ref[c, 0:nk, :] - (m_new - cq))
            alpha = jnp.exp2(m_old - m_new)
            acc_ref[par, c] = alpha * acc_ref[par, c] + jnp.dot(
                vta_ref[:, pl.ds(k0, nk)], p.astype(bf16), preferred_element_type=f32)
            m_ref[c] = m_new

        units = ATT_ITEM // tk
        nb = qi // units
        tail0 = nb * units

        def run(n_tail):
            par = n_tail - 1

            @pl.when((qi % units == par) & (nb == 0))
            def _():
                first_tile = n_tail == 1
                acc_ref[par] = jnp.zeros(acc_ref.shape[1:], f32)
                if first_tile:
                    scores(0, 0, n_tail, True)
                scores(1, 0, n_tail, True)
                if not first_tile:
                    finish(qi - 1, 1 - par)
                softmax_pv(par, 0, 0, n_tail)
                scores(0, 0, units, first_tile, next_tile=True)
                softmax_pv(par, 1, 0, n_tail)

            @pl.when((qi % units == par) & (nb > 0))
            def _():
                acc_ref[par] = jnp.zeros(acc_ref.shape[1:], f32)

                def body(j, carry):
                    scores(1, j * units, units, False)
                    softmax_pv(par, 0, j * units, units)
                    scores(0, (j + 1) * units, units, False)
                    softmax_pv(par, 1, j * units, units)
                    return carry

                lax.fori_loop(0, nb - 1, body, 0)
                last = tail0 - units
                scores(1, last, units, False)
                finish(qi - 1, 1 - par)
                softmax_pv(par, 0, last, units)
                scores(0, tail0, n_tail, True)
                softmax_pv(par, 1, last, units)
                scores(1, tail0, n_tail, True)
                softmax_pv(par, 0, tail0, n_tail)
                scores(0, 0, units, False, next_tile=True)
                softmax_pv(par, 1, tail0, n_tail)

        for n_tail in range(1, units + 1):
            run(n_tail)

        qaug_ref[0] = qaug_next[0]
        qaug_ref[1] = qaug_next[1]

    def finish(q_tile, par):
        a0 = acc_ref[par, 0]
        a1 = acc_ref[par, 1]
        o = (a0[0:DIFF_VDIM] / a0[DIFF_VDIM:DIFF_VDIM + 1]
             - lam * (a1[0:DIFF_VDIM] / a1[DIFF_VDIM:DIFF_VDIM + 1]))
        ms = jnp.mean(o * o, axis=0, keepdims=True)
        on = (o * lax.rsqrt(ms + RMS_EPS)).T
        q0 = _aligned(q_tile * tq, tq)
        y = on * (sg_ref[...] * (1.0 - lam_init)) * gd_ref[pl.ds(q0, tq), :]
        o_ref[pl.ds(q0, tq), :] = y.astype(bf16)

    lax.fori_loop(0, n_q, tile, 0)
    finish(n_q - 1, (n_q - 1) % 2)


def _attention(qt, k, vt, gd, lq1, lk1, lq2, lk2, sg, lam_init):
    B, S, _ = k.shape
    assert ATT_TQ == ATT_TK and ATT_ITEM == 2 * ATT_TK and S % ATT_ITEM == 0
    tq = ATT_TQ
    grid = (B, DIFF_HEADS)
    vec = lambda b, h: (0, 0)
    rows = lambda b, h: (b, 0, h)
    cols = lambda b, h: (b, h, 0)
    in_specs = [
        pl.BlockSpec((None, DIFF_VDIM, S), cols),
        pl.BlockSpec((None, S, DIFF_VDIM), rows),
        pl.BlockSpec((None, DIFF_VDIM, S), cols),
        pl.BlockSpec((None, S, DIFF_VDIM), rows),
        pl.BlockSpec((1, DIFF_QKDIM), vec),
        pl.BlockSpec((1, DIFF_QKDIM), vec),
        pl.BlockSpec((1, DIFF_QKDIM), vec),
        pl.BlockSpec((1, DIFF_QKDIM), vec),
        pl.BlockSpec((1, DIFF_VDIM), vec),
    ]
    return pl.pallas_call(
        functools.partial(_attn_kernel, lam_init=lam_init),
        grid=grid,
        in_specs=in_specs,
        out_specs=pl.BlockSpec((None, S, DIFF_VDIM), rows),
        out_shape=jax.ShapeDtypeStruct((B, S, DIFF_WIDTH), jnp.bfloat16),
        scratch_shapes=[
            pltpu.VMEM((2, 2, DIFF_VDIM + VT_PAD, tq), jnp.float32),
            pltpu.VMEM((2, 1, tq), jnp.float32),
            pltpu.VMEM((ATT_TK, tq), jnp.float32),
            pltpu.VMEM((2, ATT_ITEM, tq), jnp.float32),
            pltpu.VMEM((2, 1, tq), jnp.float32),
            pltpu.VMEM((S, 2 * LANES), jnp.bfloat16),
            pltpu.VMEM((DIFF_VDIM + VT_PAD, S), jnp.bfloat16),
            pltpu.VMEM((2, 2 * LANES, tq), jnp.bfloat16),
        ],
        compiler_params=pltpu.CompilerParams(
            dimension_semantics=("arbitrary", "arbitrary"),
            vmem_limit_bytes=VMEM_LIMIT),
        name="diff_attn",
    )(qt, k, vt, gd, lq1, lk1, lq2, lk2, sg)


def _out_kernel(x_ref, yc_ref, yd_ref, w_ref, g_ref, o_ref):
    y = (jnp.dot(yc_ref[...], w_ref[0:CONV_WIDTH, :], preferred_element_type=jnp.float32)
         + jnp.dot(yd_ref[...], w_ref[CONV_WIDTH:, :], preferred_element_type=jnp.float32))
    ms = jnp.mean(y * y, axis=-1, keepdims=True)
    o_ref[...] = x_ref[...] + y * lax.rsqrt(ms + RMS_EPS) * g_ref[...]


def _out_proj(x, yc, yd, w_out_b, post_g):
    B, S, D = x.shape
    ts = OUT_TILE
    assert S % ts == 0
    row = lambda b, s: (b, s, 0)
    const2 = lambda b, s: (0, 0)
    return pl.pallas_call(
        _out_kernel,
        grid=(B, S // ts),
        in_specs=[
            pl.BlockSpec((None, ts, D), row),
            pl.BlockSpec((None, ts, CONV_WIDTH), row),
            pl.BlockSpec((None, ts, DIFF_WIDTH), row),
            pl.BlockSpec(w_out_b.shape, const2),
            pl.BlockSpec((1, D), const2),
        ],
        out_specs=pl.BlockSpec((None, ts, D), row),
        out_shape=jax.ShapeDtypeStruct((B, S, D), jnp.float32),
        compiler_params=pltpu.CompilerParams(
            dimension_semantics=("arbitrary", "arbitrary"),
            vmem_limit_bytes=VMEM_LIMIT),
        name="out_proj",
    )(x, yc, yd, w_out_b, post_g)


def kernel(x, pre_norm_g, w_in, conv_dw_w, conv_dw_b, conv_ln_g, conv_ln_b, conv_pw_w, conv_pw_b,
           lambda_q1, lambda_k1, lambda_q2, lambda_k2, diff_subln_g, w_out, post_norm_g):
    depth = pre_norm_g.shape[0]
    h = x
    for i in range(depth):
        w_in_b = w_in[i].astype(jnp.bfloat16)
        w_qt_b = w_in[i][:, C_Q:C_Q + DIFF_WIDTH].T.astype(jnp.bfloat16)
        w_vt_b = w_in[i][:, C_V:C_V + DIFF_WIDTH].T.astype(jnp.bfloat16)
        yc, qt, k, vt, gd = _proj_conv(
            h, pre_norm_g[i][None], w_in_b, w_qt_b, w_vt_b, conv_dw_w[i], conv_dw_b[i][None],
            conv_ln_g[i][None], conv_ln_b[i][None], conv_pw_w[i].astype(jnp.bfloat16), conv_pw_b[i][None])
        yd = _attention(qt, k, vt, gd, lambda_q1[i][None], lambda_k1[i][None], lambda_q2[i][None],
                        lambda_k2[i][None], diff_subln_g[i][None], _lambda_init(i))
        h = _out_proj(h, yc, yd, w_out[i].astype(jnp.bfloat16), post_norm_g[i][None])
    return h
```

```python
import functools
import math

import jax
import jax.numpy as jnp
from jax import lax
from jax.experimental import pallas as pl
from jax.experimental.pallas import tpu as pltpu

D_MODEL = 1024
CONV_WIDTH = 512
CONV_KERNEL = 31
DIFF_WIDTH = 512
DIFF_HEADS = 4
DIFF_VDIM = 128
DIFF_QKDIM = 64
RMS_EPS = 1e-6
LN_EPS = 1e-5


def _lambda_init(layer_idx):
    return 0.8 - 0.6 * math.exp(-0.3 * layer_idx)


LOG2E = math.log2(math.e)
Q_PRESCALE = (DIFF_QKDIM ** -0.5) * LOG2E

C_GLU = 0
C_CGATE = 1024
C_Q = 1536
C_K = 2048
C_V = 2560
C_DGATE = 3072

PROJ_TILE = 512
OUT_TILE = 1024
CONV_HALO = 32
CONV_ROWS = 256
LANES = 128
ATT_TQ = 512
ATT_TK = 512
ATT_ITEM = 1024
POS_SHIFT = 8
POS_RADIX = 1 << POS_SHIFT
VT_PAD = 16
VMEM_LIMIT = 48 * 1024 * 1024

_NT = (((1,), (1,)), ((), ()))


def _silu(x):
    return x * jax.nn.sigmoid(x)


def _aligned(start, multiple):
    return start if isinstance(start, int) else pl.multiple_of(start, multiple)


def _proj_conv_kernel(x_ref, g_ref, w_ref, wqt_ref, wvt_ref, dww_ref, dwb_ref, lng_ref, lnb_ref, pww_ref, pwb_ref,
                      yc_ref, qt_ref, k_ref, vt_ref, gd_ref, hbuf, cbuf):
    ts = x_ref.shape[0]
    s_idx = pl.program_id(1)

    x = x_ref[...]
    ms = jnp.mean(x * x, axis=-1, keepdims=True)
    xb = (x * lax.rsqrt(ms + RMS_EPS) * g_ref[...]).astype(jnp.bfloat16)

    def proj(c0, width):
        return jnp.dot(xb, w_ref[:, c0:c0 + width], preferred_element_type=jnp.float32)

    n_slab = CONV_WIDTH // LANES

    @pl.when(s_idx == 0)
    def _():
        hbuf[:, 0:CONV_HALO, :] = jnp.zeros((n_slab, CONV_HALO, LANES), jnp.float32)

    a = proj(C_GLU, CONV_WIDTH)
    b = proj(C_GLU + CONV_WIDTH, CONV_WIDTH)
    h = a * jax.nn.sigmoid(b)
    for l in range(n_slab):
        hbuf[l, CONV_HALO:CONV_HALO + ts, :] = h[:, l * LANES:(l + 1) * LANES]

    off = CONV_HALO - (CONV_KERNEL - 1)
    half = CONV_ROWS // 2

    def conv_block(l, r0):
        lanes = slice(l * LANES, (l + 1) * LANES)
        acc = [jnp.broadcast_to(dwb_ref[:, lanes], (half, LANES)) for _ in range(2)]
        for tau in range(CONV_KERNEL):
            w = dww_ref[tau:tau + 1, lanes]
            for par in range(2):
                acc[par] = acc[par] + w * hbuf[l, pl.ds(r0 + off + tau + par, half, stride=2), :]
        for par in range(2):
            cbuf[l, pl.ds(r0 + par, half, stride=2), :] = acc[par]

    qt_ref[...] = (lax.dot_general(wqt_ref[...], xb, _NT, preferred_element_type=jnp.float32)
                   * Q_PRESCALE).astype(jnp.bfloat16)
    k_ref[...] = proj(C_K, DIFF_WIDTH).astype(jnp.bfloat16)
    vt_ref[...] = lax.dot_general(wvt_ref[...], xb, _NT,
                                  preferred_element_type=jnp.float32).astype(jnp.bfloat16)
    gd_ref[...] = _silu(proj(C_DGATE, DIFF_WIDTH))
    gate_c = _silu(proj(C_CGATE, CONV_WIDTH))

    for r0 in range(0, ts, CONV_ROWS):
        for l in range(n_slab):
            conv_block(l, r0)

    hbuf[:, 0:CONV_HALO, :] = hbuf[:, ts:ts + CONV_HALO, :]

    cv = jnp.concatenate([cbuf[l] for l in range(n_slab)], axis=-1)
    mu = jnp.mean(cv, axis=-1, keepdims=True)
    d = cv - mu
    var = jnp.mean(d * d, axis=-1, keepdims=True)
    hn = d * lax.rsqrt(var + LN_EPS) * lng_ref[...] + lnb_ref[...]
    hs = _silu(hn).astype(jnp.bfloat16)
    yc = jnp.dot(hs, pww_ref[...], preferred_element_type=jnp.float32) + pwb_ref[...]
    yc_ref[...] = (yc * gate_c).astype(jnp.bfloat16)


def _proj_conv(x, pre_g, w_in_b, w_qt_b, w_vt_b, dw_w, dw_b, ln_g, ln_b, pw_w_b, pw_b):
    B, S, D = x.shape
    ts = PROJ_TILE
    assert S % ts == 0 and ts % CONV_ROWS == 0
    grid = (B, S // ts)
    row = lambda b, s: (b, s, 0)
    const2 = lambda b, s: (0, 0)
    in_specs = [
        pl.BlockSpec((None, ts, D), row),
        pl.BlockSpec((1, D), const2),
        pl.BlockSpec(w_in_b.shape, const2),
        pl.BlockSpec(w_qt_b.shape, const2),
        pl.BlockSpec(w_vt_b.shape, const2),
        pl.BlockSpec(dw_w.shape, const2),
        pl.BlockSpec((1, CONV_WIDTH), const2),
        pl.BlockSpec((1, CONV_WIDTH), const2),
        pl.BlockSpec((1, CONV_WIDTH), const2),
        pl.BlockSpec(pw_w_b.shape, const2),
        pl.BlockSpec((1, CONV_WIDTH), const2),
    ]
    out_specs = [
        pl.BlockSpec((None, ts, CONV_WIDTH), row),
        pl.BlockSpec((None, DIFF_WIDTH, ts), lambda b, s: (b, 0, s)),
        pl.BlockSpec((None, ts, DIFF_WIDTH), row),
        pl.BlockSpec((None, DIFF_WIDTH, ts), lambda b, s: (b, 0, s)),
        pl.BlockSpec((None, ts, DIFF_WIDTH), row),
    ]
    out_shape = [
        jax.ShapeDtypeStruct((B, S, CONV_WIDTH), jnp.bfloat16),
        jax.ShapeDtypeStruct((B, DIFF_WIDTH, S), jnp.bfloat16),
        jax.ShapeDtypeStruct((B, S, DIFF_WIDTH), jnp.bfloat16),
        jax.ShapeDtypeStruct((B, DIFF_WIDTH, S), jnp.bfloat16),
        jax.ShapeDtypeStruct((B, S, DIFF_WIDTH), jnp.float32),
    ]
    return pl.pallas_call(
        _proj_conv_kernel,
        grid=grid,
        in_specs=in_specs,
        out_specs=out_specs,
        out_shape=out_shape,
        scratch_shapes=[
            pltpu.VMEM((CONV_WIDTH // LANES, CONV_HALO + ts, LANES), jnp.float32),
            pltpu.VMEM((CONV_WIDTH // LANES, ts, LANES), jnp.float32),
        ],
        compiler_params=pltpu.CompilerParams(
            dimension_semantics=("arbitrary", "arbitrary"),
            vmem_limit_bytes=VMEM_LIMIT),
        name="proj_conv",
    )(x, pre_g, w_in_b, w_qt_b, w_vt_b, dw_w, dw_b, ln_g, ln_b, pw_w_b, pw_b)


def _attn_kernel(qt_ref, k_ref, vt_ref, gd_ref, lq1_ref, lk1_ref, lq2_ref, lk2_ref, sg_ref,
                 o_ref, acc_ref, m_ref, mask_ref, t_ref, tmax_ref, kaug_ref, vta_ref, qaug_ref,
                 *, lam_init):
    tq = ATT_TQ
    tk = ATT_TK
    seq = k_ref.shape[0]
    n_q = seq // tq
    h = pl.program_id(1)
    f32, bf16 = jnp.float32, jnp.bfloat16

    hv = jnp.full((1, 1), h, jnp.int32).astype(f32)
    slope = jnp.exp2(-(8.0 / DIFF_HEADS) * (hv + 1.0)) * LOG2E

    kaug_ref[:, 0:DIFF_VDIM] = k_ref[...]
    jj = lax.broadcasted_iota(jnp.int32, (ATT_ITEM, LANES), 0)
    ln = lax.broadcasted_iota(jnp.int32, (ATT_ITEM, LANES), 1)
    feat = jnp.where(ln < 3, jj & (POS_RADIX - 1), jnp.where(ln < 6, jj >> POS_SHIFT, 0))
    feat = feat.astype(f32).astype(bf16)
    for blk in range(seq // ATT_ITEM):
        kaug_ref[blk * ATT_ITEM:(blk + 1) * ATT_ITEM, DIFF_VDIM:] = feat
    vta_ref[0:DIFF_VDIM, :] = vt_ref[...]
    row = lax.broadcasted_iota(jnp.int32, (VT_PAD, seq), 0)
    vta_ref[DIFF_VDIM:, :] = jnp.where(row == 0, 1.0, 0.0).astype(bf16)
    kk = lax.broadcasted_iota(jnp.int32, (tk, tq), 0)
    ii = lax.broadcasted_iota(jnp.int32, (tk, tq), 1)
    mask_ref[...] = jnp.where(kk > ii, -jnp.inf, 0.0).astype(f32)

    sl = jnp.broadcast_to(slope, (1, tq))
    s_hi = sl.astype(bf16).astype(f32)
    s_mid = (sl - s_hi).astype(bf16).astype(f32)
    s_lo = (sl - s_hi - s_mid).astype(bf16).astype(f32)
    r = lax.broadcasted_iota(jnp.int32, (VT_PAD, tq), 0)
    piece = jnp.where((r == 0) | (r == 3), s_hi, jnp.where((r == 1) | (r == 4), s_mid, s_lo))
    srows = jnp.where(r < 3, piece, jnp.where(r < 6, piece * POS_RADIX, 0.0)).astype(bf16)
    saug = jnp.concatenate([srows, jnp.zeros((LANES - VT_PAD, tq), bf16)], axis=0)
    zhalf = jnp.zeros((DIFF_QKDIM, tq), bf16)

    def query_operands(qt):
        return (jnp.concatenate([qt[0:DIFF_QKDIM], zhalf, saug], axis=0),
                jnp.concatenate([zhalf, qt[DIFF_QKDIM:], saug], axis=0))

    own = query_operands(qt_ref[:, 0:tq])
    qaug_ref[0] = own[0]
    qaug_ref[1] = own[1]
    qpos = lax.broadcasted_iota(jnp.int32, (1, tq), 1).astype(f32)
    lam = (jnp.exp(jnp.sum(lq1_ref[...] * lk1_ref[...], axis=-1, keepdims=True))
           - jnp.exp(jnp.sum(lq2_ref[...] * lk2_ref[...], axis=-1, keepdims=True)) + lam_init)

    def tile(qi, carry):
        _tile(qi)
        return carry

    def _tile(qi):
        qaug = (qaug_ref[0], qaug_ref[1])
        q_next0 = _aligned(jnp.minimum(qi + 1, n_q - 1) * tq, tq)
        qaug_next = query_operands(qt_ref[:, pl.ds(q_next0, tq)])

        m_ref[...] = jnp.full(m_ref.shape, -jnp.inf, f32)

        def scores(c, u0, n_units, mask_last, next_tile=False):
            qop = qaug_next[c] if next_tile else qaug[c]
            mx = None
            for u in range(n_units):
                k0 = _aligned((u0 + u) * tk, tk)
                t = jnp.dot(kaug_ref[pl.ds(k0, tk), :], qop, preferred_element_type=f32)
                if mask_last and u == n_units - 1:
                    t = t + mask_ref[...]
                t_ref[c, u * tk:(u + 1) * tk, :] = t
                tm = jnp.max(t, axis=0, keepdims=True)
                mx = tm if mx is None else jnp.maximum(mx, tm)
            tmax_ref[c] = mx

        def softmax_pv(par, c, u0, n_units):
            k0 = _aligned(u0 * tk, tk)
            nk = n_units * tk
            cq = slope * (jnp.asarray(u0 * tk - qi * tq, jnp.int32).astype(f32) - qpos)
            m_old = m_ref[c]
            m_new = jnp.maximum(m_old, tmax_ref[c] + cq)
            p = jnp.exp2(t_ref[c, 0:nk, :] - (m_new - cq))
            alpha = jnp.exp2(m_old - m_new)
            acc_ref[par, c] = alpha * acc_ref[par, c] + jnp.dot(
                vta_ref[:, pl.ds(k0, nk)], p.astype(bf16), preferred_element_type=f32)
            m_ref[c] = m_new

        units = ATT_ITEM // tk
        nb = qi // units
        tail0 = nb * units

        def run(n_tail):
            par = n_tail - 1

            @pl.when((qi % units == par) & (nb == 0))
            def _():
                first_tile = n_tail == 1
                acc_ref[par] = jnp.zeros(acc_ref.shape[1:], f32)
                if first_tile:
                    scores(0, 0, n_tail, True)
                scores(1, 0, n_tail, True)
                if not first_tile:
                    finish(qi - 1, 1 - par)
                softmax_pv(par, 0, 0, n_tail)
                scores(0, 0, units, first_tile, next_tile=True)
                softmax_pv(par, 1, 0, n_tail)

            @pl.when((qi % units == par) & (nb > 0))
            def _():
                acc_ref[par] = jnp.zeros(acc_ref.shape[1:], f32)

                def body(j, carry):
                    scores(1, j * units, units, False)
                    softmax_pv(par, 0, j * units, units)
                    scores(0, (j + 1) * units, units, False)
                    softmax_pv(par, 1, j * units, units)
                    return carry

                lax.fori_loop(0, nb - 1, body, 0)
                last = tail0 - units
                scores(1, last, units, False)
                finish(qi - 1, 1 - par)
                softmax_pv(par, 0, last, units)
                scores(0, tail0, n_tail, True)
                softmax_pv(par, 1, last, units)
                scores(1, tail0, n_tail, True)
                softmax_pv(par, 0, tail0, n_tail)
                scores(0, 0, units, False, next_tile=True)
                softmax_pv(par, 1, tail0, n_tail)

        for n_tail in range(1, units + 1):
            run(n_tail)

        qaug_ref[0] = qaug_next[0]
        qaug_ref[1] = qaug_next[1]

    def finish(q_tile, par):
        a0 = acc_ref[par, 0]
        a1 = acc_ref[par, 1]
        o = (a0[0:DIFF_VDIM] / a0[DIFF_VDIM:DIFF_VDIM + 1]
             - lam * (a1[0:DIFF_VDIM] / a1[DIFF_VDIM:DIFF_VDIM + 1]))
        ms = jnp.mean(o * o, axis=0, keepdims=True)
        on = (o * lax.rsqrt(ms + RMS_EPS)).T
        q0 = _aligned(q_tile * tq, tq)
        y = on * (sg_ref[...] * (1.0 - lam_init)) * gd_ref[pl.ds(q0, tq), :]
        o_ref[pl.ds(q0, tq), :] = y.astype(bf16)

    lax.fori_loop(0, n_q, tile, 0)
    finish(n_q - 1, (n_q - 1) % 2)


def _attention(qt, k, vt, gd, lq1, lk1, lq2, lk2, sg, lam_init):
    B, S, _ = k.shape
    assert ATT_TQ == ATT_TK and ATT_ITEM == 2 * ATT_TK and S % ATT_ITEM == 0
    tq = ATT_TQ
    grid = (B, DIFF_HEADS)
    vec = lambda b, h: (0, 0)
    rows = lambda b, h: (b, 0, h)
    cols = lambda b, h: (b, h, 0)
    in_specs = [
        pl.BlockSpec((None, DIFF_VDIM, S), cols),
        pl.BlockSpec((None, S, DIFF_VDIM), rows),
        pl.BlockSpec((None, DIFF_VDIM, S), cols),
        pl.BlockSpec((None, S, DIFF_VDIM), rows),
        pl.BlockSpec((1, DIFF_QKDIM), vec),
        pl.BlockSpec((1, DIFF_QKDIM), vec),
        pl.BlockSpec((1, DIFF_QKDIM), vec),
        pl.BlockSpec((1, DIFF_QKDIM), vec),
        pl.BlockSpec((1, DIFF_VDIM), vec),
    ]
    return pl.pallas_call(
        functools.partial(_attn_kernel, lam_init=lam_init),
        grid=grid,
        in_specs=in_specs,
        out_specs=pl.BlockSpec((None, S, DIFF_VDIM), rows),
        out_shape=jax.ShapeDtypeStruct((B, S, DIFF_WIDTH), jnp.bfloat16),
        scratch_shapes=[
            pltpu.VMEM((2, 2, DIFF_VDIM + VT_PAD, tq), jnp.float32),
            pltpu.VMEM((2, 1, tq), jnp.float32),
            pltpu.VMEM((ATT_TK, tq), jnp.float32),
            pltpu.VMEM((2, ATT_ITEM, tq), jnp.float32),
            pltpu.VMEM((2, 1, tq), jnp.float32),
            pltpu.VMEM((S, 2 * LANES), jnp.bfloat16),
            pltpu.VMEM((DIFF_VDIM + VT_PAD, S), jnp.bfloat16),
            pltpu.VMEM((2, 2 * LANES, tq), jnp.bfloat16),
        ],
        compiler_params=pltpu.CompilerParams(
            dimension_semantics=("arbitrary", "arbitrary"),
            vmem_limit_bytes=VMEM_LIMIT),
        name="diff_attn",
    )(qt, k, vt, gd, lq1, lk1, lq2, lk2, sg)


def _out_kernel(x_ref, yc_ref, yd_ref, w_ref, g_ref, o_ref):
    y = (jnp.dot(yc_ref[...], w_ref[0:CONV_WIDTH, :], preferred_element_type=jnp.float32)
         + jnp.dot(yd_ref[...], w_ref[CONV_WIDTH:, :], preferred_element_type=jnp.float32))
    ms = jnp.mean(y * y, axis=-1, keepdims=True)
    o_ref[...] = x_ref[...] + y * lax.rsqrt(ms + RMS_EPS) * g_ref[...]


def _out_proj(x, yc, yd, w_out_b, post_g):
    B, S, D = x.shape
    ts = OUT_TILE
    assert S % ts == 0
    row = lambda b, s: (b, s, 0)
    const2 = lambda b, s: (0, 0)
    return pl.pallas_call(
        _out_kernel,
        grid=(B, S // ts),
        in_specs=[
            pl.BlockSpec((None, ts, D), row),
            pl.BlockSpec((None, ts, CONV_WIDTH), row),
            pl.BlockSpec((None, ts, DIFF_WIDTH), row),
            pl.BlockSpec(w_out_b.shape, const2),
            pl.BlockSpec((1, D), const2),
        ],
        out_specs=pl.BlockSpec((None, ts, D), row),
        out_shape=jax.ShapeDtypeStruct((B, S, D), jnp.float32),
        compiler_params=pltpu.CompilerParams(
            dimension_semantics=("arbitrary", "arbitrary"),
            vmem_limit_bytes=VMEM_LIMIT),
        name="out_proj",
    )(x, yc, yd, w_out_b, post_g)


def kernel(x, pre_norm_g, w_in, conv_dw_w, conv_dw_b, conv_ln_g, conv_ln_b, conv_pw_w, conv_pw_b,
           lambda_q1, lambda_k1, lambda_q2, lambda_k2, diff_subln_g, w_out, post_norm_g):
    depth = pre_norm_g.shape[0]
    h = x
    for i in range(depth):
        w_in_b = w_in[i].astype(jnp.bfloat16)
        w_qt_b = w_in_b[:, C_Q:C_Q + DIFF_WIDTH].T
        w_vt_b = w_in_b[:, C_V:C_V + DIFF_WIDTH].T
        yc, qt, k, vt, gd = _proj_conv(
            h, pre_norm_g[i][None], w_in_b, w_qt_b, w_vt_b, conv_dw_w[i], conv_dw_b[i][None],
            conv_ln_g[i][None], conv_ln_b[i][None], conv_pw_w[i].astype(jnp.bfloat16), conv_pw_b[i][None])
        yd = _attention(qt, k, vt, gd, lambda_q1[i][None], lambda_k1[i][None], lambda_q2[i][None],
                        lambda_k2[i][None], diff_subln_g[i][None], _lambda_init(i))
        h = _out_proj(h, yc, yd, w_out[i].astype(jnp.bfloat16), post_norm_g[i][None])
    return h
```

```python
import functools
import math

import jax
import jax.numpy as jnp
from jax import lax
from jax.experimental import pallas as pl
from jax.experimental.pallas import tpu as pltpu

D_MODEL = 1024
CONV_WIDTH = 512
CONV_KERNEL = 31
DIFF_WIDTH = 512
DIFF_HEADS = 4
DIFF_VDIM = 128
DIFF_QKDIM = 64
RMS_EPS = 1e-6
LN_EPS = 1e-5


def _lambda_init(layer_idx):
    return 0.8 - 0.6 * math.exp(-0.3 * layer_idx)


LOG2E = math.log2(math.e)
Q_PRESCALE = (DIFF_QKDIM ** -0.5) * LOG2E

C_GLU = 0
C_CGATE = 1024
C_Q = 1536
C_K = 2048
C_V = 2560
C_DGATE = 3072

PROJ_TILE = 512
PROJ_CHUNK = 256
OUT_TILE = 1024
CONV_HALO = 32
CONV_ROWS = 256
LANES = 128
ATT_TQ = 512
ATT_TK = 512
ATT_ITEM = 1024
POS_SHIFT = 8
POS_RADIX = 1 << POS_SHIFT
VT_PAD = 16
VMEM_LIMIT = 48 * 1024 * 1024

_NT = (((1,), (1,)), ((), ()))


def _silu(x):
    return x * jax.nn.sigmoid(x)


def _aligned(start, multiple):
    return start if isinstance(start, int) else pl.multiple_of(start, multiple)


def _proj_conv_kernel(x_ref, g_ref, w_ref, dww_ref, dwb_ref, lng_ref, lnb_ref, pww_ref, pwb_ref,
                      yc_ref, qt_ref, k_ref, vt_ref, gd_ref, hbuf, cbuf, xb_ref, z_ref):
    ts = x_ref.shape[0]
    s_idx = pl.program_id(1)
    f32, bf16 = jnp.float32, jnp.bfloat16

    x = x_ref[...]
    ms = jnp.mean(x * x, axis=-1, keepdims=True)
    xb_ref[...] = (x * lax.rsqrt(ms + RMS_EPS) * g_ref[...]).astype(bf16)

    n_slab = CONV_WIDTH // LANES

    @pl.when(s_idx == 0)
    def _():
        hbuf[:, 0:CONV_HALO, :] = jnp.zeros((n_slab, CONV_HALO, LANES), f32)

    xb = xb_ref[...]
    a = jnp.dot(xb, w_ref[:, C_GLU:C_GLU + CONV_WIDTH], preferred_element_type=f32)
    b = jnp.dot(xb, w_ref[:, C_GLU + CONV_WIDTH:C_GLU + 2 * CONV_WIDTH], preferred_element_type=f32)
    h = a * jax.nn.sigmoid(b)
    for l in range(n_slab):
        hbuf[l, CONV_HALO:CONV_HALO + ts, :] = h[:, l * LANES:(l + 1) * LANES]

    def proj_chunk(i):
        c0 = _aligned(C_CGATE + i * PROJ_CHUNK, PROJ_CHUNK)
        z_ref[i] = jnp.dot(xb_ref[...], w_ref[:, pl.ds(c0, PROJ_CHUNK)], preferred_element_type=f32)

    off = CONV_HALO - (CONV_KERNEL - 1)
    half = CONV_ROWS // 2
    slab_bits = n_slab.bit_length() - 1

    def conv_block(blk):
        l = blk & (n_slab - 1)
        r0 = (blk >> slab_bits) * CONV_ROWS
        acc = [jnp.broadcast_to(dwb_ref[l], (half, LANES)) for _ in range(2)]
        for tau in range(CONV_KERNEL):
            w = dww_ref[l, tau:tau + 1, :]
            for par in range(2):
                acc[par] = acc[par] + w * hbuf[l, pl.ds(r0 + off + tau + par, half, stride=2), :]
        for par in range(2):
            cbuf[l, pl.ds(r0 + par, half, stride=2), :] = acc[par]

    n_blocks = (ts // CONV_ROWS) * n_slab
    n_chunks = (w_ref.shape[1] - C_CGATE) // PROJ_CHUNK

    for i in range(n_blocks):
        @pl.when(s_idx >= 0)
        def _(i=i):
            proj_chunk(i)
            conv_block(i)

    for i in range(n_blocks, n_chunks):
        proj_chunk(i)

    hbuf[:, 0:CONV_HALO, :] = hbuf[:, ts:ts + CONV_HALO, :]

    def cols(c0):
        i = (c0 - C_CGATE) // PROJ_CHUNK
        return jnp.concatenate([z_ref[i + j] for j in range(DIFF_WIDTH // PROJ_CHUNK)], axis=1)

    qt_ref[...] = (cols(C_Q).T * Q_PRESCALE).astype(bf16)
    k_ref[...] = cols(C_K).astype(bf16)
    vt_ref[...] = cols(C_V).T.astype(bf16)
    gd_ref[...] = _silu(cols(C_DGATE))

    cv = jnp.concatenate([cbuf[l] for l in range(n_slab)], axis=-1)
    mu = jnp.mean(cv, axis=-1, keepdims=True)
    d = cv - mu
    var = jnp.mean(d * d, axis=-1, keepdims=True)
    hn = d * lax.rsqrt(var + LN_EPS) * lng_ref[...] + lnb_ref[...]
    hs = _silu(hn).astype(bf16)
    yc = jnp.dot(hs, pww_ref[...], preferred_element_type=f32) + pwb_ref[...]
    yc_ref[...] = (yc * _silu(cols(C_CGATE))).astype(bf16)


def _proj_conv(x, pre_g, w_in_b, dw_w, dw_b, ln_g, ln_b, pw_w_b, pw_b):
    B, S, D = x.shape
    ts = PROJ_TILE
    n_slab = CONV_WIDTH // LANES
    n_chunks = (w_in_b.shape[1] - C_CGATE) // PROJ_CHUNK
    assert S % ts == 0 and ts % CONV_ROWS == 0 and n_slab & (n_slab - 1) == 0
    assert (ts // CONV_ROWS) * n_slab <= n_chunks and (w_in_b.shape[1] - C_CGATE) % PROJ_CHUNK == 0
    dw_w = dw_w.reshape(CONV_KERNEL, n_slab, LANES).transpose(1, 0, 2)
    dw_b = dw_b.reshape(n_slab, 1, LANES)
    grid = (B, S // ts)
    row = lambda b, s: (b, s, 0)
    const2 = lambda b, s: (0, 0)
    in_specs = [
        pl.BlockSpec((None, ts, D), row),
        pl.BlockSpec((1, D), const2),
        pl.BlockSpec(w_in_b.shape, const2),
        pl.BlockSpec(dw_w.shape, lambda b, s: (0, 0, 0)),
        pl.BlockSpec(dw_b.shape, lambda b, s: (0, 0, 0)),
        pl.BlockSpec((1, CONV_WIDTH), const2),
        pl.BlockSpec((1, CONV_WIDTH), const2),
        pl.BlockSpec(pw_w_b.shape, const2),
        pl.BlockSpec((1, CONV_WIDTH), const2),
    ]
    out_specs = [
        pl.BlockSpec((None, ts, CONV_WIDTH), row),
        pl.BlockSpec((None, DIFF_WIDTH, ts), lambda b, s: (b, 0, s)),
        pl.BlockSpec((None, ts, DIFF_WIDTH), row),
        pl.BlockSpec((None, DIFF_WIDTH, ts), lambda b, s: (b, 0, s)),
        pl.BlockSpec((None, ts, DIFF_WIDTH), row),
    ]
    out_shape = [
        jax.ShapeDtypeStruct((B, S, CONV_WIDTH), jnp.bfloat16),
        jax.ShapeDtypeStruct((B, DIFF_WIDTH, S), jnp.bfloat16),
        jax.ShapeDtypeStruct((B, S, DIFF_WIDTH), jnp.bfloat16),
        jax.ShapeDtypeStruct((B, DIFF_WIDTH, S), jnp.bfloat16),
        jax.ShapeDtypeStruct((B, S, DIFF_WIDTH), jnp.float32),
    ]
    return pl.pallas_call(
        _proj_conv_kernel,
        grid=grid,
        in_specs=in_specs,
        out_specs=out_specs,
        out_shape=out_shape,
        scratch_shapes=[
            pltpu.VMEM((CONV_WIDTH // LANES, CONV_HALO + ts, LANES), jnp.float32),
            pltpu.VMEM((CONV_WIDTH // LANES, ts, LANES), jnp.float32),
            pltpu.VMEM((ts, D), jnp.bfloat16),
            pltpu.VMEM((n_chunks, ts, PROJ_CHUNK), jnp.float32),
        ],
        compiler_params=pltpu.CompilerParams(
            dimension_semantics=("arbitrary", "arbitrary"),
            vmem_limit_bytes=VMEM_LIMIT),
        name="proj_conv",
    )(x, pre_g, w_in_b, dw_w, dw_b, ln_g, ln_b, pw_w_b, pw_b)


def _attn_kernel(qt_ref, k_ref, vt_ref, gd_ref, lq1_ref, lk1_ref, lq2_ref, lk2_ref, sg_ref,
                 o_ref, acc_ref, m_ref, mask_ref, t_ref, tmax_ref, kaug_ref, vta_ref, qaug_ref,
                 *, lam_init):
    tq = ATT_TQ
    tk = ATT_TK
    seq = k_ref.shape[0]
    n_q = seq // tq
    h = pl.program_id(1)
    f32, bf16 = jnp.float32, jnp.bfloat16

    hv = jnp.full((1, 1), h, jnp.int32).astype(f32)
    slope = jnp.exp2(-(8.0 / DIFF_HEADS) * (hv + 1.0)) * LOG2E

    kaug_ref[:, 0:DIFF_VDIM] = k_ref[...]
    jj = lax.broadcasted_iota(jnp.int32, (ATT_ITEM, LANES), 0)
    ln = lax.broadcasted_iota(jnp.int32, (ATT_ITEM, LANES), 1)
    feat = jnp.where(ln < 3, jj & (POS_RADIX - 1), jnp.where(ln < 6, jj >> POS_SHIFT, 0))
    feat = feat.astype(f32).astype(bf16)
    for blk in range(seq // ATT_ITEM):
        kaug_ref[blk * ATT_ITEM:(blk + 1) * ATT_ITEM, DIFF_VDIM:] = feat
    vta_ref[0:DIFF_VDIM, :] = vt_ref[...]
    row = lax.broadcasted_iota(jnp.int32, (VT_PAD, seq), 0)
    vta_ref[DIFF_VDIM:, :] = jnp.where(row == 0, 1.0, 0.0).astype(bf16)
    kk = lax.broadcasted_iota(jnp.int32, (tk, tq), 0)
    ii = lax.broadcasted_iota(jnp.int32, (tk, tq), 1)
    mask_ref[...] = jnp.where(kk > ii, -jnp.inf, 0.0).astype(f32)

    sl = jnp.broadcast_to(slope, (1, tq))
    s_hi = sl.astype(bf16).astype(f32)
    s_mid = (sl - s_hi).astype(bf16).astype(f32)
    s_lo = (sl - s_hi - s_mid).astype(bf16).astype(f32)
    r = lax.broadcasted_iota(jnp.int32, (VT_PAD, tq), 0)
    piece = jnp.where((r == 0) | (r == 3), s_hi, jnp.where((r == 1) | (r == 4), s_mid, s_lo))
    srows = jnp.where(r < 3, piece, jnp.where(r < 6, piece * POS_RADIX, 0.0)).astype(bf16)
    saug = jnp.concatenate([srows, jnp.zeros((LANES - VT_PAD, tq), bf16)], axis=0)
    zhalf = jnp.zeros((DIFF_QKDIM, tq), bf16)

    def query_operands(qt):
        return (jnp.concatenate([qt[0:DIFF_QKDIM], zhalf, saug], axis=0),
                jnp.concatenate([zhalf, qt[DIFF_QKDIM:], saug], axis=0))

    own = query_operands(qt_ref[:, 0:tq])
    qaug_ref[0] = own[0]
    qaug_ref[1] = own[1]
    qpos = lax.broadcasted_iota(jnp.int32, (1, tq), 1).astype(f32)
    lam = (jnp.exp(jnp.sum(lq1_ref[...] * lk1_ref[...], axis=-1, keepdims=True))
           - jnp.exp(jnp.sum(lq2_ref[...] * lk2_ref[...], axis=-1, keepdims=True)) + lam_init)

    def tile(qi, carry):
        _tile(qi)
        return carry

    def _tile(qi):
        qaug = (qaug_ref[0], qaug_ref[1])
        q_next0 = _aligned(jnp.minimum(qi + 1, n_q - 1) * tq, tq)
        qaug_next = query_operands(qt_ref[:, pl.ds(q_next0, tq)])

        m_ref[...] = jnp.full(m_ref.shape, -jnp.inf, f32)

        def scores(c, u0, n_units, mask_last, next_tile=False):
            qop = qaug_next[c] if next_tile else qaug[c]
            mx = None
            for u in range(n_units):
                k0 = _aligned((u0 + u) * tk, tk)
                t = jnp.dot(kaug_ref[pl.ds(k0, tk), :], qop, preferred_element_type=f32)
                if mask_last and u == n_units - 1:
                    t = t + mask_ref[...]
                t_ref[c, u * tk:(u + 1) * tk, :] = t
                tm = jnp.max(t, axis=0, keepdims=True)
                mx = tm if mx is None else jnp.maximum(mx, tm)
            tmax_ref[c] = mx

        def softmax_pv(par, c, u0, n_units):
            k0 = _aligned(u0 * tk, tk)
            nk = n_units * tk
            cq = slope * (jnp.asarray(u0 * tk - qi * tq, jnp.int32).astype(f32) - qpos)
            m_old = m_ref[c]
            m_new = jnp.maximum(m_old, tmax_ref[c] + cq)
            p = jnp.exp2(t_ref[c, 0:nk, :] - (m_new - cq))
            alpha = jnp.exp2(m_old - m_new)
            acc_ref[par, c] = alpha * acc_ref[par, c] + jnp.dot(
                vta_ref[:, pl.ds(k0, nk)], p.astype(bf16), preferred_element_type=f32)
            m_ref[c] = m_new

        units = ATT_ITEM // tk
        nb = qi // units
        tail0 = nb * units

        def run(n_tail):
            par = n_tail - 1

            @pl.when((qi % units == par) & (nb == 0))
            def _():
                first_tile = n_tail == 1
                acc_ref[par] = jnp.zeros(acc_ref.shape[1:], f32)
                if first_tile:
                    scores(0, 0, n_tail, True)
                scores(1, 0, n_tail, True)
                if not first_tile:
                    finish(qi - 1, 1 - par)
                softmax_pv(par, 0, 0, n_tail)
                scores(0, 0, units, first_tile, next_tile=True)
                softmax_pv(par, 1, 0, n_tail)

            @pl.when((qi % units == par) & (nb > 0))
            def _():
                acc_ref[par] = jnp.zeros(acc_ref.shape[1:], f32)

                def body(j, carry):
                    scores(1, j * units, units, False)
                    softmax_pv(par, 0, j * units, units)
                    scores(0, (j + 1) * units, units, False)
                    softmax_pv(par, 1, j * units, units)
                    return carry

                lax.fori_loop(0, nb - 1, body, 0)
                last = tail0 - units
                scores(1, last, units, False)
                finish(qi - 1, 1 - par)
                softmax_pv(par, 0, last, units)
                scores(0, tail0, n_tail, True)
                softmax_pv(par, 1, last, units)
                scores(1, tail0, n_tail, True)
                softmax_pv(par, 0, tail0, n_tail)
                scores(0, 0, units, False, next_tile=True)
                softmax_pv(par, 1, tail0, n_tail)

        for n_tail in range(1, units + 1):
            run(n_tail)

        qaug_ref[0] = qaug_next[0]
        qaug_ref[1] = qaug_next[1]

    def finish(q_tile, par):
        a0 = acc_ref[par, 0]
        a1 = acc_ref[par, 1]
        o = (a0[0:DIFF_VDIM] / a0[DIFF_VDIM:DIFF_VDIM + 1]
             - lam * (a1[0:DIFF_VDIM] / a1[DIFF_VDIM:DIFF_VDIM + 1]))
        ms = jnp.mean(o * o, axis=0, keepdims=True)
        on = (o * lax.rsqrt(ms + RMS_EPS)).T
        q0 = _aligned(q_tile * tq, tq)
        y = on * (sg_ref[...] * (1.0 - lam_init)) * gd_ref[pl.ds(q0, tq), :]
        o_ref[pl.ds(q0, tq), :] = y.astype(bf16)

    lax.fori_loop(0, n_q, tile, 0)
    finish(n_q - 1, (n_q - 1) % 2)


def _attention(qt, k, vt, gd, lq1, lk1, lq2, lk2, sg, lam_init):
    B, S, _ = k.shape
    assert ATT_TQ == ATT_TK and ATT_ITEM == 2 * ATT_TK and S % ATT_ITEM == 0
    tq = ATT_TQ
    grid = (B, DIFF_HEADS)
    vec = lambda b, h: (0, 0)
    rows = lambda b, h: (b, 0, h)
    cols = lambda b, h: (b, h, 0)
    in_specs = [
        pl.BlockSpec((None, DIFF_VDIM, S), cols),
        pl.BlockSpec((None, S, DIFF_VDIM), rows),
        pl.BlockSpec((None, DIFF_VDIM, S), cols),
        pl.BlockSpec((None, S, DIFF_VDIM), rows),
        pl.BlockSpec((1, DIFF_QKDIM), vec),
        pl.BlockSpec((1, DIFF_QKDIM), vec),
        pl.BlockSpec((1, DIFF_QKDIM), vec),
        pl.BlockSpec((1, DIFF_QKDIM), vec),
        pl.BlockSpec((1, DIFF_VDIM), vec),
    ]
    return pl.pallas_call(
        functools.partial(_attn_kernel, lam_init=lam_init),
        grid=grid,
        in_specs=in_specs,
        out_specs=pl.BlockSpec((None, S, DIFF_VDIM), rows),
        out_shape=jax.ShapeDtypeStruct((B, S, DIFF_WIDTH), jnp.bfloat16),
        scratch_shapes=[
            pltpu.VMEM((2, 2, DIFF_VDIM + VT_PAD, tq), jnp.float32),
            pltpu.VMEM((2, 1, tq), jnp.float32),
            pltpu.VMEM((ATT_TK, tq), jnp.float32),
            pltpu.VMEM((2, ATT_ITEM, tq), jnp.float32),
            pltpu.VMEM((2, 1, tq), jnp.float32),
            pltpu.VMEM((S, 2 * LANES), jnp.bfloat16),
            pltpu.VMEM((DIFF_VDIM + VT_PAD, S), jnp.bfloat16),
            pltpu.VMEM((2, 2 * LANES, tq), jnp.bfloat16),
        ],
        compiler_params=pltpu.CompilerParams(
            dimension_semantics=("arbitrary", "arbitrary"),
            vmem_limit_bytes=VMEM_LIMIT),
        name="diff_attn",
    )(qt, k, vt, gd, lq1, lk1, lq2, lk2, sg)


def _out_kernel(x_ref, yc_ref, yd_ref, w_ref, g_ref, o_ref):
    y = (jnp.dot(yc_ref[...], w_ref[0:CONV_WIDTH, :], preferred_element_type=jnp.float32)
         + jnp.dot(yd_ref[...], w_ref[CONV_WIDTH:, :], preferred_element_type=jnp.float32))
    ms = jnp.mean(y * y, axis=-1, keepdims=True)
    o_ref[...] = x_ref[...] + y * lax.rsqrt(ms + RMS_EPS) * g_ref[...]


def _out_proj(x, yc, yd, w_out_b, post_g):
    B, S, D = x.shape
    ts = OUT_TILE
    assert S % ts == 0
    row = lambda b, s: (b, s, 0)
    const2 = lambda b, s: (0, 0)
    return pl.pallas_call(
        _out_kernel,
        grid=(B, S // ts),
        in_specs=[
            pl.BlockSpec((None, ts, D), row),
            pl.BlockSpec((None, ts, CONV_WIDTH), row),
            pl.BlockSpec((None, ts, DIFF_WIDTH), row),
            pl.BlockSpec(w_out_b.shape, const2),
            pl.BlockSpec((1, D), const2),
        ],
        out_specs=pl.BlockSpec((None, ts, D), row),
        out_shape=jax.ShapeDtypeStruct((B, S, D), jnp.float32),
        compiler_params=pltpu.CompilerParams(
            dimension_semantics=("arbitrary", "arbitrary"),
            vmem_limit_bytes=VMEM_LIMIT),
        name="out_proj",
    )(x, yc, yd, w_out_b, post_g)


def kernel(x, pre_norm_g, w_in, conv_dw_w, conv_dw_b, conv_ln_g, conv_ln_b, conv_pw_w, conv_pw_b,
           lambda_q1, lambda_k1, lambda_q2, lambda_k2, diff_subln_g, w_out, post_norm_g):
    depth = pre_norm_g.shape[0]
    h = x
    for i in range(depth):
        w_in_b = w_in[i].astype(jnp.bfloat16)
        yc, qt, k, vt, gd = _proj_conv(
            h, pre_norm_g[i][None], w_in_b, conv_dw_w[i], conv_dw_b[i][None],
            conv_ln_g[i][None], conv_ln_b[i][None], conv_pw_w[i].astype(jnp.bfloat16), conv_pw_b[i][None])
        yd = _attention(qt, k, vt, gd, lambda_q1[i][None], lambda_k1[i][None], lambda_q2[i][None],
                        lambda_k2[i][None], diff_subln_g[i][None], _lambda_init(i))
        h = _out_proj(h, yc, yd, w_out[i].astype(jnp.bfloat16), post_norm_g[i][None])
    return h
```

```python
import functools
import math

import jax
import jax.numpy as jnp
from jax import lax
from jax.experimental import pallas as pl
from jax.experimental.pallas import tpu as pltpu

D_MODEL = 1024
CONV_WIDTH = 512
CONV_KERNEL = 31
DIFF_WIDTH = 512
DIFF_HEADS = 4
DIFF_VDIM = 128
DIFF_QKDIM = 64
RMS_EPS = 1e-6
LN_EPS = 1e-5


def _lambda_init(layer_idx):
    return 0.8 - 0.6 * math.exp(-0.3 * layer_idx)


LOG2E = math.log2(math.e)
Q_PRESCALE = (DIFF_QKDIM ** -0.5) * LOG2E

C_GLU = 0
C_CGATE = 1024
C_Q = 1536
C_K = 2048
C_V = 2560
C_DGATE = 3072

PROJ_TILE = 512
OUT_TILE = 1024
CONV_HALO = 32
CONV_ROWS = 256
LANES = 128
ATT_TQ = 512
ATT_TK = 512
ATT_ITEM = 1024
POS_SHIFT = 8
POS_RADIX = 1 << POS_SHIFT
VT_PAD = 16
VMEM_LIMIT = 48 * 1024 * 1024

_NT = (((1,), (1,)), ((), ()))


def _silu(x):
    return x * jax.nn.sigmoid(x)


def _aligned(start, multiple):
    return start if isinstance(start, int) else pl.multiple_of(start, multiple)


def _proj_conv_kernel(x_ref, g_ref, w_ref, dww_ref, dwb_ref, lng_ref, lnb_ref, pww_ref, pwb_ref,
                      yc_ref, qt_ref, k_ref, vt_ref, gd_ref, hbuf, cbuf, wqt_ref, wvt_ref):
    ts = x_ref.shape[0]
    s_idx = pl.program_id(1)

    @pl.when((pl.program_id(0) == 0) & (s_idx == 0))
    def _():
        wqt_ref[...] = w_ref[:, C_Q:C_Q + DIFF_WIDTH].T
        wvt_ref[...] = w_ref[:, C_V:C_V + DIFF_WIDTH].T

    x = x_ref[...]
    ms = jnp.mean(x * x, axis=-1, keepdims=True)
    xb = (x * lax.rsqrt(ms + RMS_EPS) * g_ref[...]).astype(jnp.bfloat16)

    def proj(c0, width):
        return jnp.dot(xb, w_ref[:, c0:c0 + width], preferred_element_type=jnp.float32)

    n_slab = CONV_WIDTH // LANES

    @pl.when(s_idx == 0)
    def _():
        hbuf[:, 0:CONV_HALO, :] = jnp.zeros((n_slab, CONV_HALO, LANES), jnp.float32)

    a = proj(C_GLU, CONV_WIDTH)
    b = proj(C_GLU + CONV_WIDTH, CONV_WIDTH)
    h = a * jax.nn.sigmoid(b)
    for l in range(n_slab):
        hbuf[l, CONV_HALO:CONV_HALO + ts, :] = h[:, l * LANES:(l + 1) * LANES]

    off = CONV_HALO - (CONV_KERNEL - 1)
    half = CONV_ROWS // 2

    def conv_block(l, r0):
        lanes = slice(l * LANES, (l + 1) * LANES)
        acc = [jnp.broadcast_to(dwb_ref[:, lanes], (half, LANES)) for _ in range(2)]
        for tau in range(CONV_KERNEL):
            w = dww_ref[tau:tau + 1, lanes]
            for par in range(2):
                acc[par] = acc[par] + w * hbuf[l, pl.ds(r0 + off + tau + par, half, stride=2), :]
        for par in range(2):
            cbuf[l, pl.ds(r0 + par, half, stride=2), :] = acc[par]

    qt_ref[...] = (lax.dot_general(wqt_ref[...], xb, _NT, preferred_element_type=jnp.float32)
                   * Q_PRESCALE).astype(jnp.bfloat16)
    k_ref[...] = proj(C_K, DIFF_WIDTH).astype(jnp.bfloat16)
    vt_ref[...] = lax.dot_general(wvt_ref[...], xb, _NT,
                                  preferred_element_type=jnp.float32).astype(jnp.bfloat16)
    gd_ref[...] = _silu(proj(C_DGATE, DIFF_WIDTH))
    gate_c = _silu(proj(C_CGATE, CONV_WIDTH))

    for r0 in range(0, ts, CONV_ROWS):
        for l in range(n_slab):
            conv_block(l, r0)

    hbuf[:, 0:CONV_HALO, :] = hbuf[:, ts:ts + CONV_HALO, :]

    cv = jnp.concatenate([cbuf[l] for l in range(n_slab)], axis=-1)
    mu = jnp.mean(cv, axis=-1, keepdims=True)
    d = cv - mu
    var = jnp.mean(d * d, axis=-1, keepdims=True)
    hn = d * lax.rsqrt(var + LN_EPS) * lng_ref[...] + lnb_ref[...]
    hs = _silu(hn).astype(jnp.bfloat16)
    yc = jnp.dot(hs, pww_ref[...], preferred_element_type=jnp.float32) + pwb_ref[...]
    yc_ref[...] = (yc * gate_c).astype(jnp.bfloat16)


def _proj_conv(x, pre_g, w_in_b, dw_w, dw_b, ln_g, ln_b, pw_w_b, pw_b):
    B, S, D = x.shape
    ts = PROJ_TILE
    assert S % ts == 0 and ts % CONV_ROWS == 0
    grid = (B, S // ts)
    row = lambda b, s: (b, s, 0)
    const2 = lambda b, s: (0, 0)
    in_specs = [
        pl.BlockSpec((None, ts, D), row),
        pl.BlockSpec((1, D), const2),
        pl.BlockSpec(w_in_b.shape, const2),
        pl.BlockSpec(dw_w.shape, const2),
        pl.BlockSpec((1, CONV_WIDTH), const2),
        pl.BlockSpec((1, CONV_WIDTH), const2),
        pl.BlockSpec((1, CONV_WIDTH), const2),
        pl.BlockSpec(pw_w_b.shape, const2),
        pl.BlockSpec((1, CONV_WIDTH), const2),
    ]
    out_specs = [
        pl.BlockSpec((None, ts, CONV_WIDTH), row),
        pl.BlockSpec((None, DIFF_WIDTH, ts), lambda b, s: (b, 0, s)),
        pl.BlockSpec((None, ts, DIFF_WIDTH), row),
        pl.BlockSpec((None, DIFF_WIDTH, ts), lambda b, s: (b, 0, s)),
        pl.BlockSpec((None, ts, DIFF_WIDTH), row),
    ]
    out_shape = [
        jax.ShapeDtypeStruct((B, S, CONV_WIDTH), jnp.bfloat16),
        jax.ShapeDtypeStruct((B, DIFF_WIDTH, S), jnp.bfloat16),
        jax.ShapeDtypeStruct((B, S, DIFF_WIDTH), jnp.bfloat16),
        jax.ShapeDtypeStruct((B, DIFF_WIDTH, S), jnp.bfloat16),
        jax.ShapeDtypeStruct((B, S, DIFF_WIDTH), jnp.float32),
    ]
    return pl.pallas_call(
        _proj_conv_kernel,
        grid=grid,
        in_specs=in_specs,
        out_specs=out_specs,
        out_shape=out_shape,
        scratch_shapes=[
            pltpu.VMEM((CONV_WIDTH // LANES, CONV_HALO + ts, LANES), jnp.float32),
            pltpu.VMEM((CONV_WIDTH // LANES, ts, LANES), jnp.float32),
            pltpu.VMEM((DIFF_WIDTH, D), jnp.bfloat16),
            pltpu.VMEM((DIFF_WIDTH, D), jnp.bfloat16),
        ],
        compiler_params=pltpu.CompilerParams(
            dimension_semantics=("arbitrary", "arbitrary"),
            vmem_limit_bytes=VMEM_LIMIT),
        name="proj_conv",
    )(x, pre_g, w_in_b, dw_w, dw_b, ln_g, ln_b, pw_w_b, pw_b)


def _attn_kernel(qt_ref, k_ref, vt_ref, gd_ref, lq1_ref, lk1_ref, lq2_ref, lk2_ref, sg_ref,
                 o_ref, acc_ref, m_ref, mask_ref, t_ref, tmax_ref, kaug_ref, vta_ref, qaug_ref,
                 *, lam_init):
    tq = ATT_TQ
    tk = ATT_TK
    seq = k_ref.shape[0]
    n_q = seq // tq
    h = pl.program_id(1)
    f32, bf16 = jnp.float32, jnp.bfloat16

    hv = jnp.full((1, 1), h, jnp.int32).astype(f32)
    slope = jnp.exp2(-(8.0 / DIFF_HEADS) * (hv + 1.0)) * LOG2E

    kaug_ref[:, 0:DIFF_VDIM] = k_ref[...]
    jj = lax.broadcasted_iota(jnp.int32, (ATT_ITEM, LANES), 0)
    ln = lax.broadcasted_iota(jnp.int32, (ATT_ITEM, LANES), 1)
    feat = jnp.where(ln < 3, jj & (POS_RADIX - 1), jnp.where(ln < 6, jj >> POS_SHIFT, 0))
    feat = feat.astype(f32).astype(bf16)
    for blk in range(seq // ATT_ITEM):
        kaug_ref[blk * ATT_ITEM:(blk + 1) * ATT_ITEM, DIFF_VDIM:] = feat
    vta_ref[0:DIFF_VDIM, :] = vt_ref[...]
    row = lax.broadcasted_iota(jnp.int32, (VT_PAD, seq), 0)
    vta_ref[DIFF_VDIM:, :] = jnp.where(row == 0, 1.0, 0.0).astype(bf16)
    kk = lax.broadcasted_iota(jnp.int32, (tk, tq), 0)
    ii = lax.broadcasted_iota(jnp.int32, (tk, tq), 1)
    mask_ref[...] = jnp.where(kk > ii, -jnp.inf, 0.0).astype(f32)

    sl = jnp.broadcast_to(slope, (1, tq))
    s_hi = sl.astype(bf16).astype(f32)
    s_mid = (sl - s_hi).astype(bf16).astype(f32)
    s_lo = (sl - s_hi - s_mid).astype(bf16).astype(f32)
    r = lax.broadcasted_iota(jnp.int32, (VT_PAD, tq), 0)
    piece = jnp.where((r == 0) | (r == 3), s_hi, jnp.where((r == 1) | (r == 4), s_mid, s_lo))
    srows = jnp.where(r < 3, piece, jnp.where(r < 6, piece * POS_RADIX, 0.0)).astype(bf16)
    saug = jnp.concatenate([srows, jnp.zeros((LANES - VT_PAD, tq), bf16)], axis=0)
    zhalf = jnp.zeros((DIFF_QKDIM, tq), bf16)

    def query_operands(qt):
        return (jnp.concatenate([qt[0:DIFF_QKDIM], zhalf, saug], axis=0),
                jnp.concatenate([zhalf, qt[DIFF_QKDIM:], saug], axis=0))

    own = query_operands(qt_ref[:, 0:tq])
    qaug_ref[0] = own[0]
    qaug_ref[1] = own[1]
    qpos = lax.broadcasted_iota(jnp.int32, (1, tq), 1).astype(f32)
    lam = (jnp.exp(jnp.sum(lq1_ref[...] * lk1_ref[...], axis=-1, keepdims=True))
           - jnp.exp(jnp.sum(lq2_ref[...] * lk2_ref[...], axis=-1, keepdims=True)) + lam_init)

    def tile(qi, carry):
        _tile(qi)
        return carry

    def _tile(qi):
        qaug = (qaug_ref[0], qaug_ref[1])
        q_next0 = _aligned(jnp.minimum(qi + 1, n_q - 1) * tq, tq)
        qaug_next = query_operands(qt_ref[:, pl.ds(q_next0, tq)])

        m_ref[...] = jnp.full(m_ref.shape, -jnp.inf, f32)

        def scores(c, u0, n_units, mask_last, next_tile=False):
            qop = qaug_next[c] if next_tile else qaug[c]
            mx = None
            for u in range(n_units):
                k0 = _aligned((u0 + u) * tk, tk)
                t = jnp.dot(kaug_ref[pl.ds(k0, tk), :], qop, preferred_element_type=f32)
                if mask_last and u == n_units - 1:
                    t = t + mask_ref[...]
                t_ref[c, u * tk:(u + 1) * tk, :] = t
                tm = jnp.max(t, axis=0, keepdims=True)
                mx = tm if mx is None else jnp.maximum(mx, tm)
            tmax_ref[c] = mx

        def softmax_pv(par, c, u0, n_units):
            k0 = _aligned(u0 * tk, tk)
            nk = n_units * tk
            cq = slope * (jnp.asarray(u0 * tk - qi * tq, jnp.int32).astype(f32) - qpos)
            m_old = m_ref[c]
            m_new = jnp.maximum(m_old, tmax_ref[c] + cq)
            p = jnp.exp2(t_ref[c, 0:nk, :] - (m_new - cq))
            alpha = jnp.exp2(m_old - m_new)
            acc_ref[par, c] = alpha * acc_ref[par, c] + jnp.dot(
                vta_ref[:, pl.ds(k0, nk)], p.astype(bf16), preferred_element_type=f32)
            m_ref[c] = m_new

        units = ATT_ITEM // tk
        nb = qi // units
        tail0 = nb * units

        def run(n_tail):
            par = n_tail - 1

            @pl.when((qi % units == par) & (nb == 0))
            def _():
                first_tile = n_tail == 1
                acc_ref[par] = jnp.zeros(acc_ref.shape[1:], f32)
                if first_tile:
                    scores(0, 0, n_tail, True)
                scores(1, 0, n_tail, True)
                if not first_tile:
                    finish(qi - 1, 1 - par)
                softmax_pv(par, 0, 0, n_tail)
                scores(0, 0, units, first_tile, next_tile=True)
                softmax_pv(par, 1, 0, n_tail)

            @pl.when((qi % units == par) & (nb > 0))
            def _():
                acc_ref[par] = jnp.zeros(acc_ref.shape[1:], f32)

                def body(j, carry):
                    scores(1, j * units, units, False)
                    softmax_pv(par, 0, j * units, units)
                    scores(0, (j + 1) * units, units, False)
                    softmax_pv(par, 1, j * units, units)
                    return carry

                lax.fori_loop(0, nb - 1, body, 0)
                last = tail0 - units
                scores(1, last, units, False)
                finish(qi - 1, 1 - par)
                softmax_pv(par, 0, last, units)
                scores(0, tail0, n_tail, True)
                softmax_pv(par, 1, last, units)
                scores(1, tail0, n_tail, True)
                softmax_pv(par, 0, tail0, n_tail)
                scores(0, 0, units, False, next_tile=True)
                softmax_pv(par, 1, tail0, n_tail)

        for n_tail in range(1, units + 1):
            run(n_tail)

        qaug_ref[0] = qaug_next[0]
        qaug_ref[1] = qaug_next[1]

    def finish(q_tile, par):
        a0 = acc_ref[par, 0]
        a1 = acc_ref[par, 1]
        o = (a0[0:DIFF_VDIM] / a0[DIFF_VDIM:DIFF_VDIM + 1]
             - lam * (a1[0:DIFF_VDIM] / a1[DIFF_VDIM:DIFF_VDIM + 1]))
        ms = jnp.mean(o * o, axis=0, keepdims=True)
        on = (o * lax.rsqrt(ms + RMS_EPS)).T
        q0 = _aligned(q_tile * tq, tq)
        y = on * (sg_ref[...] * (1.0 - lam_init)) * gd_ref[pl.ds(q0, tq), :]
        o_ref[pl.ds(q0, tq), :] = y.astype(bf16)

    lax.fori_loop(0, n_q, tile, 0)
    finish(n_q - 1, (n_q - 1) % 2)


def _attention(qt, k, vt, gd, lq1, lk1, lq2, lk2, sg, lam_init):
    B, S, _ = k.shape
    assert ATT_TQ == ATT_TK and ATT_ITEM == 2 * ATT_TK and S % ATT_ITEM == 0
    tq = ATT_TQ
    grid = (B, DIFF_HEADS)
    vec = lambda b, h: (0, 0)
    rows = lambda b, h: (b, 0, h)
    cols = lambda b, h: (b, h, 0)
    in_specs = [
        pl.BlockSpec((None, DIFF_VDIM, S), cols),
        pl.BlockSpec((None, S, DIFF_VDIM), rows),
        pl.BlockSpec((None, DIFF_VDIM, S), cols),
        pl.BlockSpec((None, S, DIFF_VDIM), rows),
        pl.BlockSpec((1, DIFF_QKDIM), vec),
        pl.BlockSpec((1, DIFF_QKDIM), vec),
        pl.BlockSpec((1, DIFF_QKDIM), vec),
        pl.BlockSpec((1, DIFF_QKDIM), vec),
        pl.BlockSpec((1, DIFF_VDIM), vec),
    ]
    return pl.pallas_call(
        functools.partial(_attn_kernel, lam_init=lam_init),
        grid=grid,
        in_specs=in_specs,
        out_specs=pl.BlockSpec((None, S, DIFF_VDIM), rows),
        out_shape=jax.ShapeDtypeStruct((B, S, DIFF_WIDTH), jnp.bfloat16),
        scratch_shapes=[
            pltpu.VMEM((2, 2, DIFF_VDIM + VT_PAD, tq), jnp.float32),
            pltpu.VMEM((2, 1, tq), jnp.float32),
            pltpu.VMEM((ATT_TK, tq), jnp.float32),
            pltpu.VMEM((2, ATT_ITEM, tq), jnp.float32),
            pltpu.VMEM((2, 1, tq), jnp.float32),
            pltpu.VMEM((S, 2 * LANES), jnp.bfloat16),
            pltpu.VMEM((DIFF_VDIM + VT_PAD, S), jnp.bfloat16),
            pltpu.VMEM((2, 2 * LANES, tq), jnp.bfloat16),
        ],
        compiler_params=pltpu.CompilerParams(
            dimension_semantics=("arbitrary", "arbitrary"),
            vmem_limit_bytes=VMEM_LIMIT),
        name="diff_attn",
    )(qt, k, vt, gd, lq1, lk1, lq2, lk2, sg)


def _out_kernel(x_ref, yc_ref, yd_ref, w_ref, g_ref, o_ref):
    y = (jnp.dot(yc_ref[...], w_ref[0:CONV_WIDTH, :], preferred_element_type=jnp.float32)
         + jnp.dot(yd_ref[...], w_ref[CONV_WIDTH:, :], preferred_element_type=jnp.float32))
    ms = jnp.mean(y * y, axis=-1, keepdims=True)
    o_ref[...] = x_ref[...] + y * lax.rsqrt(ms + RMS_EPS) * g_ref[...]


def _out_proj(x, yc, yd, w_out_b, post_g):
    B, S, D = x.shape
    ts = OUT_TILE
    assert S % ts == 0
    row = lambda b, s: (b, s, 0)
    const2 = lambda b, s: (0, 0)
    return pl.pallas_call(
        _out_kernel,
        grid=(B, S // ts),
        in_specs=[
            pl.BlockSpec((None, ts, D), row),
            pl.BlockSpec((None, ts, CONV_WIDTH), row),
            pl.BlockSpec((None, ts, DIFF_WIDTH), row),
            pl.BlockSpec(w_out_b.shape, const2),
            pl.BlockSpec((1, D), const2),
        ],
        out_specs=pl.BlockSpec((None, ts, D), row),
        out_shape=jax.ShapeDtypeStruct((B, S, D), jnp.float32),
        compiler_params=pltpu.CompilerParams(
            dimension_semantics=("arbitrary", "arbitrary"),
            vmem_limit_bytes=VMEM_LIMIT),
        name="out_proj",
    )(x, yc, yd, w_out_b, post_g)


def kernel(x, pre_norm_g, w_in, conv_dw_w, conv_dw_b, conv_ln_g, conv_ln_b, conv_pw_w, conv_pw_b,
           lambda_q1, lambda_k1, lambda_q2, lambda_k2, diff_subln_g, w_out, post_norm_g):
    depth = pre_norm_g.shape[0]
    h = x
    for i in range(depth):
        w_in_b = w_in[i].astype(jnp.bfloat16)
        yc, qt, k, vt, gd = _proj_conv(
            h, pre_norm_g[i][None], w_in_b, conv_dw_w[i], conv_dw_b[i][None],
            conv_ln_g[i][None], conv_ln_b[i][None], conv_pw_w[i].astype(jnp.bfloat16), conv_pw_b[i][None])
        yd = _attention(qt, k, vt, gd, lambda_q1[i][None], lambda_k1[i][None], lambda_q2[i][None],
                        lambda_k2[i][None], diff_subln_g[i][None], _lambda_init(i))
        h = _out_proj(h, yc, yd, w_out[i].astype(jnp.bfloat16), post_norm_g[i][None])
    return h
```

```python
import functools
import math

import jax
import jax.numpy as jnp
from jax import lax
from jax.experimental import pallas as pl
from jax.experimental.pallas import tpu as pltpu

D_MODEL = 1024
CONV_WIDTH = 512
CONV_KERNEL = 31
DIFF_WIDTH = 512
DIFF_HEADS = 4
DIFF_VDIM = 128
DIFF_QKDIM = 64
RMS_EPS = 1e-6
LN_EPS = 1e-5


def _lambda_init(layer_idx):
    return 0.8 - 0.6 * math.exp(-0.3 * layer_idx)


LOG2E = math.log2(math.e)
Q_PRESCALE = (DIFF_QKDIM ** -0.5) * LOG2E

C_GLU = 0
C_CGATE = 1024
C_Q = 1536
C_K = 2048
C_V = 2560
C_DGATE = 3072

PROJ_TILE = 512
OUT_TILE = 1024
CONV_HALO = 32
CONV_ROWS = 256
LANES = 128
ATT_HEADS = 1
ATT_TQ = 512
ATT_TK = 512
ATT_ITEM = 1024
POS_SHIFT = 8
POS_RADIX = 1 << POS_SHIFT
VT_PAD = 16
VMEM_LIMIT = 48 * 1024 * 1024
ATT_VMEM_LIMIT = VMEM_LIMIT

_NT = (((1,), (1,)), ((), ()))


def _silu(x):
    return x * jax.nn.sigmoid(x)


def _aligned(start, multiple):
    return start if isinstance(start, int) else pl.multiple_of(start, multiple)


def _proj_conv_kernel(x_ref, g_ref, w_ref, dww_ref, dwb_ref, lng_ref, lnb_ref, pww_ref, pwb_ref,
                      yc_ref, qt_ref, k_ref, vt_ref, gd_ref, hbuf, cbuf, wqt_ref, wvt_ref):
    ts = x_ref.shape[0]
    s_idx = pl.program_id(1)

    @pl.when((pl.program_id(0) == 0) & (s_idx == 0))
    def _():
        wqt_ref[...] = w_ref[:, C_Q:C_Q + DIFF_WIDTH].T
        wvt_ref[...] = w_ref[:, C_V:C_V + DIFF_WIDTH].T

    x = x_ref[...]
    ms = jnp.mean(x * x, axis=-1, keepdims=True)
    xb = (x * lax.rsqrt(ms + RMS_EPS) * g_ref[...]).astype(jnp.bfloat16)

    def proj(c0, width):
        return jnp.dot(xb, w_ref[:, c0:c0 + width], preferred_element_type=jnp.float32)

    n_slab = CONV_WIDTH // LANES

    @pl.when(s_idx == 0)
    def _():
        hbuf[:, 0:CONV_HALO, :] = jnp.zeros((n_slab, CONV_HALO, LANES), jnp.float32)

    a = proj(C_GLU, CONV_WIDTH)
    b = proj(C_GLU + CONV_WIDTH, CONV_WIDTH)
    h = a * jax.nn.sigmoid(b)
    for l in range(n_slab):
        hbuf[l, CONV_HALO:CONV_HALO + ts, :] = h[:, l * LANES:(l + 1) * LANES]

    off = CONV_HALO - (CONV_KERNEL - 1)
    half = CONV_ROWS // 2

    def conv_block(l, r0):
        lanes = slice(l * LANES, (l + 1) * LANES)
        acc = [jnp.broadcast_to(dwb_ref[:, lanes], (half, LANES)) for _ in range(2)]
        for tau in range(CONV_KERNEL):
            w = dww_ref[tau:tau + 1, lanes]
            for par in range(2):
                acc[par] = acc[par] + w * hbuf[l, pl.ds(r0 + off + tau + par, half, stride=2), :]
        for par in range(2):
            cbuf[l, pl.ds(r0 + par, half, stride=2), :] = acc[par]

    qt_ref[...] = (lax.dot_general(wqt_ref[...], xb, _NT, preferred_element_type=jnp.float32)
                   * Q_PRESCALE).astype(jnp.bfloat16)
    k_ref[...] = proj(C_K, DIFF_WIDTH).astype(jnp.bfloat16)
    vt_ref[...] = lax.dot_general(wvt_ref[...], xb, _NT,
                                  preferred_element_type=jnp.float32).astype(jnp.bfloat16)
    gd_ref[...] = _silu(proj(C_DGATE, DIFF_WIDTH))
    gate_c = _silu(proj(C_CGATE, CONV_WIDTH))

    for r0 in range(0, ts, CONV_ROWS):
        for l in range(n_slab):
            conv_block(l, r0)

    hbuf[:, 0:CONV_HALO, :] = hbuf[:, ts:ts + CONV_HALO, :]

    cv = jnp.concatenate([cbuf[l] for l in range(n_slab)], axis=-1)
    mu = jnp.mean(cv, axis=-1, keepdims=True)
    d = cv - mu
    var = jnp.mean(d * d, axis=-1, keepdims=True)
    hn = d * lax.rsqrt(var + LN_EPS) * lng_ref[...] + lnb_ref[...]
    hs = _silu(hn).astype(jnp.bfloat16)
    yc = jnp.dot(hs, pww_ref[...], preferred_element_type=jnp.float32) + pwb_ref[...]
    yc_ref[...] = (yc * gate_c).astype(jnp.bfloat16)


def _proj_conv(x, pre_g, w_in_b, dw_w, dw_b, ln_g, ln_b, pw_w_b, pw_b):
    B, S, D = x.shape
    ts = PROJ_TILE
    assert S % ts == 0 and ts % CONV_ROWS == 0
    grid = (B, S // ts)
    row = lambda b, s: (b, s, 0)
    const2 = lambda b, s: (0, 0)
    in_specs = [
        pl.BlockSpec((None, ts, D), row),
        pl.BlockSpec((1, D), const2),
        pl.BlockSpec(w_in_b.shape, const2),
        pl.BlockSpec(dw_w.shape, const2),
        pl.BlockSpec((1, CONV_WIDTH), const2),
        pl.BlockSpec((1, CONV_WIDTH), const2),
        pl.BlockSpec((1, CONV_WIDTH), const2),
        pl.BlockSpec(pw_w_b.shape, const2),
        pl.BlockSpec((1, CONV_WIDTH), const2),
    ]
    out_specs = [
        pl.BlockSpec((None, ts, CONV_WIDTH), row),
        pl.BlockSpec((None, DIFF_WIDTH, ts), lambda b, s: (b, 0, s)),
        pl.BlockSpec((None, ts, DIFF_WIDTH), row),
        pl.BlockSpec((None, DIFF_WIDTH, ts), lambda b, s: (b, 0, s)),
        pl.BlockSpec((None, ts, DIFF_WIDTH), row),
    ]
    out_shape = [
        jax.ShapeDtypeStruct((B, S, CONV_WIDTH), jnp.bfloat16),
        jax.ShapeDtypeStruct((B, DIFF_WIDTH, S), jnp.bfloat16),
        jax.ShapeDtypeStruct((B, S, DIFF_WIDTH), jnp.bfloat16),
        jax.ShapeDtypeStruct((B, DIFF_WIDTH, S), jnp.bfloat16),
        jax.ShapeDtypeStruct((B, S, DIFF_WIDTH), jnp.float32),
    ]
    return pl.pallas_call(
        _proj_conv_kernel,
        grid=grid,
        in_specs=in_specs,
        out_specs=out_specs,
        out_shape=out_shape,
        scratch_shapes=[
            pltpu.VMEM((CONV_WIDTH // LANES, CONV_HALO + ts, LANES), jnp.float32),
            pltpu.VMEM((CONV_WIDTH // LANES, ts, LANES), jnp.float32),
            pltpu.VMEM((DIFF_WIDTH, D), jnp.bfloat16),
            pltpu.VMEM((DIFF_WIDTH, D), jnp.bfloat16),
        ],
        compiler_params=pltpu.CompilerParams(
            dimension_semantics=("arbitrary", "arbitrary"),
            vmem_limit_bytes=VMEM_LIMIT),
        name="proj_conv",
    )(x, pre_g, w_in_b, dw_w, dw_b, ln_g, ln_b, pw_w_b, pw_b)


def _attn_kernel(qt_ref, k_ref, vt_ref, gd_ref, lq1_ref, lk1_ref, lq2_ref, lk2_ref, sg_ref,
                 o_ref, acc_ref, m_ref, mask_ref, t_ref, tmax_ref, kaug_ref, vta_ref, qaug_ref,
                 *, lam_init):
    tq = ATT_TQ
    tk = ATT_TK
    seq = k_ref.shape[0]
    n_q = seq // tq
    group = pl.program_id(1)
    heads = range(ATT_HEADS)
    f32, bf16 = jnp.float32, jnp.bfloat16

    def head_rows(e):
        return slice(e * DIFF_VDIM, (e + 1) * DIFF_VDIM)

    slopes = []
    for e in heads:
        hv = jnp.full((1, 1), group * ATT_HEADS + e, jnp.int32).astype(f32)
        slopes.append(jnp.exp2(-(8.0 / DIFF_HEADS) * (hv + 1.0)) * LOG2E)

    @pl.when((pl.program_id(0) == 0) & (group == 0))
    def _():
        jj = lax.broadcasted_iota(jnp.int32, (ATT_ITEM, LANES), 0)
        ln = lax.broadcasted_iota(jnp.int32, (ATT_ITEM, LANES), 1)
        feat = jnp.where(ln < 3, jj & (POS_RADIX - 1), jnp.where(ln < 6, jj >> POS_SHIFT, 0))
        feat = feat.astype(f32).astype(bf16)
        row = lax.broadcasted_iota(jnp.int32, (VT_PAD, seq), 0)
        ones_row = jnp.where(row == 0, 1.0, 0.0).astype(bf16)
        for e in heads:
            for blk in range(seq // ATT_ITEM):
                kaug_ref[e, blk * ATT_ITEM:(blk + 1) * ATT_ITEM, DIFF_VDIM:] = feat
            vta_ref[e, DIFF_VDIM:, :] = ones_row
        kk = lax.broadcasted_iota(jnp.int32, (tk, tq), 0)
        ii = lax.broadcasted_iota(jnp.int32, (tk, tq), 1)
        mask_ref[...] = jnp.where(kk > ii, -jnp.inf, 0.0).astype(f32)

    for e in heads:
        kaug_ref[e, :, 0:DIFF_VDIM] = k_ref[:, head_rows(e)]
        vta_ref[e, 0:DIFF_VDIM, :] = vt_ref[head_rows(e), :]

    r = lax.broadcasted_iota(jnp.int32, (VT_PAD, tq), 0)
    zhalf = jnp.zeros((DIFF_QKDIM, tq), bf16)
    saug = []
    for e in heads:
        sl = jnp.broadcast_to(slopes[e], (1, tq))
        s_hi = sl.astype(bf16).astype(f32)
        s_mid = (sl - s_hi).astype(bf16).astype(f32)
        s_lo = (sl - s_hi - s_mid).astype(bf16).astype(f32)
        piece = jnp.where((r == 0) | (r == 3), s_hi, jnp.where((r == 1) | (r == 4), s_mid, s_lo))
        srows = jnp.where(r < 3, piece, jnp.where(r < 6, piece * POS_RADIX, 0.0)).astype(bf16)
        saug.append(jnp.concatenate([srows, jnp.zeros((LANES - VT_PAD, tq), bf16)], axis=0))

    def query_operands(e, q0):
        qt = qt_ref[head_rows(e), pl.ds(q0, tq)]
        return (jnp.concatenate([qt[0:DIFF_QKDIM], zhalf, saug[e]], axis=0),
                jnp.concatenate([zhalf, qt[DIFF_QKDIM:], saug[e]], axis=0))

    for e in heads:
        own = query_operands(e, 0)
        qaug_ref[e, 0] = own[0]
        qaug_ref[e, 1] = own[1]
    qpos = lax.broadcasted_iota(jnp.int32, (1, tq), 1).astype(f32)
    lam = (jnp.exp(jnp.sum(lq1_ref[...] * lk1_ref[...], axis=-1, keepdims=True))
           - jnp.exp(jnp.sum(lq2_ref[...] * lk2_ref[...], axis=-1, keepdims=True)) + lam_init)

    def tile(qi, carry):
        _tile(qi)
        return carry

    def _tile(qi):
        qaug = [(qaug_ref[e, 0], qaug_ref[e, 1]) for e in heads]
        q_next0 = _aligned(jnp.minimum(qi + 1, n_q - 1) * tq, tq)
        qaug_next = [query_operands(e, q_next0) for e in heads]

        m_ref[...] = jnp.full(m_ref.shape, -jnp.inf, f32)

        def scores(c, u0, n_units, mask_last, next_tile=False):
            for e in heads:
                qop = qaug_next[e][c] if next_tile else qaug[e][c]
                mx = None
                for u in range(n_units):
                    k0 = _aligned((u0 + u) * tk, tk)
                    t = jnp.dot(kaug_ref[e, pl.ds(k0, tk), :], qop, preferred_element_type=f32)
                    if mask_last and u == n_units - 1:
                        t = t + mask_ref[...]
                    t_ref[e, c, u * tk:(u + 1) * tk, :] = t
                    tm = jnp.max(t, axis=0, keepdims=True)
                    mx = tm if mx is None else jnp.maximum(mx, tm)
                tmax_ref[e, c] = mx

        def softmax_pv(par, c, u0, n_units):
            k0 = _aligned(u0 * tk, tk)
            nk = n_units * tk
            for e in heads:
                cq = slopes[e] * (jnp.asarray(u0 * tk - qi * tq, jnp.int32).astype(f32) - qpos)
                m_old = m_ref[e, c]
                m_new = jnp.maximum(m_old, tmax_ref[e, c] + cq)
                p = jnp.exp2(t_ref[e, c, 0:nk, :] - (m_new - cq))
                alpha = jnp.exp2(m_old - m_new)
                acc_ref[e, par, c] = alpha * acc_ref[e, par, c] + jnp.dot(
                    vta_ref[e, :, pl.ds(k0, nk)], p.astype(bf16), preferred_element_type=f32)
                m_ref[e, c] = m_new

        units = ATT_ITEM // tk
        nb = qi // units
        tail0 = nb * units

        def run(n_tail):
            par = n_tail - 1

            def clear_sums():
                for e in heads:
                    acc_ref[e, par] = jnp.zeros(acc_ref.shape[2:], f32)

            @pl.when((qi % units == par) & (nb == 0))
            def _():
                first_tile = n_tail == 1
                clear_sums()
                if first_tile:
                    scores(0, 0, n_tail, True)
                scores(1, 0, n_tail, True)
                if not first_tile:
                    finish(qi - 1, 1 - par)
                softmax_pv(par, 0, 0, n_tail)
                scores(0, 0, units, first_tile, next_tile=True)
                softmax_pv(par, 1, 0, n_tail)

            @pl.when((qi % units == par) & (nb > 0))
            def _():
                clear_sums()

                def body(j, carry):
                    scores(1, j * units, units, False)
                    softmax_pv(par, 0, j * units, units)
                    scores(0, (j + 1) * units, units, False)
                    softmax_pv(par, 1, j * units, units)
                    return carry

                lax.fori_loop(0, nb - 1, body, 0)
                last = tail0 - units
                scores(1, last, units, False)
                finish(qi - 1, 1 - par)
                softmax_pv(par, 0, last, units)
                scores(0, tail0, n_tail, True)
                softmax_pv(par, 1, last, units)
                scores(1, tail0, n_tail, True)
                softmax_pv(par, 0, tail0, n_tail)
                scores(0, 0, units, False, next_tile=True)
                softmax_pv(par, 1, tail0, n_tail)

        for n_tail in range(1, units + 1):
            run(n_tail)

        for e in heads:
            qaug_ref[e, 0] = qaug_next[e][0]
            qaug_ref[e, 1] = qaug_next[e][1]

    def finish(q_tile, par):
        q0 = _aligned(q_tile * tq, tq)
        for e in heads:
            a0 = acc_ref[e, par, 0]
            a1 = acc_ref[e, par, 1]
            o = (a0[0:DIFF_VDIM] / a0[DIFF_VDIM:DIFF_VDIM + 1]
                 - lam * (a1[0:DIFF_VDIM] / a1[DIFF_VDIM:DIFF_VDIM + 1]))
            ms = jnp.mean(o * o, axis=0, keepdims=True)
            on = (o * lax.rsqrt(ms + RMS_EPS)).T
            y = on * (sg_ref[...] * (1.0 - lam_init)) * gd_ref[pl.ds(q0, tq), head_rows(e)]
            o_ref[pl.ds(q0, tq), head_rows(e)] = y.astype(bf16)

    lax.fori_loop(0, n_q, tile, 0)
    finish(n_q - 1, (n_q - 1) % 2)


def _attention(qt, k, vt, gd, lq1, lk1, lq2, lk2, sg, lam_init):
    B, S, _ = k.shape
    assert ATT_TQ == ATT_TK and ATT_ITEM == 2 * ATT_TK and S % ATT_ITEM == 0
    assert DIFF_HEADS % ATT_HEADS == 0
    tq = ATT_TQ
    width = ATT_HEADS * DIFF_VDIM
    grid = (B, DIFF_HEADS // ATT_HEADS)
    vec = lambda b, g: (0, 0)
    rows = lambda b, g: (b, 0, g)
    cols = lambda b, g: (b, g, 0)
    in_specs = [
        pl.BlockSpec((None, width, S), cols),
        pl.BlockSpec((None, S, width), rows),
        pl.BlockSpec((None, width, S), cols),
        pl.BlockSpec((None, S, width), rows),
        pl.BlockSpec((1, DIFF_QKDIM), vec),
        pl.BlockSpec((1, DIFF_QKDIM), vec),
        pl.BlockSpec((1, DIFF_QKDIM), vec),
        pl.BlockSpec((1, DIFF_QKDIM), vec),
        pl.BlockSpec((1, DIFF_VDIM), vec),
    ]
    n_h = ATT_HEADS
    return pl.pallas_call(
        functools.partial(_attn_kernel, lam_init=lam_init),
        grid=grid,
        in_specs=in_specs,
        out_specs=pl.BlockSpec((None, S, width), rows),
        out_shape=jax.ShapeDtypeStruct((B, S, DIFF_WIDTH), jnp.bfloat16),
        scratch_shapes=[
            pltpu.VMEM((n_h, 2, 2, DIFF_VDIM + VT_PAD, tq), jnp.float32),
            pltpu.VMEM((n_h, 2, 1, tq), jnp.float32),
            pltpu.VMEM((ATT_TK, tq), jnp.float32),
            pltpu.VMEM((n_h, 2, ATT_ITEM, tq), jnp.float32),
            pltpu.VMEM((n_h, 2, 1, tq), jnp.float32),
            pltpu.VMEM((n_h, S, 2 * LANES), jnp.bfloat16),
            pltpu.VMEM((n_h, DIFF_VDIM + VT_PAD, S), jnp.bfloat16),
            pltpu.VMEM((n_h, 2, 2 * LANES, tq), jnp.bfloat16),
        ],
        compiler_params=pltpu.CompilerParams(
            dimension_semantics=("arbitrary", "arbitrary"),
            vmem_limit_bytes=ATT_VMEM_LIMIT),
        name="diff_attn",
    )(qt, k, vt, gd, lq1, lk1, lq2, lk2, sg)


def _out_kernel(x_ref, yc_ref, yd_ref, w_ref, g_ref, o_ref):
    y = (jnp.dot(yc_ref[...], w_ref[0:CONV_WIDTH, :], preferred_element_type=jnp.float32)
         + jnp.dot(yd_ref[...], w_ref[CONV_WIDTH:, :], preferred_element_type=jnp.float32))
    ms = jnp.mean(y * y, axis=-1, keepdims=True)
    o_ref[...] = x_ref[...] + y * lax.rsqrt(ms + RMS_EPS) * g_ref[...]


def _out_proj(x, yc, yd, w_out_b, post_g):
    B, S, D = x.shape
    ts = OUT_TILE
    assert S % ts == 0
    row = lambda b, s: (b, s, 0)
    const2 = lambda b, s: (0, 0)
    return pl.pallas_call(
        _out_kernel,
        grid=(B, S // ts),
        in_specs=[
            pl.BlockSpec((None, ts, D), row),
            pl.BlockSpec((None, ts, CONV_WIDTH), row),
            pl.BlockSpec((None, ts, DIFF_WIDTH), row),
            pl.BlockSpec(w_out_b.shape, const2),
            pl.BlockSpec((1, D), const2),
        ],
        out_specs=pl.BlockSpec((None, ts, D), row),
        out_shape=jax.ShapeDtypeStruct((B, S, D), jnp.float32),
        compiler_params=pltpu.CompilerParams(
            dimension_semantics=("arbitrary", "arbitrary"),
            vmem_limit_bytes=VMEM_LIMIT),
        name="out_proj",
    )(x, yc, yd, w_out_b, post_g)


def kernel(x, pre_norm_g, w_in, conv_dw_w, conv_dw_b, conv_ln_g, conv_ln_b, conv_pw_w, conv_pw_b,
           lambda_q1, lambda_k1, lambda_q2, lambda_k2, diff_subln_g, w_out, post_norm_g):
    depth = pre_norm_g.shape[0]
    h = x
    for i in range(depth):
        w_in_b = w_in[i].astype(jnp.bfloat16)
        yc, qt, k, vt, gd = _proj_conv(
            h, pre_norm_g[i][None], w_in_b, conv_dw_w[i], conv_dw_b[i][None],
            conv_ln_g[i][None], conv_ln_b[i][None], conv_pw_w[i].astype(jnp.bfloat16), conv_pw_b[i][None])
        yd = _attention(qt, k, vt, gd, lambda_q1[i][None], lambda_k1[i][None], lambda_q2[i][None],
                        lambda_k2[i][None], diff_subln_g[i][None], _lambda_init(i))
        h = _out_proj(h, yc, yd, w_out[i].astype(jnp.bfloat16), post_norm_g[i][None])
    return h
```

```python
import functools
import math

import jax
import jax.numpy as jnp
from jax import lax
from jax.experimental import pallas as pl
from jax.experimental.pallas import tpu as pltpu

D_MODEL = 1024
CONV_WIDTH = 512
CONV_KERNEL = 31
DIFF_WIDTH = 512
DIFF_HEADS = 4
DIFF_VDIM = 128
DIFF_QKDIM = 64
RMS_EPS = 1e-6
LN_EPS = 1e-5


def _lambda_init(layer_idx):
    return 0.8 - 0.6 * math.exp(-0.3 * layer_idx)


LOG2E = math.log2(math.e)
Q_PRESCALE = (DIFF_QKDIM ** -0.5) * LOG2E

C_GLU = 0
C_CGATE = 1024
C_Q = 1536
C_K = 2048
C_V = 2560
C_DGATE = 3072

PROJ_TILE = 512
OUT_TILE = 1024
CONV_HALO = 32
CONV_ROWS = 256
LANES = 128
ATT_HEADS = 1
ATT_TQ = 512
ATT_TK = 512
ATT_ITEM = 1024
POS_SHIFT = 8
POS_RADIX = 1 << POS_SHIFT
SLOPE_PIECES = 3
VT_PAD = 16
VMEM_LIMIT = 48 * 1024 * 1024

_NT = (((1,), (1,)), ((), ()))


def _silu(x):
    return x * jax.nn.sigmoid(x)


def _aligned(start, multiple):
    return start if isinstance(start, int) else pl.multiple_of(start, multiple)


def _proj_conv_kernel(x_ref, g_ref, w_ref, dww_ref, dwb_ref, lng_ref, lnb_ref, pww_ref, pwb_ref,
                      yc_ref, qt_ref, k_ref, vt_ref, gd_ref, hbuf, cbuf, wqt_ref, wvt_ref):
    ts = x_ref.shape[0]
    s_idx = pl.program_id(1)

    @pl.when((pl.program_id(0) == 0) & (s_idx == 0))
    def _():
        wqt_ref[...] = w_ref[:, C_Q:C_Q + DIFF_WIDTH].T
        wvt_ref[...] = w_ref[:, C_V:C_V + DIFF_WIDTH].T

    x = x_ref[...]
    ms = jnp.mean(x * x, axis=-1, keepdims=True)
    xb = (x * lax.rsqrt(ms + RMS_EPS) * g_ref[...]).astype(jnp.bfloat16)

    def proj(c0, width):
        return jnp.dot(xb, w_ref[:, c0:c0 + width], preferred_element_type=jnp.float32)

    n_slab = CONV_WIDTH // LANES

    @pl.when(s_idx == 0)
    def _():
        hbuf[:, 0:CONV_HALO, :] = jnp.zeros((n_slab, CONV_HALO, LANES), jnp.float32)

    a = proj(C_GLU, CONV_WIDTH)
    b = proj(C_GLU + CONV_WIDTH, CONV_WIDTH)
    h = a * jax.nn.sigmoid(b)
    for l in range(n_slab):
        hbuf[l, CONV_HALO:CONV_HALO + ts, :] = h[:, l * LANES:(l + 1) * LANES]

    off = CONV_HALO - (CONV_KERNEL - 1)
    half = CONV_ROWS // 2

    def conv_block(l, r0):
        lanes = slice(l * LANES, (l + 1) * LANES)
        acc = [jnp.broadcast_to(dwb_ref[:, lanes], (half, LANES)) for _ in range(2)]
        for tau in range(CONV_KERNEL):
            w = dww_ref[tau:tau + 1, lanes]
            for par in range(2):
                acc[par] = acc[par] + w * hbuf[l, pl.ds(r0 + off + tau + par, half, stride=2), :]
        for par in range(2):
            cbuf[l, pl.ds(r0 + par, half, stride=2), :] = acc[par]

    qt_ref[...] = (lax.dot_general(wqt_ref[...], xb, _NT, preferred_element_type=jnp.float32)
                   * Q_PRESCALE).astype(jnp.bfloat16)
    k_ref[...] = proj(C_K, DIFF_WIDTH).astype(jnp.bfloat16)
    vt_ref[...] = lax.dot_general(wvt_ref[...], xb, _NT,
                                  preferred_element_type=jnp.float32).astype(jnp.bfloat16)
    gd_ref[...] = _silu(proj(C_DGATE, DIFF_WIDTH))
    gate_c = _silu(proj(C_CGATE, CONV_WIDTH))

    for r0 in range(0, ts, CONV_ROWS):
        for l in range(n_slab):
            conv_block(l, r0)

    hbuf[:, 0:CONV_HALO, :] = hbuf[:, ts:ts + CONV_HALO, :]

    cv = jnp.concatenate([cbuf[l] for l in range(n_slab)], axis=-1)
    mu = jnp.mean(cv, axis=-1, keepdims=True)
    d = cv - mu
    var = jnp.mean(d * d, axis=-1, keepdims=True)
    hn = d * lax.rsqrt(var + LN_EPS) * lng_ref[...] + lnb_ref[...]
    hs = _silu(hn).astype(jnp.bfloat16)
    yc = jnp.dot(hs, pww_ref[...], preferred_element_type=jnp.float32) + pwb_ref[...]
    yc_ref[...] = (yc * gate_c).astype(jnp.bfloat16)


def _proj_conv(x, pre_g, w_in_b, dw_w, dw_b, ln_g, ln_b, pw_w_b, pw_b):
    B, S, D = x.shape
    ts = PROJ_TILE
    assert S % ts == 0 and ts % CONV_ROWS == 0
    grid = (B, S // ts)
    row = lambda b, s: (b, s, 0)
    const2 = lambda b, s: (0, 0)
    in_specs = [
        pl.BlockSpec((None, ts, D), row),
        pl.BlockSpec((1, D), const2),
        pl.BlockSpec(w_in_b.shape, const2),
        pl.BlockSpec(dw_w.shape, const2),
        pl.BlockSpec((1, CONV_WIDTH), const2),
        pl.BlockSpec((1, CONV_WIDTH), const2),
        pl.BlockSpec((1, CONV_WIDTH), const2),
        pl.BlockSpec(pw_w_b.shape, const2),
        pl.BlockSpec((1, CONV_WIDTH), const2),
    ]
    out_specs = [
        pl.BlockSpec((None, ts, CONV_WIDTH), row),
        pl.BlockSpec((None, DIFF_WIDTH, ts), lambda b, s: (b, 0, s)),
        pl.BlockSpec((None, ts, DIFF_WIDTH), row),
        pl.BlockSpec((None, DIFF_WIDTH, ts), lambda b, s: (b, 0, s)),
        pl.BlockSpec((None, ts, DIFF_WIDTH), row),
    ]
    out_shape = [
        jax.ShapeDtypeStruct((B, S, CONV_WIDTH), jnp.bfloat16),
        jax.ShapeDtypeStruct((B, DIFF_WIDTH, S), jnp.bfloat16),
        jax.ShapeDtypeStruct((B, S, DIFF_WIDTH), jnp.bfloat16),
        jax.ShapeDtypeStruct((B, DIFF_WIDTH, S), jnp.bfloat16),
        jax.ShapeDtypeStruct((B, S, DIFF_WIDTH), jnp.float32),
    ]
    return pl.pallas_call(
        _proj_conv_kernel,
        grid=grid,
        in_specs=in_specs,
        out_specs=out_specs,
        out_shape=out_shape,
        scratch_shapes=[
            pltpu.VMEM((CONV_WIDTH // LANES, CONV_HALO + ts, LANES), jnp.float32),
            pltpu.VMEM((CONV_WIDTH // LANES, ts, LANES), jnp.float32),
            pltpu.VMEM((DIFF_WIDTH, D), jnp.bfloat16),
            pltpu.VMEM((DIFF_WIDTH, D), jnp.bfloat16),
        ],
        compiler_params=pltpu.CompilerParams(
            dimension_semantics=("arbitrary", "arbitrary"),
            vmem_limit_bytes=VMEM_LIMIT),
        name="proj_conv",
    )(x, pre_g, w_in_b, dw_w, dw_b, ln_g, ln_b, pw_w_b, pw_b)


def _attn_kernel(qt_ref, k_ref, vt_ref, gd_ref, lq1_ref, lk1_ref, lq2_ref, lk2_ref, sg_ref,
                 o_ref, acc_ref, m_ref, mask_ref, t_ref, tmax_ref, kaug_ref, vta_ref, qaug_ref,
                 *, lam_init):
    tq = ATT_TQ
    tk = ATT_TK
    seq = k_ref.shape[0]
    n_q = seq // tq
    group = pl.program_id(1)
    heads = range(ATT_HEADS)
    f32, bf16 = jnp.float32, jnp.bfloat16

    def head_rows(e):
        return slice(e * DIFF_VDIM, (e + 1) * DIFF_VDIM)

    slopes = []
    for e in heads:
        hv = jnp.full((1, 1), group * ATT_HEADS + e, jnp.int32).astype(f32)
        slopes.append(jnp.exp2(-(8.0 / DIFF_HEADS) * (hv + 1.0)) * LOG2E)

    @pl.when((pl.program_id(0) == 0) & (group == 0))
    def _():
        jj = lax.broadcasted_iota(jnp.int32, (ATT_ITEM, LANES), 0)
        ln = lax.broadcasted_iota(jnp.int32, (ATT_ITEM, LANES), 1)
        feat = jnp.where(ln < SLOPE_PIECES, jj & (POS_RADIX - 1),
                         jnp.where(ln < 2 * SLOPE_PIECES, jj >> POS_SHIFT, 0))
        feat = feat.astype(f32).astype(bf16)
        row = lax.broadcasted_iota(jnp.int32, (VT_PAD, seq), 0)
        ones_row = jnp.where(row == 0, 1.0, 0.0).astype(bf16)
        for e in heads:
            for blk in range(seq // ATT_ITEM):
                kaug_ref[e, blk * ATT_ITEM:(blk + 1) * ATT_ITEM, DIFF_VDIM:] = feat
            vta_ref[e, DIFF_VDIM:, :] = ones_row
        kk = lax.broadcasted_iota(jnp.int32, (tk, tq), 0)
        ii = lax.broadcasted_iota(jnp.int32, (tk, tq), 1)
        mask_ref[...] = jnp.where(kk > ii, -jnp.inf, 0.0).astype(f32)

    for e in heads:
        kaug_ref[e, :, 0:DIFF_VDIM] = k_ref[:, head_rows(e)]
        vta_ref[e, 0:DIFF_VDIM, :] = vt_ref[head_rows(e), :]

    r = lax.broadcasted_iota(jnp.int32, (VT_PAD, tq), 0)
    zhalf = jnp.zeros((DIFF_QKDIM, tq), bf16)
    saug = []
    for e in heads:
        sl = jnp.broadcast_to(slopes[e], (1, tq))
        s_hi = sl.astype(bf16).astype(f32)
        s_mid = (sl - s_hi).astype(bf16).astype(f32)
        s_lo = (sl - s_hi - s_mid).astype(bf16).astype(f32)
        piece = jnp.where((r == 0) | (r == 3), s_hi, jnp.where((r == 1) | (r == 4), s_mid, s_lo))
        srows = jnp.where(r < SLOPE_PIECES, piece,
                          jnp.where(r < 2 * SLOPE_PIECES, piece * POS_RADIX, 0.0)).astype(bf16)
        saug.append(jnp.concatenate([srows, jnp.zeros((LANES - VT_PAD, tq), bf16)], axis=0))

    def query_operands(e, q0):
        qt = qt_ref[head_rows(e), pl.ds(q0, tq)]
        return (jnp.concatenate([qt[0:DIFF_QKDIM], zhalf, saug[e]], axis=0),
                jnp.concatenate([zhalf, qt[DIFF_QKDIM:], saug[e]], axis=0))

    for e in heads:
        own = query_operands(e, 0)
        qaug_ref[e, 0] = own[0]
        qaug_ref[e, 1] = own[1]
    qpos = lax.broadcasted_iota(jnp.int32, (1, tq), 1).astype(f32)
    lam = (jnp.exp(jnp.sum(lq1_ref[...] * lk1_ref[...], axis=-1, keepdims=True))
           - jnp.exp(jnp.sum(lq2_ref[...] * lk2_ref[...], axis=-1, keepdims=True)) + lam_init)

    def tile(qi, carry):
        _tile(qi)
        return carry

    def _tile(qi, has_next=True):
        qaug = [(qaug_ref[e, 0], qaug_ref[e, 1]) for e in heads]
        if has_next:
            qaug_next = [query_operands(e, _aligned((qi + 1) * tq, tq)) for e in heads]

        m_ref[...] = jnp.full(m_ref.shape, -jnp.inf, f32)

        def scores(c, u0, n_units, mask_last, next_tile=False):
            if next_tile and not has_next:
                return
            for e in heads:
                qop = qaug_next[e][c] if next_tile else qaug[e][c]
                mx = None
                for u in range(n_units):
                    k0 = _aligned((u0 + u) * tk, tk)
                    t = jnp.dot(kaug_ref[e, pl.ds(k0, tk), :], qop, preferred_element_type=f32)
                    if mask_last and u == n_units - 1:
                        t = t + mask_ref[...]
                    t_ref[e, c, u * tk:(u + 1) * tk, :] = t
                    tm = jnp.max(t, axis=0, keepdims=True)
                    mx = tm if mx is None else jnp.maximum(mx, tm)
                tmax_ref[e, c] = mx

        def softmax_pv(par, c, u0, n_units):
            k0 = _aligned(u0 * tk, tk)
            nk = n_units * tk
            for e in heads:
                cq = slopes[e] * (jnp.asarray(u0 * tk - qi * tq, jnp.int32).astype(f32) - qpos)
                m_old = m_ref[e, c]
                m_new = jnp.maximum(m_old, tmax_ref[e, c] + cq)
                p = jnp.exp2(t_ref[e, c, 0:nk, :] - (m_new - cq))
                alpha = jnp.exp2(m_old - m_new)
                acc_ref[e, par, c] = alpha * acc_ref[e, par, c] + jnp.dot(
                    vta_ref[e, :, pl.ds(k0, nk)], p.astype(bf16), preferred_element_type=f32)
                m_ref[e, c] = m_new

        units = ATT_ITEM // tk
        nb = qi // units
        tail0 = nb * units

        def run(n_tail):
            par = (n_tail - 1) % 2

            def clear_sums():
                for e in heads:
                    acc_ref[e, par] = jnp.zeros(acc_ref.shape[2:], f32)

            @pl.when((qi % units == n_tail - 1) & (nb == 0))
            def _():
                first_tile = n_tail == 1
                clear_sums()
                if first_tile:
                    scores(0, 0, n_tail, True)
                scores(1, 0, n_tail, True)
                if not first_tile:
                    finish(qi - 1, 1 - par)
                softmax_pv(par, 0, 0, n_tail)
                if n_tail < units:
                    scores(0, 0, n_tail + 1, True, next_tile=True)
                else:
                    scores(0, 0, units, False, next_tile=True)
                softmax_pv(par, 1, 0, n_tail)

            @pl.when((qi % units == n_tail - 1) & (nb > 0))
            def _():
                clear_sums()

                def body(j, carry):
                    scores(1, j * units, units, False)
                    softmax_pv(par, 0, j * units, units)
                    scores(0, (j + 1) * units, units, False)
                    softmax_pv(par, 1, j * units, units)
                    return carry

                lax.fori_loop(0, nb - 1, body, 0)
                last = tail0 - units
                scores(1, last, units, False)
                finish(qi - 1, 1 - par)
                softmax_pv(par, 0, last, units)
                scores(0, tail0, n_tail, True)
                softmax_pv(par, 1, last, units)
                scores(1, tail0, n_tail, True)
                softmax_pv(par, 0, tail0, n_tail)
                scores(0, 0, units, False, next_tile=True)
                softmax_pv(par, 1, tail0, n_tail)

        for n_tail in range(1, units + 1):
            run(n_tail)

        if has_next:
            for e in heads:
                qaug_ref[e, 0] = qaug_next[e][0]
                qaug_ref[e, 1] = qaug_next[e][1]

    def finish(q_tile, par):
        q0 = _aligned(q_tile * tq, tq)
        for e in heads:
            a0 = acc_ref[e, par, 0]
            a1 = acc_ref[e, par, 1]
            o = (a0[0:DIFF_VDIM] / a0[DIFF_VDIM:DIFF_VDIM + 1]
                 - lam * (a1[0:DIFF_VDIM] / a1[DIFF_VDIM:DIFF_VDIM + 1]))
            ms = jnp.mean(o * o, axis=0, keepdims=True)
            on = (o * lax.rsqrt(ms + RMS_EPS)).T
            y = on * (sg_ref[...] * (1.0 - lam_init)) * gd_ref[pl.ds(q0, tq), head_rows(e)]
            o_ref[pl.ds(q0, tq), head_rows(e)] = y.astype(bf16)

    lax.fori_loop(0, n_q - 1, tile, 0)
    _tile(n_q - 1, has_next=False)
    finish(n_q - 1, (n_q - 1) % 2)


def _attention(qt, k, vt, gd, lq1, lk1, lq2, lk2, sg, lam_init):
    B, S, _ = k.shape
    assert ATT_TQ == ATT_TK and ATT_ITEM % ATT_TK == 0 and S % ATT_ITEM == 0
    assert ATT_ITEM <= POS_RADIX * POS_RADIX
    assert DIFF_HEADS % ATT_HEADS == 0
    tq = ATT_TQ
    width = ATT_HEADS * DIFF_VDIM
    grid = (B, DIFF_HEADS // ATT_HEADS)
    vec = lambda b, g: (0, 0)
    rows = lambda b, g: (b, 0, g)
    cols = lambda b, g: (b, g, 0)
    in_specs = [
        pl.BlockSpec((None, width, S), cols),
        pl.BlockSpec((None, S, width), rows),
        pl.BlockSpec((None, width, S), cols),
        pl.BlockSpec((None, S, width), rows),
        pl.BlockSpec((1, DIFF_QKDIM), vec),
        pl.BlockSpec((1, DIFF_QKDIM), vec),
        pl.BlockSpec((1, DIFF_QKDIM), vec),
        pl.BlockSpec((1, DIFF_QKDIM), vec),
        pl.BlockSpec((1, DIFF_VDIM), vec),
    ]
    n_h = ATT_HEADS
    return pl.pallas_call(
        functools.partial(_attn_kernel, lam_init=lam_init),
        grid=grid,
        in_specs=in_specs,
        out_specs=pl.BlockSpec((None, S, width), rows),
        out_shape=jax.ShapeDtypeStruct((B, S, DIFF_WIDTH), jnp.bfloat16),
        scratch_shapes=[
            pltpu.VMEM((n_h, 2, 2, DIFF_VDIM + VT_PAD, tq), jnp.float32),
            pltpu.VMEM((n_h, 2, 1, tq), jnp.float32),
            pltpu.VMEM((ATT_TK, tq), jnp.float32),
            pltpu.VMEM((n_h, 2, ATT_ITEM, tq), jnp.float32),
            pltpu.VMEM((n_h, 2, 1, tq), jnp.float32),
            pltpu.VMEM((n_h, S, 2 * LANES), jnp.bfloat16),
            pltpu.VMEM((n_h, DIFF_VDIM + VT_PAD, S), jnp.bfloat16),
            pltpu.VMEM((n_h, 2, 2 * LANES, tq), jnp.bfloat16),
        ],
        compiler_params=pltpu.CompilerParams(
            dimension_semantics=("arbitrary", "arbitrary"),
            vmem_limit_bytes=VMEM_LIMIT),
        name="diff_attn",
    )(qt, k, vt, gd, lq1, lk1, lq2, lk2, sg)


def _out_kernel(x_ref, yc_ref, yd_ref, w_ref, g_ref, o_ref):
    y = (jnp.dot(yc_ref[...], w_ref[0:CONV_WIDTH, :], preferred_element_type=jnp.float32)
         + jnp.dot(yd_ref[...], w_ref[CONV_WIDTH:, :], preferred_element_type=jnp.float32))
    ms = jnp.mean(y * y, axis=-1, keepdims=True)
    o_ref[...] = x_ref[...] + y * lax.rsqrt(ms + RMS_EPS) * g_ref[...]


def _out_proj(x, yc, yd, w_out_b, post_g):
    B, S, D = x.shape
    ts = OUT_TILE
    assert S % ts == 0
    row = lambda b, s: (b, s, 0)
    const2 = lambda b, s: (0, 0)
    return pl.pallas_call(
        _out_kernel,
        grid=(B, S // ts),
        in_specs=[
            pl.BlockSpec((None, ts, D), row),
            pl.BlockSpec((None, ts, CONV_WIDTH), row),
            pl.BlockSpec((None, ts, DIFF_WIDTH), row),
            pl.BlockSpec(w_out_b.shape, const2),
            pl.BlockSpec((1, D), const2),
        ],
        out_specs=pl.BlockSpec((None, ts, D), row),
        out_shape=jax.ShapeDtypeStruct((B, S, D), jnp.float32),
        compiler_params=pltpu.CompilerParams(
            dimension_semantics=("arbitrary", "arbitrary"),
            vmem_limit_bytes=VMEM_LIMIT),
        name="out_proj",
    )(x, yc, yd, w_out_b, post_g)


def kernel(x, pre_norm_g, w_in, conv_dw_w, conv_dw_b, conv_ln_g, conv_ln_b, conv_pw_w, conv_pw_b,
           lambda_q1, lambda_k1, lambda_q2, lambda_k2, diff_subln_g, w_out, post_norm_g):
    depth = pre_norm_g.shape[0]
    h = x
    for i in range(depth):
        w_in_b = w_in[i].astype(jnp.bfloat16)
        yc, qt, k, vt, gd = _proj_conv(
            h, pre_norm_g[i][None], w_in_b, conv_dw_w[i], conv_dw_b[i][None],
            conv_ln_g[i][None], conv_ln_b[i][None], conv_pw_w[i].astype(jnp.bfloat16), conv_pw_b[i][None])
        yd = _attention(qt, k, vt, gd, lambda_q1[i][None], lambda_k1[i][None], lambda_q2[i][None],
                        lambda_k2[i][None], diff_subln_g[i][None], _lambda_init(i))
        h = _out_proj(h, yc, yd, w_out[i].astype(jnp.bfloat16), post_norm_g[i][None])
    return h
```

```python
import functools
import math

import jax
import jax.numpy as jnp
from jax import lax
from jax.experimental import pallas as pl
from jax.experimental.pallas import tpu as pltpu

D_MODEL = 1024
CONV_WIDTH = 512
CONV_KERNEL = 31
DIFF_WIDTH = 512
DIFF_HEADS = 4
DIFF_VDIM = 128
DIFF_QKDIM = 64
RMS_EPS = 1e-6
LN_EPS = 1e-5


def _lambda_init(layer_idx):
    return 0.8 - 0.6 * math.exp(-0.3 * layer_idx)


LOG2E = math.log2(math.e)
Q_PRESCALE = (DIFF_QKDIM ** -0.5) * LOG2E

C_GLU = 0
C_CGATE = 1024
C_Q = 1536
C_K = 2048
C_V = 2560
C_DGATE = 3072

PROJ_TILE = 512
NORM_ROWS = 256
OUT_TILE = 1024
OUT_ROWS = 256
CONV_HALO = 32
CONV_ROWS = 256
LANES = 128
ATT_HEADS = 1
ATT_TQ = 512
ATT_TK = 512
ATT_ITEM = 1024
POS_SHIFT = 8
POS_RADIX = 1 << POS_SHIFT
SLOPE_PIECES = 3
VT_PAD = 16
VMEM_LIMIT = 48 * 1024 * 1024

_NT = (((1,), (1,)), ((), ()))


def _silu(x):
    return x * jax.nn.sigmoid(x)


def _aligned(start, multiple):
    return start if isinstance(start, int) else pl.multiple_of(start, multiple)


def _proj_conv_kernel(x_ref, g_ref, w_ref, dww_ref, dwb_ref, lng_ref, lnb_ref, pww_ref, pwb_ref,
                      yc_ref, qt_ref, k_ref, vt_ref, gd_ref, hbuf, cbuf, wqt_ref, wvt_ref):
    ts = x_ref.shape[0]
    s_idx = pl.program_id(1)

    @pl.when((pl.program_id(0) == 0) & (s_idx == 0))
    def _():
        wqt_ref[...] = w_ref[:, C_Q:C_Q + DIFF_WIDTH].T
        wvt_ref[...] = w_ref[:, C_V:C_V + DIFF_WIDTH].T

    n_slab = CONV_WIDTH // LANES

    @pl.when(s_idx == 0)
    def _():
        hbuf[:, 0:CONV_HALO, :] = jnp.zeros((n_slab, CONV_HALO, LANES), jnp.float32)

    xb_parts = []
    for r0 in range(0, ts, NORM_ROWS):
        x = x_ref[r0:r0 + NORM_ROWS, :]
        ms = jnp.mean(x * x, axis=-1, keepdims=True)
        xb_r = (x * lax.rsqrt(ms + RMS_EPS) * g_ref[...]).astype(jnp.bfloat16)
        xb_parts.append(xb_r)
        a = jnp.dot(xb_r, w_ref[:, C_GLU:C_GLU + CONV_WIDTH], preferred_element_type=jnp.float32)
        b = jnp.dot(xb_r, w_ref[:, C_GLU + CONV_WIDTH:C_GLU + 2 * CONV_WIDTH],
                    preferred_element_type=jnp.float32)
        h = a * jax.nn.sigmoid(b)
        for l in range(n_slab):
            hbuf[l, CONV_HALO + r0:CONV_HALO + r0 + NORM_ROWS, :] = h[:, l * LANES:(l + 1) * LANES]
    xb = jnp.concatenate(xb_parts, axis=0)

    def proj(c0, width):
        return jnp.dot(xb, w_ref[:, c0:c0 + width], preferred_element_type=jnp.float32)

    off = CONV_HALO - (CONV_KERNEL - 1)
    half = CONV_ROWS // 2

    def conv_block(l, r0):
        lanes = slice(l * LANES, (l + 1) * LANES)
        acc = [jnp.broadcast_to(dwb_ref[:, lanes], (half, LANES)) for _ in range(2)]
        for tau in range(CONV_KERNEL):
            w = dww_ref[tau:tau + 1, lanes]
            for par in range(2):
                acc[par] = acc[par] + w * hbuf[l, pl.ds(r0 + off + tau + par, half, stride=2), :]
        for par in range(2):
            cbuf[l, pl.ds(r0 + par, half, stride=2), :] = acc[par]

    qt_ref[...] = (lax.dot_general(wqt_ref[...], xb, _NT, preferred_element_type=jnp.float32)
                   * Q_PRESCALE).astype(jnp.bfloat16)
    k_ref[...] = proj(C_K, DIFF_WIDTH).astype(jnp.bfloat16)
    vt_ref[...] = lax.dot_general(wvt_ref[...], xb, _NT,
                                  preferred_element_type=jnp.float32).astype(jnp.bfloat16)
    gd_ref[...] = _silu(proj(C_DGATE, DIFF_WIDTH))
    gate_c = _silu(proj(C_CGATE, CONV_WIDTH))

    for r0 in range(0, ts, CONV_ROWS):
        for l in range(n_slab):
            conv_block(l, r0)

    hbuf[:, 0:CONV_HALO, :] = hbuf[:, ts:ts + CONV_HALO, :]

    cv = jnp.concatenate([cbuf[l] for l in range(n_slab)], axis=-1)
    mu = jnp.mean(cv, axis=-1, keepdims=True)
    d = cv - mu
    var = jnp.mean(d * d, axis=-1, keepdims=True)
    hn = d * lax.rsqrt(var + LN_EPS) * lng_ref[...] + lnb_ref[...]
    hs = _silu(hn).astype(jnp.bfloat16)
    yc = jnp.dot(hs, pww_ref[...], preferred_element_type=jnp.float32) + pwb_ref[...]
    yc_ref[...] = (yc * gate_c).astype(jnp.bfloat16)


def _proj_conv(x, pre_g, w_in_b, dw_w, dw_b, ln_g, ln_b, pw_w_b, pw_b):
    B, S, D = x.shape
    ts = PROJ_TILE
    assert S % ts == 0 and ts % CONV_ROWS == 0
    grid = (B, S // ts)
    row = lambda b, s: (b, s, 0)
    const2 = lambda b, s: (0, 0)
    in_specs = [
        pl.BlockSpec((None, ts, D), row),
        pl.BlockSpec((1, D), const2),
        pl.BlockSpec(w_in_b.shape, const2),
        pl.BlockSpec(dw_w.shape, const2),
        pl.BlockSpec((1, CONV_WIDTH), const2),
        pl.BlockSpec((1, CONV_WIDTH), const2),
        pl.BlockSpec((1, CONV_WIDTH), const2),
        pl.BlockSpec(pw_w_b.shape, const2),
        pl.BlockSpec((1, CONV_WIDTH), const2),
    ]
    out_specs = [
        pl.BlockSpec((None, ts, CONV_WIDTH), row),
        pl.BlockSpec((None, DIFF_WIDTH, ts), lambda b, s: (b, 0, s)),
        pl.BlockSpec((None, ts, DIFF_WIDTH), row),
        pl.BlockSpec((None, DIFF_WIDTH, ts), lambda b, s: (b, 0, s)),
        pl.BlockSpec((None, ts, DIFF_WIDTH), row),
    ]
    out_shape = [
        jax.ShapeDtypeStruct((B, S, CONV_WIDTH), jnp.bfloat16),
        jax.ShapeDtypeStruct((B, DIFF_WIDTH, S), jnp.bfloat16),
        jax.ShapeDtypeStruct((B, S, DIFF_WIDTH), jnp.bfloat16),
        jax.ShapeDtypeStruct((B, DIFF_WIDTH, S), jnp.bfloat16),
        jax.ShapeDtypeStruct((B, S, DIFF_WIDTH), jnp.float32),
    ]
    return pl.pallas_call(
        _proj_conv_kernel,
        grid=grid,
        in_specs=in_specs,
        out_specs=out_specs,
        out_shape=out_shape,
        scratch_shapes=[
            pltpu.VMEM((CONV_WIDTH // LANES, CONV_HALO + ts, LANES), jnp.float32),
            pltpu.VMEM((CONV_WIDTH // LANES, ts, LANES), jnp.float32),
            pltpu.VMEM((DIFF_WIDTH, D), jnp.bfloat16),
            pltpu.VMEM((DIFF_WIDTH, D), jnp.bfloat16),
        ],
        compiler_params=pltpu.CompilerParams(
            dimension_semantics=("arbitrary", "arbitrary"),
            vmem_limit_bytes=VMEM_LIMIT),
        name="proj_conv",
    )(x, pre_g, w_in_b, dw_w, dw_b, ln_g, ln_b, pw_w_b, pw_b)


def _attn_kernel(qt_ref, k_ref, vt_ref, gd_ref, lq1_ref, lk1_ref, lq2_ref, lk2_ref, sg_ref,
                 o_ref, acc_ref, m_ref, mask_ref, t_ref, tmax_ref, kaug_ref, vta_ref, qaug_ref,
                 *, lam_init):
    tq = ATT_TQ
    tk = ATT_TK
    seq = k_ref.shape[0]
    n_q = seq // tq
    group = pl.program_id(1)
    heads = range(ATT_HEADS)
    f32, bf16 = jnp.float32, jnp.bfloat16

    def head_rows(e):
        return slice(e * DIFF_VDIM, (e + 1) * DIFF_VDIM)

    slopes = []
    for e in heads:
        hv = jnp.full((1, 1), group * ATT_HEADS + e, jnp.int32).astype(f32)
        slopes.append(jnp.exp2(-(8.0 / DIFF_HEADS) * (hv + 1.0)) * LOG2E)

    @pl.when((pl.program_id(0) == 0) & (group == 0))
    def _():
        jj = lax.broadcasted_iota(jnp.int32, (ATT_ITEM, LANES), 0)
        ln = lax.broadcasted_iota(jnp.int32, (ATT_ITEM, LANES), 1)
        feat = jnp.where(ln < SLOPE_PIECES, jj & (POS_RADIX - 1),
                         jnp.where(ln < 2 * SLOPE_PIECES, jj >> POS_SHIFT, 0))
        feat = feat.astype(f32).astype(bf16)
        row = lax.broadcasted_iota(jnp.int32, (VT_PAD, seq), 0)
        ones_row = jnp.where(row == 0, 1.0, 0.0).astype(bf16)
        for e in heads:
            for blk in range(seq // ATT_ITEM):
                kaug_ref[e, blk * ATT_ITEM:(blk + 1) * ATT_ITEM, DIFF_VDIM:] = feat
            vta_ref[e, DIFF_VDIM:, :] = ones_row
        kk = lax.broadcasted_iota(jnp.int32, (tk, tq), 0)
        ii = lax.broadcasted_iota(jnp.int32, (tk, tq), 1)
        mask_ref[...] = jnp.where(kk > ii, -jnp.inf, 0.0).astype(f32)

    for e in heads:
        kaug_ref[e, :, 0:DIFF_VDIM] = k_ref[:, head_rows(e)]
        vta_ref[e, 0:DIFF_VDIM, :] = vt_ref[head_rows(e), :]

    r = lax.broadcasted_iota(jnp.int32, (VT_PAD, tq), 0)
    zhalf = jnp.zeros((DIFF_QKDIM, tq), bf16)
    saug = []
    for e in heads:
        sl = jnp.broadcast_to(slopes[e], (1, tq))
        s_hi = sl.astype(bf16).astype(f32)
        s_mid = (sl - s_hi).astype(bf16).astype(f32)
        s_lo = (sl - s_hi - s_mid).astype(bf16).astype(f32)
        piece = jnp.where((r == 0) | (r == 3), s_hi, jnp.where((r == 1) | (r == 4), s_mid, s_lo))
        srows = jnp.where(r < SLOPE_PIECES, piece,
                          jnp.where(r < 2 * SLOPE_PIECES, piece * POS_RADIX, 0.0)).astype(bf16)
        saug.append(jnp.concatenate([srows, jnp.zeros((LANES - VT_PAD, tq), bf16)], axis=0))

    def query_operands(e, q0):
        qt = qt_ref[head_rows(e), pl.ds(q0, tq)]
        return (jnp.concatenate([qt[0:DIFF_QKDIM], zhalf, saug[e]], axis=0),
                jnp.concatenate([zhalf, qt[DIFF_QKDIM:], saug[e]], axis=0))

    for e in heads:
        own = query_operands(e, 0)
        qaug_ref[e, 0] = own[0]
        qaug_ref[e, 1] = own[1]
    qpos = lax.broadcasted_iota(jnp.int32, (1, tq), 1).astype(f32)
    lam = (jnp.exp(jnp.sum(lq1_ref[...] * lk1_ref[...], axis=-1, keepdims=True))
           - jnp.exp(jnp.sum(lq2_ref[...] * lk2_ref[...], axis=-1, keepdims=True)) + lam_init)

    def tile(qi, carry):
        _tile(qi)
        return carry

    def _tile(qi, has_next=True):
        qaug = [(qaug_ref[e, 0], qaug_ref[e, 1]) for e in heads]
        if has_next:
            qaug_next = [query_operands(e, _aligned((qi + 1) * tq, tq)) for e in heads]

        m_ref[...] = jnp.full(m_ref.shape, -jnp.inf, f32)

        def scores(c, u0, n_units, mask_last, next_tile=False):
            if next_tile and not has_next:
                return
            for e in heads:
                qop = qaug_next[e][c] if next_tile else qaug[e][c]
                mx = None
                for u in range(n_units):
                    k0 = _aligned((u0 + u) * tk, tk)
                    t = jnp.dot(kaug_ref[e, pl.ds(k0, tk), :], qop, preferred_element_type=f32)
                    if mask_last and u == n_units - 1:
                        t = t + mask_ref[...]
                    t_ref[e, c, u * tk:(u + 1) * tk, :] = t
                    tm = jnp.max(t, axis=0, keepdims=True)
                    mx = tm if mx is None else jnp.maximum(mx, tm)
                tmax_ref[e, c] = mx

        def softmax_pv(par, c, u0, n_units):
            k0 = _aligned(u0 * tk, tk)
            nk = n_units * tk
            for e in heads:
                cq = slopes[e] * (jnp.asarray(u0 * tk - qi * tq, jnp.int32).astype(f32) - qpos)
                m_old = m_ref[e, c]
                m_new = jnp.maximum(m_old, tmax_ref[e, c] + cq)
                p = jnp.exp2(t_ref[e, c, 0:nk, :] - (m_new - cq))
                alpha = jnp.exp2(m_old - m_new)
                acc_ref[e, par, c] = alpha * acc_ref[e, par, c] + jnp.dot(
                    vta_ref[e, :, pl.ds(k0, nk)], p.astype(bf16), preferred_element_type=f32)
                m_ref[e, c] = m_new

        units = ATT_ITEM // tk
        nb = qi // units
        tail0 = nb * units

        def run(n_tail):
            par = (n_tail - 1) % 2

            def clear_sums():
                for e in heads:
                    acc_ref[e, par] = jnp.zeros(acc_ref.shape[2:], f32)

            @pl.when((qi % units == n_tail - 1) & (nb == 0))
            def _():
                first_tile = n_tail == 1
                clear_sums()
                if first_tile:
                    scores(0, 0, n_tail, True)
                scores(1, 0, n_tail, True)
                if not first_tile:
                    finish(qi - 1, 1 - par)
                softmax_pv(par, 0, 0, n_tail)
                if n_tail < units:
                    scores(0, 0, n_tail + 1, True, next_tile=True)
                else:
                    scores(0, 0, units, False, next_tile=True)
                softmax_pv(par, 1, 0, n_tail)

            @pl.when((qi % units == n_tail - 1) & (nb > 0))
            def _():
                clear_sums()

                def body(j, carry):
                    scores(1, j * units, units, False)
                    softmax_pv(par, 0, j * units, units)
                    scores(0, (j + 1) * units, units, False)
                    softmax_pv(par, 1, j * units, units)
                    return carry

                lax.fori_loop(0, nb - 1, body, 0)
                last = tail0 - units
                scores(1, last, units, False)
                finish(qi - 1, 1 - par)
                softmax_pv(par, 0, last, units)
                scores(0, tail0, n_tail, True)
                softmax_pv(par, 1, last, units)
                scores(1, tail0, n_tail, True)
                softmax_pv(par, 0, tail0, n_tail)
                scores(0, 0, units, False, next_tile=True)
                softmax_pv(par, 1, tail0, n_tail)

        for n_tail in range(1, units + 1):
            run(n_tail)

        if has_next:
            for e in heads:
                qaug_ref[e, 0] = qaug_next[e][0]
                qaug_ref[e, 1] = qaug_next[e][1]

    def finish(q_tile, par):
        q0 = _aligned(q_tile * tq, tq)
        for e in heads:
            a0 = acc_ref[e, par, 0]
            a1 = acc_ref[e, par, 1]
            o = (a0[0:DIFF_VDIM] / a0[DIFF_VDIM:DIFF_VDIM + 1]
                 - lam * (a1[0:DIFF_VDIM] / a1[DIFF_VDIM:DIFF_VDIM + 1]))
            ms = jnp.mean(o * o, axis=0, keepdims=True)
            on = (o * lax.rsqrt(ms + RMS_EPS)).T
            y = on * (sg_ref[...] * (1.0 - lam_init)) * gd_ref[pl.ds(q0, tq), head_rows(e)]
            o_ref[pl.ds(q0, tq), head_rows(e)] = y.astype(bf16)

    lax.fori_loop(0, n_q - 1, tile, 0)
    _tile(n_q - 1, has_next=False)
    finish(n_q - 1, (n_q - 1) % 2)


def _attention(qt, k, vt, gd, lq1, lk1, lq2, lk2, sg, lam_init):
    B, S, _ = k.shape
    assert ATT_TQ == ATT_TK and ATT_ITEM % ATT_TK == 0 and S % ATT_ITEM == 0
    assert ATT_ITEM <= POS_RADIX * POS_RADIX
    assert DIFF_HEADS % ATT_HEADS == 0
    tq = ATT_TQ
    width = ATT_HEADS * DIFF_VDIM
    grid = (B, DIFF_HEADS // ATT_HEADS)
    vec = lambda b, g: (0, 0)
    rows = lambda b, g: (b, 0, g)
    cols = lambda b, g: (b, g, 0)
    in_specs = [
        pl.BlockSpec((None, width, S), cols),
        pl.BlockSpec((None, S, width), rows),
        pl.BlockSpec((None, width, S), cols),
        pl.BlockSpec((None, S, width), rows),
        pl.BlockSpec((1, DIFF_QKDIM), vec),
        pl.BlockSpec((1, DIFF_QKDIM), vec),
        pl.BlockSpec((1, DIFF_QKDIM), vec),
        pl.BlockSpec((1, DIFF_QKDIM), vec),
        pl.BlockSpec((1, DIFF_VDIM), vec),
    ]
    n_h = ATT_HEADS
    return pl.pallas_call(
        functools.partial(_attn_kernel, lam_init=lam_init),
        grid=grid,
        in_specs=in_specs,
        out_specs=pl.BlockSpec((None, S, width), rows),
        out_shape=jax.ShapeDtypeStruct((B, S, DIFF_WIDTH), jnp.bfloat16),
        scratch_shapes=[
            pltpu.VMEM((n_h, 2, 2, DIFF_VDIM + VT_PAD, tq), jnp.float32),
            pltpu.VMEM((n_h, 2, 1, tq), jnp.float32),
            pltpu.VMEM((ATT_TK, tq), jnp.float32),
            pltpu.VMEM((n_h, 2, ATT_ITEM, tq), jnp.float32),
            pltpu.VMEM((n_h, 2, 1, tq), jnp.float32),
            pltpu.VMEM((n_h, S, 2 * LANES), jnp.bfloat16),
            pltpu.VMEM((n_h, DIFF_VDIM + VT_PAD, S), jnp.bfloat16),
            pltpu.VMEM((n_h, 2, 2 * LANES, tq), jnp.bfloat16),
        ],
        compiler_params=pltpu.CompilerParams(
            dimension_semantics=("arbitrary", "arbitrary"),
            vmem_limit_bytes=VMEM_LIMIT),
        name="diff_attn",
    )(qt, k, vt, gd, lq1, lk1, lq2, lk2, sg)


def _out_kernel(x_ref, yc_ref, yd_ref, w_ref, g_ref, o_ref):
    def project(r0):
        rows = slice(r0, r0 + OUT_ROWS)
        return (jnp.dot(yc_ref[rows, :], w_ref[0:CONV_WIDTH, :], preferred_element_type=jnp.float32)
                + jnp.dot(yd_ref[rows, :], w_ref[CONV_WIDTH:, :], preferred_element_type=jnp.float32))

    def finish(r0, y):
        rows = slice(r0, r0 + OUT_ROWS)
        ms = jnp.mean(y * y, axis=-1, keepdims=True)
        o_ref[rows, :] = x_ref[rows, :] + y * lax.rsqrt(ms + RMS_EPS) * g_ref[...]

    starts = list(range(0, x_ref.shape[0], OUT_ROWS))
    pending = None
    for r0 in starts:
        y = project(r0)
        if pending is not None:
            finish(*pending)
        pending = (r0, y)
    finish(*pending)


def _out_proj(x, yc, yd, w_out_b, post_g):
    B, S, D = x.shape
    ts = OUT_TILE
    assert S % ts == 0
    row = lambda b, s: (b, s, 0)
    const2 = lambda b, s: (0, 0)
    return pl.pallas_call(
        _out_kernel,
        grid=(B, S // ts),
        in_specs=[
            pl.BlockSpec((None, ts, D), row),
            pl.BlockSpec((None, ts, CONV_WIDTH), row),
            pl.BlockSpec((None, ts, DIFF_WIDTH), row),
            pl.BlockSpec(w_out_b.shape, const2),
            pl.BlockSpec((1, D), const2),
        ],
        out_specs=pl.BlockSpec((None, ts, D), row),
        out_shape=jax.ShapeDtypeStruct((B, S, D), jnp.float32),
        compiler_params=pltpu.CompilerParams(
            dimension_semantics=("arbitrary", "arbitrary"),
            vmem_limit_bytes=VMEM_LIMIT),
        name="out_proj",
    )(x, yc, yd, w_out_b, post_g)


def kernel(x, pre_norm_g, w_in, conv_dw_w, conv_dw_b, conv_ln_g, conv_ln_b, conv_pw_w, conv_pw_b,
           lambda_q1, lambda_k1, lambda_q2, lambda_k2, diff_subln_g, w_out, post_norm_g):
    depth = pre_norm_g.shape[0]
    h = x
    for i in range(depth):
        w_in_b = w_in[i].astype(jnp.bfloat16)
        yc, qt, k, vt, gd = _proj_conv(
            h, pre_norm_g[i][None], w_in_b, conv_dw_w[i], conv_dw_b[i][None],
            conv_ln_g[i][None], conv_ln_b[i][None], conv_pw_w[i].astype(jnp.bfloat16), conv_pw_b[i][None])
        yd = _attention(qt, k, vt, gd, lambda_q1[i][None], lambda_k1[i][None], lambda_q2[i][None],
                        lambda_k2[i][None], diff_subln_g[i][None], _lambda_init(i))
        h = _out_proj(h, yc, yd, w_out[i].astype(jnp.bfloat16), post_norm_g[i][None])
    return h
```

```python
import functools
import math

import jax
import jax.numpy as jnp
from jax import lax
from jax.experimental import pallas as pl
from jax.experimental.pallas import tpu as pltpu

D_MODEL = 1024
CONV_WIDTH = 512
CONV_KERNEL = 31
DIFF_WIDTH = 512
DIFF_HEADS = 4
DIFF_VDIM = 128
DIFF_QKDIM = 64
RMS_EPS = 1e-6
LN_EPS = 1e-5


def _lambda_init(layer_idx):
    return 0.8 - 0.6 * math.exp(-0.3 * layer_idx)


LOG2E = math.log2(math.e)
Q_PRESCALE = (DIFF_QKDIM ** -0.5) * LOG2E

C_GLU = 0
C_CGATE = 1024
C_Q = 1536
C_K = 2048
C_V = 2560
C_DGATE = 3072

PROJ_TILE = 512
OUT_TILE = 1024
CONV_HALO = 32
CONV_ROWS = 256
LANES = 128
ATT_HEADS = 1
ATT_TQ = 512
ATT_TK = 512
ATT_ITEM = 1024
POS_SHIFT = 8
POS_RADIX = 1 << POS_SHIFT
SLOPE_PIECES = 3
VT_PAD = 16
VMEM_LIMIT = 48 * 1024 * 1024

_NT = (((1,), (1,)), ((), ()))


def _silu(x):
    return x * jax.nn.sigmoid(x)


def _aligned(start, multiple):
    return start if isinstance(start, int) else pl.multiple_of(start, multiple)


def _proj_conv_kernel(x_ref, g_ref, w_ref, dww_ref, dwb_ref, lng_ref, lnb_ref, pww_ref, pwb_ref,
                      yc_ref, qt_ref, k_ref, vt_ref, gd_ref, hbuf, cbuf, wqt_ref, wvt_ref):
    ts = x_ref.shape[0]
    s_idx = pl.program_id(1)

    @pl.when((pl.program_id(0) == 0) & (s_idx == 0))
    def _():
        wqt_ref[...] = w_ref[:, C_Q:C_Q + DIFF_WIDTH].T
        wvt_ref[...] = w_ref[:, C_V:C_V + DIFF_WIDTH].T

    x = x_ref[...]
    ms = jnp.mean(x * x, axis=-1, keepdims=True)
    xb = (x * lax.rsqrt(ms + RMS_EPS) * g_ref[...]).astype(jnp.bfloat16)

    def proj(c0, width):
        return jnp.dot(xb, w_ref[:, c0:c0 + width], preferred_element_type=jnp.float32)

    n_slab = CONV_WIDTH // LANES

    @pl.when(s_idx == 0)
    def _():
        hbuf[:, 0:CONV_HALO, :] = jnp.zeros((n_slab, CONV_HALO, LANES), jnp.float32)

    a = proj(C_GLU, CONV_WIDTH)
    b = proj(C_GLU + CONV_WIDTH, CONV_WIDTH)
    h = a * jax.nn.sigmoid(b)
    for l in range(n_slab):
        hbuf[l, CONV_HALO:CONV_HALO + ts, :] = h[:, l * LANES:(l + 1) * LANES]

    off = CONV_HALO - (CONV_KERNEL - 1)
    half = CONV_ROWS // 2

    def conv_block(l, r0):
        lanes = slice(l * LANES, (l + 1) * LANES)
        acc = [jnp.broadcast_to(dwb_ref[:, lanes], (half, LANES)) for _ in range(2)]
        for tau in range(CONV_KERNEL):
            w = dww_ref[tau:tau + 1, lanes]
            for par in range(2):
                acc[par] = acc[par] + w * hbuf[l, pl.ds(r0 + off + tau + par, half, stride=2), :]
        for par in range(2):
            cbuf[l, pl.ds(r0 + par, half, stride=2), :] = acc[par]

    qt_ref[...] = (lax.dot_general(wqt_ref[...], xb, _NT, preferred_element_type=jnp.float32)
                   * Q_PRESCALE).astype(jnp.bfloat16)
    k_ref[...] = proj(C_K, DIFF_WIDTH).astype(jnp.bfloat16)
    vt_ref[...] = lax.dot_general(wvt_ref[...], xb, _NT,
                                  preferred_element_type=jnp.float32).astype(jnp.bfloat16)
    gd_ref[...] = _silu(proj(C_DGATE, DIFF_WIDTH))
    gate_c = _silu(proj(C_CGATE, CONV_WIDTH))

    for r0 in range(0, ts, CONV_ROWS):
        for l in range(n_slab):
            conv_block(l, r0)

    hbuf[:, 0:CONV_HALO, :] = hbuf[:, ts:ts + CONV_HALO, :]

    cv = jnp.concatenate([cbuf[l] for l in range(n_slab)], axis=-1)
    mu = jnp.mean(cv, axis=-1, keepdims=True)
    d = cv - mu
    var = jnp.mean(d * d, axis=-1, keepdims=True)
    hn = d * lax.rsqrt(var + LN_EPS) * lng_ref[...] + lnb_ref[...]
    hs = _silu(hn).astype(jnp.bfloat16)
    yc = jnp.dot(hs, pww_ref[...], preferred_element_type=jnp.float32) + pwb_ref[...]
    yc_ref[...] = (yc * gate_c).astype(jnp.bfloat16)


def _proj_conv(x, pre_g, w_in_b, dw_w, dw_b, ln_g, ln_b, pw_w_b, pw_b):
    B, S, D = x.shape
    ts = PROJ_TILE
    assert S % ts == 0 and ts % CONV_ROWS == 0
    grid = (B, S // ts)
    row = lambda b, s: (b, s, 0)
    const2 = lambda b, s: (0, 0)
    in_specs = [
        pl.BlockSpec((None, ts, D), row),
        pl.BlockSpec((1, D), const2),
        pl.BlockSpec(w_in_b.shape, const2),
        pl.BlockSpec(dw_w.shape, const2),
        pl.BlockSpec((1, CONV_WIDTH), const2),
        pl.BlockSpec((1, CONV_WIDTH), const2),
        pl.BlockSpec((1, CONV_WIDTH), const2),
        pl.BlockSpec(pw_w_b.shape, const2),
        pl.BlockSpec((1, CONV_WIDTH), const2),
    ]
    out_specs = [
        pl.BlockSpec((None, ts, CONV_WIDTH), row),
        pl.BlockSpec((None, DIFF_WIDTH, ts), lambda b, s: (b, 0, s)),
        pl.BlockSpec((None, ts, DIFF_WIDTH), row),
        pl.BlockSpec((None, DIFF_WIDTH, ts), lambda b, s: (b, 0, s)),
        pl.BlockSpec((None, ts, DIFF_WIDTH), row),
    ]
    out_shape = [
        jax.ShapeDtypeStruct((B, S, CONV_WIDTH), jnp.bfloat16),
        jax.ShapeDtypeStruct((B, DIFF_WIDTH, S), jnp.bfloat16),
        jax.ShapeDtypeStruct((B, S, DIFF_WIDTH), jnp.bfloat16),
        jax.ShapeDtypeStruct((B, DIFF_WIDTH, S), jnp.bfloat16),
        jax.ShapeDtypeStruct((B, S, DIFF_WIDTH), jnp.float32),
    ]
    return pl.pallas_call(
        _proj_conv_kernel,
        grid=grid,
        in_specs=in_specs,
        out_specs=out_specs,
        out_shape=out_shape,
        scratch_shapes=[
            pltpu.VMEM((CONV_WIDTH // LANES, CONV_HALO + ts, LANES), jnp.float32),
            pltpu.VMEM((CONV_WIDTH // LANES, ts, LANES), jnp.float32),
            pltpu.VMEM((DIFF_WIDTH, D), jnp.bfloat16),
            pltpu.VMEM((DIFF_WIDTH, D), jnp.bfloat16),
        ],
        compiler_params=pltpu.CompilerParams(
            dimension_semantics=("arbitrary", "arbitrary"),
            vmem_limit_bytes=VMEM_LIMIT),
        name="proj_conv",
    )(x, pre_g, w_in_b, dw_w, dw_b, ln_g, ln_b, pw_w_b, pw_b)


def _attn_kernel(qt_ref, k_ref, vt_ref, gd_ref, lq1_ref, lk1_ref, lq2_ref, lk2_ref, sg_ref,
                 o_ref, acc_ref, m_ref, mask_ref, t_ref, tmax_ref, kaug_ref, vta_ref, qaug_ref,
                 *, lam_init):
    tq = ATT_TQ
    tk = ATT_TK
    hq, hk = tq // 2, tk // 2
    seq = k_ref.shape[0]
    n_q = seq // tq
    group = pl.program_id(1)
    heads = range(ATT_HEADS)
    f32, bf16 = jnp.float32, jnp.bfloat16

    def head_rows(e):
        return slice(e * DIFF_VDIM, (e + 1) * DIFF_VDIM)

    slopes = []
    for e in heads:
        hv = jnp.full((1, 1), group * ATT_HEADS + e, jnp.int32).astype(f32)
        slopes.append(jnp.exp2(-(8.0 / DIFF_HEADS) * (hv + 1.0)) * LOG2E)

    @pl.when((pl.program_id(0) == 0) & (group == 0))
    def _():
        jj = lax.broadcasted_iota(jnp.int32, (ATT_ITEM, LANES), 0)
        ln = lax.broadcasted_iota(jnp.int32, (ATT_ITEM, LANES), 1)
        feat = jnp.where(ln < SLOPE_PIECES, jj & (POS_RADIX - 1),
                         jnp.where(ln < 2 * SLOPE_PIECES, jj >> POS_SHIFT, 0))
        feat = feat.astype(f32).astype(bf16)
        row = lax.broadcasted_iota(jnp.int32, (VT_PAD, seq), 0)
        ones_row = jnp.where(row == 0, 1.0, 0.0).astype(bf16)
        for e in heads:
            for blk in range(seq // ATT_ITEM):
                kaug_ref[e, blk * ATT_ITEM:(blk + 1) * ATT_ITEM, DIFF_VDIM:] = feat
            vta_ref[e, DIFF_VDIM:, :] = ones_row
        kk = lax.broadcasted_iota(jnp.int32, (tk, tq), 0)
        ii = lax.broadcasted_iota(jnp.int32, (tk, tq), 1)
        mask_ref[...] = jnp.where(kk > ii, -jnp.inf, 0.0).astype(f32)

    for e in heads:
        kaug_ref[e, :, 0:DIFF_VDIM] = k_ref[:, head_rows(e)]
        vta_ref[e, 0:DIFF_VDIM, :] = vt_ref[head_rows(e), :]

    r = lax.broadcasted_iota(jnp.int32, (VT_PAD, tq), 0)
    zhalf = jnp.zeros((DIFF_QKDIM, tq), bf16)
    saug = []
    for e in heads:
        sl = jnp.broadcast_to(slopes[e], (1, tq))
        s_hi = sl.astype(bf16).astype(f32)
        s_mid = (sl - s_hi).astype(bf16).astype(f32)
        s_lo = (sl - s_hi - s_mid).astype(bf16).astype(f32)
        piece = jnp.where((r == 0) | (r == 3), s_hi, jnp.where((r == 1) | (r == 4), s_mid, s_lo))
        srows = jnp.where(r < SLOPE_PIECES, piece,
                          jnp.where(r < 2 * SLOPE_PIECES, piece * POS_RADIX, 0.0)).astype(bf16)
        saug.append(jnp.concatenate([srows, jnp.zeros((LANES - VT_PAD, tq), bf16)], axis=0))

    def query_operands(e, q0):
        qt = qt_ref[head_rows(e), pl.ds(q0, tq)]
        return (jnp.concatenate([qt[0:DIFF_QKDIM], zhalf, saug[e]], axis=0),
                jnp.concatenate([zhalf, qt[DIFF_QKDIM:], saug[e]], axis=0))

    for e in heads:
        own = query_operands(e, 0)
        qaug_ref[e, 0] = own[0]
        qaug_ref[e, 1] = own[1]
    qpos = lax.broadcasted_iota(jnp.int32, (1, tq), 1).astype(f32)
    lam = (jnp.exp(jnp.sum(lq1_ref[...] * lk1_ref[...], axis=-1, keepdims=True))
           - jnp.exp(jnp.sum(lq2_ref[...] * lk2_ref[...], axis=-1, keepdims=True)) + lam_init)

    def tile(qi, carry):
        _tile(qi)
        return carry

    def _tile(qi, has_next=True):
        qaug = [(qaug_ref[e, 0], qaug_ref[e, 1]) for e in heads]
        if has_next:
            qaug_next = [query_operands(e, _aligned((qi + 1) * tq, tq)) for e in heads]

        m_ref[...] = jnp.full(m_ref.shape, -jnp.inf, f32)

        def scores(c, u0, n_units, mask_last, next_tile=False):
            if next_tile and not has_next:
                return
            for e in heads:
                qop = qaug_next[e][c] if next_tile else qaug[e][c]
                mx = None
                for u in range(n_units):
                    k0 = _aligned((u0 + u) * tk, tk)
                    if mask_last and u == n_units - 1:
                        k1 = _aligned((u0 + u) * tk + hk, hk)
                        ta = jnp.dot(kaug_ref[e, pl.ds(k0, hk), :], qop,
                                     preferred_element_type=f32) + mask_ref[0:hk, :]
                        tb = jnp.dot(kaug_ref[e, pl.ds(k1, hk), :], qop[:, hq:],
                                     preferred_element_type=f32) + mask_ref[hk:, hq:]
                        t_ref[e, c, u * tk:u * tk + hk, :] = ta
                        t_ref[e, c, u * tk + hk:(u + 1) * tk, hq:] = tb
                        tm = jnp.maximum(
                            jnp.max(ta, axis=0, keepdims=True),
                            jnp.concatenate([jnp.full((1, hq), -jnp.inf, f32),
                                             jnp.max(tb, axis=0, keepdims=True)], axis=1))
                    else:
                        t = jnp.dot(kaug_ref[e, pl.ds(k0, tk), :], qop, preferred_element_type=f32)
                        t_ref[e, c, u * tk:(u + 1) * tk, :] = t
                        tm = jnp.max(t, axis=0, keepdims=True)
                    mx = tm if mx is None else jnp.maximum(mx, tm)
                tmax_ref[e, c] = mx

        def softmax_pv(par, c, u0, n_units, masked=False):
            k0 = _aligned(u0 * tk, tk)
            nk = n_units * tk
            for e in heads:
                cq = slopes[e] * (jnp.asarray(u0 * tk - qi * tq, jnp.int32).astype(f32) - qpos)
                m_old = m_ref[e, c]
                m_new = jnp.maximum(m_old, tmax_ref[e, c] + cq)
                alpha = jnp.exp2(m_old - m_new)
                shift = m_new - cq
                if masked:
                    nf = nk - hk
                    k1 = _aligned(u0 * tk + nf, hk)
                    p = jnp.exp2(t_ref[e, c, 0:nf, :] - shift)
                    pb = jnp.exp2(t_ref[e, c, nf:nk, hq:] - shift[:, hq:])
                    upd = alpha * acc_ref[e, par, c] + jnp.dot(
                        vta_ref[e, :, pl.ds(k0, nf)], p.astype(bf16), preferred_element_type=f32)
                    late = jnp.dot(vta_ref[e, :, pl.ds(k1, hk)], pb.astype(bf16), preferred_element_type=f32)
                    acc_ref[e, par, c] = jnp.concatenate([upd[:, 0:hq], upd[:, hq:] + late], axis=1)
                else:
                    p = jnp.exp2(t_ref[e, c, 0:nk, :] - shift)
                    acc_ref[e, par, c] = alpha * acc_ref[e, par, c] + jnp.dot(
                        vta_ref[e, :, pl.ds(k0, nk)], p.astype(bf16), preferred_element_type=f32)
                m_ref[e, c] = m_new

        units = ATT_ITEM // tk
        nb = qi // units
        tail0 = nb * units

        def run(n_tail):
            par = (n_tail - 1) % 2

            def clear_sums():
                for e in heads:
                    acc_ref[e, par] = jnp.zeros(acc_ref.shape[2:], f32)

            @pl.when((qi % units == n_tail - 1) & (nb == 0))
            def _():
                first_tile = n_tail == 1
                clear_sums()
                if first_tile:
                    scores(0, 0, n_tail, True)
                scores(1, 0, n_tail, True)
                if not first_tile:
                    finish(qi - 1, 1 - par)
                softmax_pv(par, 0, 0, n_tail, masked=True)
                if n_tail < units:
                    scores(0, 0, n_tail + 1, True, next_tile=True)
                else:
                    scores(0, 0, units, False, next_tile=True)
                softmax_pv(par, 1, 0, n_tail, masked=True)

            @pl.when((qi % units == n_tail - 1) & (nb > 0))
            def _():
                clear_sums()

                def body(j, carry):
                    scores(1, j * units, units, False)
                    softmax_pv(par, 0, j * units, units)
                    scores(0, (j + 1) * units, units, False)
                    softmax_pv(par, 1, j * units, units)
                    return carry

                lax.fori_loop(0, nb - 1, body, 0)
                last = tail0 - units
                scores(1, last, units, False)
                finish(qi - 1, 1 - par)
                softmax_pv(par, 0, last, units)
                scores(0, tail0, n_tail, True)
                softmax_pv(par, 1, last, units)
                scores(1, tail0, n_tail, True)
                softmax_pv(par, 0, tail0, n_tail, masked=True)
                scores(0, 0, units, False, next_tile=True)
                softmax_pv(par, 1, tail0, n_tail, masked=True)

        for n_tail in range(1, units + 1):
            run(n_tail)

        if has_next:
            for e in heads:
                qaug_ref[e, 0] = qaug_next[e][0]
                qaug_ref[e, 1] = qaug_next[e][1]

    def finish(q_tile, par):
        q0 = _aligned(q_tile * tq, tq)
        for e in heads:
            a0 = acc_ref[e, par, 0]
            a1 = acc_ref[e, par, 1]
            o = (a0[0:DIFF_VDIM] / a0[DIFF_VDIM:DIFF_VDIM + 1]
                 - lam * (a1[0:DIFF_VDIM] / a1[DIFF_VDIM:DIFF_VDIM + 1]))
            ms = jnp.mean(o * o, axis=0, keepdims=True)
            on = (o * lax.rsqrt(ms + RMS_EPS)).T
            y = on * (sg_ref[...] * (1.0 - lam_init)) * gd_ref[pl.ds(q0, tq), head_rows(e)]
            o_ref[pl.ds(q0, tq), head_rows(e)] = y.astype(bf16)

    lax.fori_loop(0, n_q - 1, tile, 0)
    _tile(n_q - 1, has_next=False)
    finish(n_q - 1, (n_q - 1) % 2)


def _attention(qt, k, vt, gd, lq1, lk1, lq2, lk2, sg, lam_init):
    B, S, _ = k.shape
    assert ATT_TQ == ATT_TK and ATT_ITEM % ATT_TK == 0 and S % ATT_ITEM == 0
    assert ATT_ITEM <= POS_RADIX * POS_RADIX
    assert DIFF_HEADS % ATT_HEADS == 0
    tq = ATT_TQ
    width = ATT_HEADS * DIFF_VDIM
    grid = (B, DIFF_HEADS // ATT_HEADS)
    vec = lambda b, g: (0, 0)
    rows = lambda b, g: (b, 0, g)
    cols = lambda b, g: (b, g, 0)
    in_specs = [
        pl.BlockSpec((None, width, S), cols),
        pl.BlockSpec((None, S, width), rows),
        pl.BlockSpec((None, width, S), cols),
        pl.BlockSpec((None, S, width), rows),
        pl.BlockSpec((1, DIFF_QKDIM), vec),
        pl.BlockSpec((1, DIFF_QKDIM), vec),
        pl.BlockSpec((1, DIFF_QKDIM), vec),
        pl.BlockSpec((1, DIFF_QKDIM), vec),
        pl.BlockSpec((1, DIFF_VDIM), vec),
    ]
    n_h = ATT_HEADS
    return pl.pallas_call(
        functools.partial(_attn_kernel, lam_init=lam_init),
        grid=grid,
        in_specs=in_specs,
        out_specs=pl.BlockSpec((None, S, width), rows),
        out_shape=jax.ShapeDtypeStruct((B, S, DIFF_WIDTH), jnp.bfloat16),
        scratch_shapes=[
            pltpu.VMEM((n_h, 2, 2, DIFF_VDIM + VT_PAD, tq), jnp.float32),
            pltpu.VMEM((n_h, 2, 1, tq), jnp.float32),
            pltpu.VMEM((ATT_TK, tq), jnp.float32),
            pltpu.VMEM((n_h, 2, ATT_ITEM, tq), jnp.float32),
            pltpu.VMEM((n_h, 2, 1, tq), jnp.float32),
            pltpu.VMEM((n_h, S, 2 * LANES), jnp.bfloat16),
            pltpu.VMEM((n_h, DIFF_VDIM + VT_PAD, S), jnp.bfloat16),
            pltpu.VMEM((n_h, 2, 2 * LANES, tq), jnp.bfloat16),
        ],
        compiler_params=pltpu.CompilerParams(
            dimension_semantics=("arbitrary", "arbitrary"),
            vmem_limit_bytes=VMEM_LIMIT),
        name="diff_attn",
    )(qt, k, vt, gd, lq1, lk1, lq2, lk2, sg)


def _out_kernel(x_ref, yc_ref, yd_ref, w_ref, g_ref, o_ref):
    y = (jnp.dot(yc_ref[...], w_ref[0:CONV_WIDTH, :], preferred_element_type=jnp.float32)
         + jnp.dot(yd_ref[...], w_ref[CONV_WIDTH:, :], preferred_element_type=jnp.float32))
    ms = jnp.mean(y * y, axis=-1, keepdims=True)
    o_ref[...] = x_ref[...] + y * lax.rsqrt(ms + RMS_EPS) * g_ref[...]


def _out_proj(x, yc, yd, w_out_b, post_g):
    B, S, D = x.shape
    ts = OUT_TILE
    assert S % ts == 0
    row = lambda b, s: (b, s, 0)
    const2 = lambda b, s: (0, 0)
    return pl.pallas_call(
        _out_kernel,
        grid=(B, S // ts),
        in_specs=[
            pl.BlockSpec((None, ts, D), row),
            pl.BlockSpec((None, ts, CONV_WIDTH), row),
            pl.BlockSpec((None, ts, DIFF_WIDTH), row),
            pl.BlockSpec(w_out_b.shape, const2),
            pl.BlockSpec((1, D), const2),
        ],
        out_specs=pl.BlockSpec((None, ts, D), row),
        out_shape=jax.ShapeDtypeStruct((B, S, D), jnp.float32),
        compiler_params=pltpu.CompilerParams(
            dimension_semantics=("arbitrary", "arbitrary"),
            vmem_limit_bytes=VMEM_LIMIT),
        name="out_proj",
    )(x, yc, yd, w_out_b, post_g)


def kernel(x, pre_norm_g, w_in, conv_dw_w, conv_dw_b, conv_ln_g, conv_ln_b, conv_pw_w, conv_pw_b,
           lambda_q1, lambda_k1, lambda_q2, lambda_k2, diff_subln_g, w_out, post_norm_g):
    depth = pre_norm_g.shape[0]
    h = x
    for i in range(depth):
        w_in_b = w_in[i].astype(jnp.bfloat16)
        yc, qt, k, vt, gd = _proj_conv(
            h, pre_norm_g[i][None], w_in_b, conv_dw_w[i], conv_dw_b[i][None],
            conv_ln_g[i][None], conv_ln_b[i][None], conv_pw_w[i].astype(jnp.bfloat16), conv_pw_b[i][None])
        yd = _attention(qt, k, vt, gd, lambda_q1[i][None], lambda_k1[i][None], lambda_q2[i][None],
                        lambda_k2[i][None], diff_subln_g[i][None], _lambda_init(i))
        h = _out_proj(h, yc, yd, w_out[i].astype(jnp.bfloat16), post_norm_g[i][None])
    return h
```

```python
import functools
import math

import jax
import jax.numpy as jnp
from jax import lax
from jax.experimental import pallas as pl
from jax.experimental.pallas import tpu as pltpu

D_MODEL = 1024
CONV_WIDTH = 512
CONV_KERNEL = 31
DIFF_WIDTH = 512
DIFF_HEADS = 4
DIFF_VDIM = 128
DIFF_QKDIM = 64
RMS_EPS = 1e-6
LN_EPS = 1e-5


def _lambda_init(layer_idx):
    return 0.8 - 0.6 * math.exp(-0.3 * layer_idx)


LOG2E = math.log2(math.e)
Q_PRESCALE = (DIFF_QKDIM ** -0.5) * LOG2E

C_GLU = 0
C_CGATE = 1024
C_Q = 1536
C_K = 2048
C_V = 2560
C_DGATE = 3072

PROJ_TILE = 1024
OUT_TILE = 1024
CONV_HALO = 32
CONV_ROWS = 256
LANES = 128
ATT_HEADS = 1
ATT_TQ = 512
ATT_TK = 512
ATT_ITEM = 1024
POS_SHIFT = 8
POS_RADIX = 1 << POS_SHIFT
SLOPE_PIECES = 3
VT_PAD = 16
VMEM_LIMIT = 48 * 1024 * 1024

_NT = (((1,), (1,)), ((), ()))


def _silu(x):
    return x * jax.nn.sigmoid(x)


def _aligned(start, multiple):
    return start if isinstance(start, int) else pl.multiple_of(start, multiple)


def _proj_conv_kernel(x_ref, g_ref, w_ref, dww_ref, dwb_ref, lng_ref, lnb_ref, pww_ref, pwb_ref,
                      yc_ref, qt_ref, k_ref, vt_ref, gd_ref, hbuf, cbuf, wqt_ref, wvt_ref):
    ts = x_ref.shape[0]
    s_idx = pl.program_id(1)

    @pl.when((pl.program_id(0) == 0) & (s_idx == 0))
    def _():
        wqt_ref[...] = w_ref[:, C_Q:C_Q + DIFF_WIDTH].T
        wvt_ref[...] = w_ref[:, C_V:C_V + DIFF_WIDTH].T

    x = x_ref[...]
    ms = jnp.mean(x * x, axis=-1, keepdims=True)
    xb = (x * lax.rsqrt(ms + RMS_EPS) * g_ref[...]).astype(jnp.bfloat16)

    def proj(c0, width):
        return jnp.dot(xb, w_ref[:, c0:c0 + width], preferred_element_type=jnp.float32)

    n_slab = CONV_WIDTH // LANES

    @pl.when(s_idx == 0)
    def _():
        hbuf[:, 0:CONV_HALO, :] = jnp.zeros((n_slab, CONV_HALO, LANES), jnp.float32)

    a = proj(C_GLU, CONV_WIDTH)
    b = proj(C_GLU + CONV_WIDTH, CONV_WIDTH)
    h = a * jax.nn.sigmoid(b)
    for l in range(n_slab):
        hbuf[l, CONV_HALO:CONV_HALO + ts, :] = h[:, l * LANES:(l + 1) * LANES]

    off = CONV_HALO - (CONV_KERNEL - 1)
    half = CONV_ROWS // 2

    def conv_block(l, r0):
        lanes = slice(l * LANES, (l + 1) * LANES)
        acc = [jnp.broadcast_to(dwb_ref[:, lanes], (half, LANES)) for _ in range(2)]
        for tau in range(CONV_KERNEL):
            w = dww_ref[tau:tau + 1, lanes]
            for par in range(2):
                acc[par] = acc[par] + w * hbuf[l, pl.ds(r0 + off + tau + par, half, stride=2), :]
        for par in range(2):
            cbuf[l, pl.ds(r0 + par, half, stride=2), :] = acc[par]

    qt_ref[...] = (lax.dot_general(wqt_ref[...], xb, _NT, preferred_element_type=jnp.float32)
                   * Q_PRESCALE).astype(jnp.bfloat16)
    k_ref[...] = proj(C_K, DIFF_WIDTH).astype(jnp.bfloat16)
    vt_ref[...] = lax.dot_general(wvt_ref[...], xb, _NT,
                                  preferred_element_type=jnp.float32).astype(jnp.bfloat16)
    gd_ref[...] = _silu(proj(C_DGATE, DIFF_WIDTH))
    gate_c = _silu(proj(C_CGATE, CONV_WIDTH))

    for r0 in range(0, ts, CONV_ROWS):
        for l in range(n_slab):
            conv_block(l, r0)

    hbuf[:, 0:CONV_HALO, :] = hbuf[:, ts:ts + CONV_HALO, :]

    cv = jnp.concatenate([cbuf[l] for l in range(n_slab)], axis=-1)
    mu = jnp.mean(cv, axis=-1, keepdims=True)
    d = cv - mu
    var = jnp.mean(d * d, axis=-1, keepdims=True)
    hn = d * lax.rsqrt(var + LN_EPS) * lng_ref[...] + lnb_ref[...]
    hs = _silu(hn).astype(jnp.bfloat16)
    yc = jnp.dot(hs, pww_ref[...], preferred_element_type=jnp.float32) + pwb_ref[...]
    yc_ref[...] = (yc * gate_c).astype(jnp.bfloat16)


def _proj_conv(x, pre_g, w_in_b, dw_w, dw_b, ln_g, ln_b, pw_w_b, pw_b):
    B, S, D = x.shape
    ts = PROJ_TILE
    assert S % ts == 0 and ts % CONV_ROWS == 0
    grid = (B, S // ts)
    row = lambda b, s: (b, s, 0)
    const2 = lambda b, s: (0, 0)
    in_specs = [
        pl.BlockSpec((None, ts, D), row),
        pl.BlockSpec((1, D), const2),
        pl.BlockSpec(w_in_b.shape, const2),
        pl.BlockSpec(dw_w.shape, const2),
        pl.BlockSpec((1, CONV_WIDTH), const2),
        pl.BlockSpec((1, CONV_WIDTH), const2),
        pl.BlockSpec((1, CONV_WIDTH), const2),
        pl.BlockSpec(pw_w_b.shape, const2),
        pl.BlockSpec((1, CONV_WIDTH), const2),
    ]
    out_specs = [
        pl.BlockSpec((None, ts, CONV_WIDTH), row),
        pl.BlockSpec((None, DIFF_WIDTH, ts), lambda b, s: (b, 0, s)),
        pl.BlockSpec((None, ts, DIFF_WIDTH), row),
        pl.BlockSpec((None, DIFF_WIDTH, ts), lambda b, s: (b, 0, s)),
        pl.BlockSpec((None, ts, DIFF_WIDTH), row),
    ]
    out_shape = [
        jax.ShapeDtypeStruct((B, S, CONV_WIDTH), jnp.bfloat16),
        jax.ShapeDtypeStruct((B, DIFF_WIDTH, S), jnp.bfloat16),
        jax.ShapeDtypeStruct((B, S, DIFF_WIDTH), jnp.bfloat16),
        jax.ShapeDtypeStruct((B, DIFF_WIDTH, S), jnp.bfloat16),
        jax.ShapeDtypeStruct((B, S, DIFF_WIDTH), jnp.float32),
    ]
    return pl.pallas_call(
        _proj_conv_kernel,
        grid=grid,
        in_specs=in_specs,
        out_specs=out_specs,
        out_shape=out_shape,
        scratch_shapes=[
            pltpu.VMEM((CONV_WIDTH // LANES, CONV_HALO + ts, LANES), jnp.float32),
            pltpu.VMEM((CONV_WIDTH // LANES, ts, LANES), jnp.float32),
            pltpu.VMEM((DIFF_WIDTH, D), jnp.bfloat16),
            pltpu.VMEM((DIFF_WIDTH, D), jnp.bfloat16),
        ],
        compiler_params=pltpu.CompilerParams(
            dimension_semantics=("arbitrary", "arbitrary"),
            vmem_limit_bytes=VMEM_LIMIT),
        name="proj_conv",
    )(x, pre_g, w_in_b, dw_w, dw_b, ln_g, ln_b, pw_w_b, pw_b)


def _attn_kernel(qt_ref, k_ref, vt_ref, gd_ref, lq1_ref, lk1_ref, lq2_ref, lk2_ref, sg_ref,
                 o_ref, acc_ref, m_ref, mask_ref, t_ref, tmax_ref, kaug_ref, vta_ref, qaug_ref,
                 *, lam_init):
    tq = ATT_TQ
    tk = ATT_TK
    hq, hk = tq // 2, tk // 2
    seq = k_ref.shape[0]
    n_q = seq // tq
    group = pl.program_id(1)
    heads = range(ATT_HEADS)
    f32, bf16 = jnp.float32, jnp.bfloat16

    def head_rows(e):
        return slice(e * DIFF_VDIM, (e + 1) * DIFF_VDIM)

    slopes = []
    for e in heads:
        hv = jnp.full((1, 1), group * ATT_HEADS + e, jnp.int32).astype(f32)
        slopes.append(jnp.exp2(-(8.0 / DIFF_HEADS) * (hv + 1.0)) * LOG2E)

    @pl.when((pl.program_id(0) == 0) & (group == 0))
    def _():
        jj = lax.broadcasted_iota(jnp.int32, (ATT_ITEM, LANES), 0)
        ln = lax.broadcasted_iota(jnp.int32, (ATT_ITEM, LANES), 1)
        feat = jnp.where(ln < SLOPE_PIECES, jj & (POS_RADIX - 1),
                         jnp.where(ln < 2 * SLOPE_PIECES, jj >> POS_SHIFT, 0))
        feat = feat.astype(f32).astype(bf16)
        row = lax.broadcasted_iota(jnp.int32, (VT_PAD, seq), 0)
        ones_row = jnp.where(row == 0, 1.0, 0.0).astype(bf16)
        for e in heads:
            for blk in range(seq // ATT_ITEM):
                kaug_ref[e, blk * ATT_ITEM:(blk + 1) * ATT_ITEM, DIFF_VDIM:] = feat
            vta_ref[e, DIFF_VDIM:, :] = ones_row
        kk = lax.broadcasted_iota(jnp.int32, (tk, tq), 0)
        ii = lax.broadcasted_iota(jnp.int32, (tk, tq), 1)
        mask_ref[...] = jnp.where(kk > ii, -jnp.inf, 0.0).astype(f32)

    for e in heads:
        kaug_ref[e, :, 0:DIFF_VDIM] = k_ref[:, head_rows(e)]
        vta_ref[e, 0:DIFF_VDIM, :] = vt_ref[head_rows(e), :]

    r = lax.broadcasted_iota(jnp.int32, (VT_PAD, tq), 0)
    zhalf = jnp.zeros((DIFF_QKDIM, tq), bf16)
    saug = []
    for e in heads:
        sl = jnp.broadcast_to(slopes[e], (1, tq))
        s_hi = sl.astype(bf16).astype(f32)
        s_mid = (sl - s_hi).astype(bf16).astype(f32)
        s_lo = (sl - s_hi - s_mid).astype(bf16).astype(f32)
        piece = jnp.where((r == 0) | (r == 3), s_hi, jnp.where((r == 1) | (r == 4), s_mid, s_lo))
        srows = jnp.where(r < SLOPE_PIECES, piece,
                          jnp.where(r < 2 * SLOPE_PIECES, piece * POS_RADIX, 0.0)).astype(bf16)
        saug.append(jnp.concatenate([srows, jnp.zeros((LANES - VT_PAD, tq), bf16)], axis=0))

    def query_operands(e, q0):
        qt = qt_ref[head_rows(e), pl.ds(q0, tq)]
        return (jnp.concatenate([qt[0:DIFF_QKDIM], zhalf, saug[e]], axis=0),
                jnp.concatenate([zhalf, qt[DIFF_QKDIM:], saug[e]], axis=0))

    for e in heads:
        own = query_operands(e, 0)
        qaug_ref[e, 0] = own[0]
        qaug_ref[e, 1] = own[1]
    qpos = lax.broadcasted_iota(jnp.int32, (1, tq), 1).astype(f32)
    lam = (jnp.exp(jnp.sum(lq1_ref[...] * lk1_ref[...], axis=-1, keepdims=True))
           - jnp.exp(jnp.sum(lq2_ref[...] * lk2_ref[...], axis=-1, keepdims=True)) + lam_init)

    def tile(qi, carry):
        _tile(qi)
        return carry

    def _tile(qi, has_next=True):
        qaug = [(qaug_ref[e, 0], qaug_ref[e, 1]) for e in heads]
        if has_next:
            qaug_next = [query_operands(e, _aligned((qi + 1) * tq, tq)) for e in heads]

        m_ref[...] = jnp.full(m_ref.shape, -jnp.inf, f32)

        def scores(c, u0, n_units, mask_last, next_tile=False):
            if next_tile and not has_next:
                return
            for e in heads:
                qop = qaug_next[e][c] if next_tile else qaug[e][c]
                mx = None
                for u in range(n_units):
                    k0 = _aligned((u0 + u) * tk, tk)
                    if mask_last and u == n_units - 1:
                        k1 = _aligned((u0 + u) * tk + hk, hk)
                        ta = jnp.dot(kaug_ref[e, pl.ds(k0, hk), :], qop,
                                     preferred_element_type=f32) + mask_ref[0:hk, :]
                        tb = jnp.dot(kaug_ref[e, pl.ds(k1, hk), :], qop[:, hq:],
                                     preferred_element_type=f32) + mask_ref[hk:, hq:]
                        t_ref[e, c, u * tk:u * tk + hk, :] = ta
                        t_ref[e, c, u * tk + hk:(u + 1) * tk, hq:] = tb
                        tm = jnp.maximum(
                            jnp.max(ta, axis=0, keepdims=True),
                            jnp.concatenate([jnp.full((1, hq), -jnp.inf, f32),
                                             jnp.max(tb, axis=0, keepdims=True)], axis=1))
                    else:
                        t = jnp.dot(kaug_ref[e, pl.ds(k0, tk), :], qop, preferred_element_type=f32)
                        t_ref[e, c, u * tk:(u + 1) * tk, :] = t
                        tm = jnp.max(t, axis=0, keepdims=True)
                    mx = tm if mx is None else jnp.maximum(mx, tm)
                tmax_ref[e, c] = mx

        def softmax_pv(par, c, u0, n_units, masked=False):
            k0 = _aligned(u0 * tk, tk)
            nk = n_units * tk
            for e in heads:
                cq = slopes[e] * (jnp.asarray(u0 * tk - qi * tq, jnp.int32).astype(f32) - qpos)
                m_old = m_ref[e, c]
                m_new = jnp.maximum(m_old, tmax_ref[e, c] + cq)
                alpha = jnp.exp2(m_old - m_new)
                shift = m_new - cq
                if masked:
                    nf = nk - hk
                    k1 = _aligned(u0 * tk + nf, hk)
                    p = jnp.exp2(t_ref[e, c, 0:nf, :] - shift)
                    pb = jnp.exp2(t_ref[e, c, nf:nk, hq:] - shift[:, hq:])
                    upd = alpha * acc_ref[e, par, c] + jnp.dot(
                        vta_ref[e, :, pl.ds(k0, nf)], p.astype(bf16), preferred_element_type=f32)
                    late = jnp.dot(vta_ref[e, :, pl.ds(k1, hk)], pb.astype(bf16), preferred_element_type=f32)
                    acc_ref[e, par, c] = jnp.concatenate([upd[:, 0:hq], upd[:, hq:] + late], axis=1)
                else:
                    p = jnp.exp2(t_ref[e, c, 0:nk, :] - shift)
                    acc_ref[e, par, c] = alpha * acc_ref[e, par, c] + jnp.dot(
                        vta_ref[e, :, pl.ds(k0, nk)], p.astype(bf16), preferred_element_type=f32)
                m_ref[e, c] = m_new

        units = ATT_ITEM // tk
        nb = qi // units
        tail0 = nb * units

        def run(n_tail):
            par = (n_tail - 1) % 2

            def clear_sums():
                for e in heads:
                    acc_ref[e, par] = jnp.zeros(acc_ref.shape[2:], f32)

            @pl.when((qi % units == n_tail - 1) & (nb == 0))
            def _():
                first_tile = n_tail == 1
                clear_sums()
                if first_tile:
                    scores(0, 0, n_tail, True)
                scores(1, 0, n_tail, True)
                if not first_tile:
                    finish(qi - 1, 1 - par)
                softmax_pv(par, 0, 0, n_tail, masked=True)
                if n_tail < units:
                    scores(0, 0, n_tail + 1, True, next_tile=True)
                else:
                    scores(0, 0, units, False, next_tile=True)
                softmax_pv(par, 1, 0, n_tail, masked=True)

            @pl.when((qi % units == n_tail - 1) & (nb > 0))
            def _():
                clear_sums()

                def body(j, carry):
                    scores(1, j * units, units, False)
                    softmax_pv(par, 0, j * units, units)
                    scores(0, (j + 1) * units, units, False)
                    softmax_pv(par, 1, j * units, units)
                    return carry

                lax.fori_loop(0, nb - 1, body, 0)
                last = tail0 - units
                scores(1, last, units, False)
                finish(qi - 1, 1 - par)
                softmax_pv(par, 0, last, units)
                scores(0, tail0, n_tail, True)
                softmax_pv(par, 1, last, units)
                scores(1, tail0, n_tail, True)
                softmax_pv(par, 0, tail0, n_tail, masked=True)
                scores(0, 0, units, False, next_tile=True)
                softmax_pv(par, 1, tail0, n_tail, masked=True)

        for n_tail in range(1, units + 1):
            run(n_tail)

        if has_next:
            for e in heads:
                qaug_ref[e, 0] = qaug_next[e][0]
                qaug_ref[e, 1] = qaug_next[e][1]

    def finish(q_tile, par):
        q0 = _aligned(q_tile * tq, tq)
        for e in heads:
            a0 = acc_ref[e, par, 0]
            a1 = acc_ref[e, par, 1]
            o = (a0[0:DIFF_VDIM] / a0[DIFF_VDIM:DIFF_VDIM + 1]
                 - lam * (a1[0:DIFF_VDIM] / a1[DIFF_VDIM:DIFF_VDIM + 1]))
            ms = jnp.mean(o * o, axis=0, keepdims=True)
            on = (o * lax.rsqrt(ms + RMS_EPS)).T
            y = on * (sg_ref[...] * (1.0 - lam_init)) * gd_ref[pl.ds(q0, tq), head_rows(e)]
            o_ref[pl.ds(q0, tq), head_rows(e)] = y.astype(bf16)

    lax.fori_loop(0, n_q - 1, tile, 0)
    _tile(n_q - 1, has_next=False)
    finish(n_q - 1, (n_q - 1) % 2)


def _attention(qt, k, vt, gd, lq1, lk1, lq2, lk2, sg, lam_init):
    B, S, _ = k.shape
    assert ATT_TQ == ATT_TK and ATT_ITEM % ATT_TK == 0 and S % ATT_ITEM == 0
    assert ATT_ITEM <= POS_RADIX * POS_RADIX
    assert DIFF_HEADS % ATT_HEADS == 0
    tq = ATT_TQ
    width = ATT_HEADS * DIFF_VDIM
    grid = (B, DIFF_HEADS // ATT_HEADS)
    vec = lambda b, g: (0, 0)
    rows = lambda b, g: (b, 0, g)
    cols = lambda b, g: (b, g, 0)
    in_specs = [
        pl.BlockSpec((None, width, S), cols),
        pl.BlockSpec((None, S, width), rows),
        pl.BlockSpec((None, width, S), cols),
        pl.BlockSpec((None, S, width), rows),
        pl.BlockSpec((1, DIFF_QKDIM), vec),
        pl.BlockSpec((1, DIFF_QKDIM), vec),
        pl.BlockSpec((1, DIFF_QKDIM), vec),
        pl.BlockSpec((1, DIFF_QKDIM), vec),
        pl.BlockSpec((1, DIFF_VDIM), vec),
    ]
    n_h = ATT_HEADS
    return pl.pallas_call(
        functools.partial(_attn_kernel, lam_init=lam_init),
        grid=grid,
        in_specs=in_specs,
        out_specs=pl.BlockSpec((None, S, width), rows),
        out_shape=jax.ShapeDtypeStruct((B, S, DIFF_WIDTH), jnp.bfloat16),
        scratch_shapes=[
            pltpu.VMEM((n_h, 2, 2, DIFF_VDIM + VT_PAD, tq), jnp.float32),
            pltpu.VMEM((n_h, 2, 1, tq), jnp.float32),
            pltpu.VMEM((ATT_TK, tq), jnp.float32),
            pltpu.VMEM((n_h, 2, ATT_ITEM, tq), jnp.float32),
            pltpu.VMEM((n_h, 2, 1, tq), jnp.float32),
            pltpu.VMEM((n_h, S, 2 * LANES), jnp.bfloat16),
            pltpu.VMEM((n_h, DIFF_VDIM + VT_PAD, S), jnp.bfloat16),
            pltpu.VMEM((n_h, 2, 2 * LANES, tq), jnp.bfloat16),
        ],
        compiler_params=pltpu.CompilerParams(
            dimension_semantics=("arbitrary", "arbitrary"),
            vmem_limit_bytes=VMEM_LIMIT),
        name="diff_attn",
    )(qt, k, vt, gd, lq1, lk1, lq2, lk2, sg)


def _out_kernel(x_ref, yc_ref, yd_ref, w_ref, g_ref, o_ref):
    y = (jnp.dot(yc_ref[...], w_ref[0:CONV_WIDTH, :], preferred_element_type=jnp.float32)
         + jnp.dot(yd_ref[...], w_ref[CONV_WIDTH:, :], preferred_element_type=jnp.float32))
    ms = jnp.mean(y * y, axis=-1, keepdims=True)
    o_ref[...] = x_ref[...] + y * lax.rsqrt(ms + RMS_EPS) * g_ref[...]


def _out_proj(x, yc, yd, w_out_b, post_g):
    B, S, D = x.shape
    ts = OUT_TILE
    assert S % ts == 0
    row = lambda b, s: (b, s, 0)
    const2 = lambda b, s: (0, 0)
    return pl.pallas_call(
        _out_kernel,
        grid=(B, S // ts),
        in_specs=[
            pl.BlockSpec((None, ts, D), row),
            pl.BlockSpec((None, ts, CONV_WIDTH), row),
            pl.BlockSpec((None, ts, DIFF_WIDTH), row),
            pl.BlockSpec(w_out_b.shape, const2),
            pl.BlockSpec((1, D), const2),
        ],
        out_specs=pl.BlockSpec((None, ts, D), row),
        out_shape=jax.ShapeDtypeStruct((B, S, D), jnp.float32),
        compiler_params=pltpu.CompilerParams(
            dimension_semantics=("arbitrary", "arbitrary"),
            vmem_limit_bytes=VMEM_LIMIT),
        name="out_proj",
    )(x, yc, yd, w_out_b, post_g)


def kernel(x, pre_norm_g, w_in, conv_dw_w, conv_dw_b, conv_ln_g, conv_ln_b, conv_pw_w, conv_pw_b,
           lambda_q1, lambda_k1, lambda_q2, lambda_k2, diff_subln_g, w_out, post_norm_g):
    depth = pre_norm_g.shape[0]
    h = x
    for i in range(depth):
        w_in_b = w_in[i].astype(jnp.bfloat16)
        yc, qt, k, vt, gd = _proj_conv(
            h, pre_norm_g[i][None], w_in_b, conv_dw_w[i], conv_dw_b[i][None],
            conv_ln_g[i][None], conv_ln_b[i][None], conv_pw_w[i].astype(jnp.bfloat16), conv_pw_b[i][None])
        yd = _attention(qt, k, vt, gd, lambda_q1[i][None], lambda_k1[i][None], lambda_q2[i][None],
                        lambda_k2[i][None], diff_subln_g[i][None], _lambda_init(i))
        h = _out_proj(h, yc, yd, w_out[i].astype(jnp.bfloat16), post_norm_g[i][None])
    return h
```

```python
import functools
import math

import jax
import jax.numpy as jnp
from jax import lax
from jax.experimental import pallas as pl
from jax.experimental.pallas import tpu as pltpu

D_MODEL = 1024
CONV_WIDTH = 512
CONV_KERNEL = 31
DIFF_WIDTH = 512
DIFF_HEADS = 4
DIFF_VDIM = 128
DIFF_QKDIM = 64
RMS_EPS = 1e-6
LN_EPS = 1e-5


def _lambda_init(layer_idx):
    return 0.8 - 0.6 * math.exp(-0.3 * layer_idx)


LOG2E = math.log2(math.e)
Q_PRESCALE = (DIFF_QKDIM ** -0.5) * LOG2E

C_GLU = 0
C_CGATE = 1024
C_Q = 1536
C_K = 2048
C_V = 2560
C_DGATE = 3072

PROJ_TILE = 1024
OUT_TILE = 1024
CONV_HALO = 32
CONV_ROWS = 256
LANES = 128
ATT_HEADS = 2
ATT_TQ = 512
ATT_TK = 512
ATT_ITEM = 1024
POS_SHIFT = 8
POS_RADIX = 1 << POS_SHIFT
SLOPE_PIECES = 3
VT_PAD = 16
VMEM_LIMIT = 58 * 1024 * 1024

_NT = (((1,), (1,)), ((), ()))


def _silu(x):
    return x * jax.nn.sigmoid(x)


def _aligned(start, multiple):
    return start if isinstance(start, int) else pl.multiple_of(start, multiple)


def _proj_conv_kernel(x_ref, g_ref, w_ref, dww_ref, dwb_ref, lng_ref, lnb_ref, pww_ref, pwb_ref,
                      yc_ref, qt_ref, k_ref, vt_ref, gd_ref, hbuf, cbuf, wqt_ref, wvt_ref):
    ts = x_ref.shape[0]
    s_idx = pl.program_id(1)

    @pl.when((pl.program_id(0) == 0) & (s_idx == 0))
    def _():
        wqt_ref[...] = w_ref[:, C_Q:C_Q + DIFF_WIDTH].T
        wvt_ref[...] = w_ref[:, C_V:C_V + DIFF_WIDTH].T

    x = x_ref[...]
    ms = jnp.mean(x * x, axis=-1, keepdims=True)
    xb = (x * lax.rsqrt(ms + RMS_EPS) * g_ref[...]).astype(jnp.bfloat16)

    def proj(c0, width):
        return jnp.dot(xb, w_ref[:, c0:c0 + width], preferred_element_type=jnp.float32)

    n_slab = CONV_WIDTH // LANES

    @pl.when(s_idx == 0)
    def _():
        hbuf[:, 0:CONV_HALO, :] = jnp.zeros((n_slab, CONV_HALO, LANES), jnp.float32)

    a = proj(C_GLU, CONV_WIDTH)
    b = proj(C_GLU + CONV_WIDTH, CONV_WIDTH)
    h = a * jax.nn.sigmoid(b)
    for l in range(n_slab):
        hbuf[l, CONV_HALO:CONV_HALO + ts, :] = h[:, l * LANES:(l + 1) * LANES]

    off = CONV_HALO - (CONV_KERNEL - 1)
    half = CONV_ROWS // 2

    def conv_block(l, r0):
        lanes = slice(l * LANES, (l + 1) * LANES)
        acc = [jnp.broadcast_to(dwb_ref[:, lanes], (half, LANES)) for _ in range(2)]
        for tau in range(CONV_KERNEL):
            w = dww_ref[tau:tau + 1, lanes]
            for par in range(2):
                acc[par] = acc[par] + w * hbuf[l, pl.ds(r0 + off + tau + par, half, stride=2), :]
        for par in range(2):
            cbuf[l, pl.ds(r0 + par, half, stride=2), :] = acc[par]

    qt_ref[...] = (lax.dot_general(wqt_ref[...], xb, _NT, preferred_element_type=jnp.float32)
                   * Q_PRESCALE).astype(jnp.bfloat16)
    k_ref[...] = proj(C_K, DIFF_WIDTH).astype(jnp.bfloat16)
    vt_ref[...] = lax.dot_general(wvt_ref[...], xb, _NT,
                                  preferred_element_type=jnp.float32).astype(jnp.bfloat16)
    gd_ref[...] = _silu(proj(C_DGATE, DIFF_WIDTH))
    gate_c = _silu(proj(C_CGATE, CONV_WIDTH))

    for r0 in range(0, ts, CONV_ROWS):
        for l in range(n_slab):
            conv_block(l, r0)

    hbuf[:, 0:CONV_HALO, :] = hbuf[:, ts:ts + CONV_HALO, :]

    cv = jnp.concatenate([cbuf[l] for l in range(n_slab)], axis=-1)
    mu = jnp.mean(cv, axis=-1, keepdims=True)
    d = cv - mu
    var = jnp.mean(d * d, axis=-1, keepdims=True)
    hn = d * lax.rsqrt(var + LN_EPS) * lng_ref[...] + lnb_ref[...]
    hs = _silu(hn).astype(jnp.bfloat16)
    yc = jnp.dot(hs, pww_ref[...], preferred_element_type=jnp.float32) + pwb_ref[...]
    yc_ref[...] = (yc * gate_c).astype(jnp.bfloat16)


def _proj_conv(x, pre_g, w_in_b, dw_w, dw_b, ln_g, ln_b, pw_w_b, pw_b):
    B, S, D = x.shape
    ts = PROJ_TILE
    assert S % ts == 0 and ts % CONV_ROWS == 0
    grid = (B, S // ts)
    row = lambda b, s: (b, s, 0)
    const2 = lambda b, s: (0, 0)
    in_specs = [
        pl.BlockSpec((None, ts, D), row),
        pl.BlockSpec((1, D), const2),
        pl.BlockSpec(w_in_b.shape, const2),
        pl.BlockSpec(dw_w.shape, const2),
        pl.BlockSpec((1, CONV_WIDTH), const2),
        pl.BlockSpec((1, CONV_WIDTH), const2),
        pl.BlockSpec((1, CONV_WIDTH), const2),
        pl.BlockSpec(pw_w_b.shape, const2),
        pl.BlockSpec((1, CONV_WIDTH), const2),
    ]
    out_specs = [
        pl.BlockSpec((None, ts, CONV_WIDTH), row),
        pl.BlockSpec((None, DIFF_WIDTH, ts), lambda b, s: (b, 0, s)),
        pl.BlockSpec((None, ts, DIFF_WIDTH), row),
        pl.BlockSpec((None, DIFF_WIDTH, ts), lambda b, s: (b, 0, s)),
        pl.BlockSpec((None, ts, DIFF_WIDTH), row),
    ]
    out_shape = [
        jax.ShapeDtypeStruct((B, S, CONV_WIDTH), jnp.bfloat16),
        jax.ShapeDtypeStruct((B, DIFF_WIDTH, S), jnp.bfloat16),
        jax.ShapeDtypeStruct((B, S, DIFF_WIDTH), jnp.bfloat16),
        jax.ShapeDtypeStruct((B, DIFF_WIDTH, S), jnp.bfloat16),
        jax.ShapeDtypeStruct((B, S, DIFF_WIDTH), jnp.float32),
    ]
    return pl.pallas_call(
        _proj_conv_kernel,
        grid=grid,
        in_specs=in_specs,
        out_specs=out_specs,
        out_shape=out_shape,
        scratch_shapes=[
            pltpu.VMEM((CONV_WIDTH // LANES, CONV_HALO + ts, LANES), jnp.float32),
            pltpu.VMEM((CONV_WIDTH // LANES, ts, LANES), jnp.float32),
            pltpu.VMEM((DIFF_WIDTH, D), jnp.bfloat16),
            pltpu.VMEM((DIFF_WIDTH, D), jnp.bfloat16),
        ],
        compiler_params=pltpu.CompilerParams(
            dimension_semantics=("arbitrary", "arbitrary"),
            vmem_limit_bytes=VMEM_LIMIT),
        name="proj_conv",
    )(x, pre_g, w_in_b, dw_w, dw_b, ln_g, ln_b, pw_w_b, pw_b)


def _attn_kernel(qt_ref, k_ref, vt_ref, gd_ref, lq1_ref, lk1_ref, lq2_ref, lk2_ref, sg_ref,
                 o_ref, acc_ref, m_ref, mask_ref, t_ref, tmax_ref, kaug_ref, vta_ref, qaug_ref,
                 *, lam_init):
    tq = ATT_TQ
    tk = ATT_TK
    hq, hk = tq // 2, tk // 2
    seq = k_ref.shape[0]
    n_q = seq // tq
    group = pl.program_id(1)
    heads = range(ATT_HEADS)
    f32, bf16 = jnp.float32, jnp.bfloat16

    def head_rows(e):
        return slice(e * DIFF_VDIM, (e + 1) * DIFF_VDIM)

    slopes = []
    for e in heads:
        hv = jnp.full((1, 1), group * ATT_HEADS + e, jnp.int32).astype(f32)
        slopes.append(jnp.exp2(-(8.0 / DIFF_HEADS) * (hv + 1.0)) * LOG2E)

    @pl.when((pl.program_id(0) == 0) & (group == 0))
    def _():
        jj = lax.broadcasted_iota(jnp.int32, (ATT_ITEM, LANES), 0)
        ln = lax.broadcasted_iota(jnp.int32, (ATT_ITEM, LANES), 1)
        feat = jnp.where(ln < SLOPE_PIECES, jj & (POS_RADIX - 1),
                         jnp.where(ln < 2 * SLOPE_PIECES, jj >> POS_SHIFT, 0))
        feat = feat.astype(f32).astype(bf16)
        row = lax.broadcasted_iota(jnp.int32, (VT_PAD, seq), 0)
        ones_row = jnp.where(row == 0, 1.0, 0.0).astype(bf16)
        for e in heads:
            for blk in range(seq // ATT_ITEM):
                kaug_ref[e, blk * ATT_ITEM:(blk + 1) * ATT_ITEM, DIFF_VDIM:] = feat
            vta_ref[e, DIFF_VDIM:, :] = ones_row
        kk = lax.broadcasted_iota(jnp.int32, (tk, tq), 0)
        ii = lax.broadcasted_iota(jnp.int32, (tk, tq), 1)
        mask_ref[...] = jnp.where(kk > ii, -jnp.inf, 0.0).astype(f32)

    for e in heads:
        kaug_ref[e, :, 0:DIFF_VDIM] = k_ref[:, head_rows(e)]
        vta_ref[e, 0:DIFF_VDIM, :] = vt_ref[head_rows(e), :]

    r = lax.broadcasted_iota(jnp.int32, (VT_PAD, tq), 0)
    zhalf = jnp.zeros((DIFF_QKDIM, tq), bf16)
    saug = []
    for e in heads:
        sl = jnp.broadcast_to(slopes[e], (1, tq))
        s_hi = sl.astype(bf16).astype(f32)
        s_mid = (sl - s_hi).astype(bf16).astype(f32)
        s_lo = (sl - s_hi - s_mid).astype(bf16).astype(f32)
        piece = jnp.where((r == 0) | (r == 3), s_hi, jnp.where((r == 1) | (r == 4), s_mid, s_lo))
        srows = jnp.where(r < SLOPE_PIECES, piece,
                          jnp.where(r < 2 * SLOPE_PIECES, piece * POS_RADIX, 0.0)).astype(bf16)
        saug.append(jnp.concatenate([srows, jnp.zeros((LANES - VT_PAD, tq), bf16)], axis=0))

    def query_operands(e, q0):
        qt = qt_ref[head_rows(e), pl.ds(q0, tq)]
        return (jnp.concatenate([qt[0:DIFF_QKDIM], zhalf, saug[e]], axis=0),
                jnp.concatenate([zhalf, qt[DIFF_QKDIM:], saug[e]], axis=0))

    for e in heads:
        own = query_operands(e, 0)
        qaug_ref[e, 0] = own[0]
        qaug_ref[e, 1] = own[1]
    qpos = lax.broadcasted_iota(jnp.int32, (1, tq), 1).astype(f32)
    lam = (jnp.exp(jnp.sum(lq1_ref[...] * lk1_ref[...], axis=-1, keepdims=True))
           - jnp.exp(jnp.sum(lq2_ref[...] * lk2_ref[...], axis=-1, keepdims=True)) + lam_init)

    def tile(qi, carry):
        _tile(qi)
        return carry

    def _tile(qi, has_next=True):
        qaug = [(qaug_ref[e, 0], qaug_ref[e, 1]) for e in heads]
        if has_next:
            qaug_next = [query_operands(e, _aligned((qi + 1) * tq, tq)) for e in heads]

        m_ref[...] = jnp.full(m_ref.shape, -jnp.inf, f32)

        def scores(c, u0, n_units, mask_last, next_tile=False):
            if next_tile and not has_next:
                return
            for e in heads:
                qop = qaug_next[e][c] if next_tile else qaug[e][c]
                mx = None
                for u in range(n_units):
                    k0 = _aligned((u0 + u) * tk, tk)
                    if mask_last and u == n_units - 1:
                        k1 = _aligned((u0 + u) * tk + hk, hk)
                        ta = jnp.dot(kaug_ref[e, pl.ds(k0, hk), :], qop,
                                     preferred_element_type=f32) + mask_ref[0:hk, :]
                        tb = jnp.dot(kaug_ref[e, pl.ds(k1, hk), :], qop[:, hq:],
                                     preferred_element_type=f32) + mask_ref[hk:, hq:]
                        t_ref[e, c, u * tk:u * tk + hk, :] = ta
                        t_ref[e, c, u * tk + hk:(u + 1) * tk, hq:] = tb
                        tm = jnp.maximum(
                            jnp.max(ta, axis=0, keepdims=True),
                            jnp.concatenate([jnp.full((1, hq), -jnp.inf, f32),
                                             jnp.max(tb, axis=0, keepdims=True)], axis=1))
                    else:
                        t = jnp.dot(kaug_ref[e, pl.ds(k0, tk), :], qop, preferred_element_type=f32)
                        t_ref[e, c, u * tk:(u + 1) * tk, :] = t
                        tm = jnp.max(t, axis=0, keepdims=True)
                    mx = tm if mx is None else jnp.maximum(mx, tm)
                tmax_ref[e, c] = mx

        def softmax_pv(par, c, u0, n_units, masked=False):
            k0 = _aligned(u0 * tk, tk)
            nk = n_units * tk
            for e in heads:
                cq = slopes[e] * (jnp.asarray(u0 * tk - qi * tq, jnp.int32).astype(f32) - qpos)
                m_old = m_ref[e, c]
                m_new = jnp.maximum(m_old, tmax_ref[e, c] + cq)
                alpha = jnp.exp2(m_old - m_new)
                shift = m_new - cq
                if masked:
                    nf = nk - hk
                    k1 = _aligned(u0 * tk + nf, hk)
                    p = jnp.exp2(t_ref[e, c, 0:nf, :] - shift)
                    pb = jnp.exp2(t_ref[e, c, nf:nk, hq:] - shift[:, hq:])
                    upd = alpha * acc_ref[e, par, c] + jnp.dot(
                        vta_ref[e, :, pl.ds(k0, nf)], p.astype(bf16), preferred_element_type=f32)
                    late = jnp.dot(vta_ref[e, :, pl.ds(k1, hk)], pb.astype(bf16), preferred_element_type=f32)
                    acc_ref[e, par, c] = jnp.concatenate([upd[:, 0:hq], upd[:, hq:] + late], axis=1)
                else:
                    p = jnp.exp2(t_ref[e, c, 0:nk, :] - shift)
                    acc_ref[e, par, c] = alpha * acc_ref[e, par, c] + jnp.dot(
                        vta_ref[e, :, pl.ds(k0, nk)], p.astype(bf16), preferred_element_type=f32)
                m_ref[e, c] = m_new

        units = ATT_ITEM // tk
        nb = qi // units
        tail0 = nb * units

        def run(n_tail):
            par = (n_tail - 1) % 2

            def clear_sums():
                for e in heads:
                    acc_ref[e, par] = jnp.zeros(acc_ref.shape[2:], f32)

            @pl.when((qi % units == n_tail - 1) & (nb == 0))
            def _():
                first_tile = n_tail == 1
                clear_sums()
                if first_tile:
                    scores(0, 0, n_tail, True)
                scores(1, 0, n_tail, True)
                if not first_tile:
                    finish(qi - 1, 1 - par)
                softmax_pv(par, 0, 0, n_tail, masked=True)
                if n_tail < units:
                    scores(0, 0, n_tail + 1, True, next_tile=True)
                else:
                    scores(0, 0, units, False, next_tile=True)
                softmax_pv(par, 1, 0, n_tail, masked=True)

            @pl.when((qi % units == n_tail - 1) & (nb > 0))
            def _():
                clear_sums()

                def body(j, carry):
                    scores(1, j * units, units, False)
                    softmax_pv(par, 0, j * units, units)
                    scores(0, (j + 1) * units, units, False)
                    softmax_pv(par, 1, j * units, units)
                    return carry

                lax.fori_loop(0, nb - 1, body, 0)
                last = tail0 - units
                scores(1, last, units, False)
                finish(qi - 1, 1 - par)
                softmax_pv(par, 0, last, units)
                scores(0, tail0, n_tail, True)
                softmax_pv(par, 1, last, units)
                scores(1, tail0, n_tail, True)
                softmax_pv(par, 0, tail0, n_tail, masked=True)
                scores(0, 0, units, False, next_tile=True)
                softmax_pv(par, 1, tail0, n_tail, masked=True)

        for n_tail in range(1, units + 1):
            run(n_tail)

        if has_next:
            for e in heads:
                qaug_ref[e, 0] = qaug_next[e][0]
                qaug_ref[e, 1] = qaug_next[e][1]

    def finish(q_tile, par):
        q0 = _aligned(q_tile * tq, tq)
        for e in heads:
            a0 = acc_ref[e, par, 0]
            a1 = acc_ref[e, par, 1]
            o = (a0[0:DIFF_VDIM] / a0[DIFF_VDIM:DIFF_VDIM + 1]
                 - lam * (a1[0:DIFF_VDIM] / a1[DIFF_VDIM:DIFF_VDIM + 1]))
            ms = jnp.mean(o * o, axis=0, keepdims=True)
            on = (o * lax.rsqrt(ms + RMS_EPS)).T
            y = on * (sg_ref[...] * (1.0 - lam_init)) * gd_ref[pl.ds(q0, tq), head_rows(e)]
            o_ref[pl.ds(q0, tq), head_rows(e)] = y.astype(bf16)

    lax.fori_loop(0, n_q - 1, tile, 0)
    _tile(n_q - 1, has_next=False)
    finish(n_q - 1, (n_q - 1) % 2)


def _attention(qt, k, vt, gd, lq1, lk1, lq2, lk2, sg, lam_init):
    B, S, _ = k.shape
    assert ATT_TQ == ATT_TK and ATT_ITEM % ATT_TK == 0 and S % ATT_ITEM == 0
    assert ATT_ITEM <= POS_RADIX * POS_RADIX
    assert DIFF_HEADS % ATT_HEADS == 0
    tq = ATT_TQ
    width = ATT_HEADS * DIFF_VDIM
    grid = (B, DIFF_HEADS // ATT_HEADS)
    vec = lambda b, g: (0, 0)
    rows = lambda b, g: (b, 0, g)
    cols = lambda b, g: (b, g, 0)
    in_specs = [
        pl.BlockSpec((None, width, S), cols),
        pl.BlockSpec((None, S, width), rows),
        pl.BlockSpec((None, width, S), cols),
        pl.BlockSpec((None, S, width), rows),
        pl.BlockSpec((1, DIFF_QKDIM), vec),
        pl.BlockSpec((1, DIFF_QKDIM), vec),
        pl.BlockSpec((1, DIFF_QKDIM), vec),
        pl.BlockSpec((1, DIFF_QKDIM), vec),
        pl.BlockSpec((1, DIFF_VDIM), vec),
    ]
    n_h = ATT_HEADS
    return pl.pallas_call(
        functools.partial(_attn_kernel, lam_init=lam_init),
        grid=grid,
        in_specs=in_specs,
        out_specs=pl.BlockSpec((None, S, width), rows),
        out_shape=jax.ShapeDtypeStruct((B, S, DIFF_WIDTH), jnp.bfloat16),
        scratch_shapes=[
            pltpu.VMEM((n_h, 2, 2, DIFF_VDIM + VT_PAD, tq), jnp.float32),
            pltpu.VMEM((n_h, 2, 1, tq), jnp.float32),
            pltpu.VMEM((ATT_TK, tq), jnp.float32),
            pltpu.VMEM((n_h, 2, ATT_ITEM, tq), jnp.float32),
            pltpu.VMEM((n_h, 2, 1, tq), jnp.float32),
            pltpu.VMEM((n_h, S, 2 * LANES), jnp.bfloat16),
            pltpu.VMEM((n_h, DIFF_VDIM + VT_PAD, S), jnp.bfloat16),
            pltpu.VMEM((n_h, 2, 2 * LANES, tq), jnp.bfloat16),
        ],
        compiler_params=pltpu.CompilerParams(
            dimension_semantics=("arbitrary", "arbitrary"),
            vmem_limit_bytes=VMEM_LIMIT),
        name="diff_attn",
    )(qt, k, vt, gd, lq1, lk1, lq2, lk2, sg)


def _out_kernel(x_ref, yc_ref, yd_ref, w_ref, g_ref, o_ref):
    y = (jnp.dot(yc_ref[...], w_ref[0:CONV_WIDTH, :], preferred_element_type=jnp.float32)
         + jnp.dot(yd_ref[...], w_ref[CONV_WIDTH:, :], preferred_element_type=jnp.float32))
    ms = jnp.mean(y * y, axis=-1, keepdims=True)
    o_ref[...] = x_ref[...] + y * lax.rsqrt(ms + RMS_EPS) * g_ref[...]


def _out_proj(x, yc, yd, w_out_b, post_g):
    B, S, D = x.shape
    ts = OUT_TILE
    assert S % ts == 0
    row = lambda b, s: (b, s, 0)
    const2 = lambda b, s: (0, 0)
    return pl.pallas_call(
        _out_kernel,
        grid=(B, S // ts),
        in_specs=[
            pl.BlockSpec((None, ts, D), row),
            pl.BlockSpec((None, ts, CONV_WIDTH), row),
            pl.BlockSpec((None, ts, DIFF_WIDTH), row),
            pl.BlockSpec(w_out_b.shape, const2),
            pl.BlockSpec((1, D), const2),
        ],
        out_specs=pl.BlockSpec((None, ts, D), row),
        out_shape=jax.ShapeDtypeStruct((B, S, D), jnp.float32),
        compiler_params=pltpu.CompilerParams(
            dimension_semantics=("arbitrary", "arbitrary"),
            vmem_limit_bytes=VMEM_LIMIT),
        name="out_proj",
    )(x, yc, yd, w_out_b, post_g)


def kernel(x, pre_norm_g, w_in, conv_dw_w, conv_dw_b, conv_ln_g, conv_ln_b, conv_pw_w, conv_pw_b,
           lambda_q1, lambda_k1, lambda_q2, lambda_k2, diff_subln_g, w_out, post_norm_g):
    depth = pre_norm_g.shape[0]
    h = x
    for i in range(depth):
        w_in_b = w_in[i].astype(jnp.bfloat16)
        yc, qt, k, vt, gd = _proj_conv(
            h, pre_norm_g[i][None], w_in_b, conv_dw_w[i], conv_dw_b[i][None],
            conv_ln_g[i][None], conv_ln_b[i][None], conv_pw_w[i].astype(jnp.bfloat16), conv_pw_b[i][None])
        yd = _attention(qt, k, vt, gd, lambda_q1[i][None], lambda_k1[i][None], lambda_q2[i][None],
                        lambda_k2[i][None], diff_subln_g[i][None], _lambda_init(i))
        h = _out_proj(h, yc, yd, w_out[i].astype(jnp.bfloat16), post_norm_g[i][None])
    return h
```

```python
import functools
import math

import jax
import jax.numpy as jnp
from jax import lax
from jax.experimental import pallas as pl
from jax.experimental.pallas import tpu as pltpu

D_MODEL = 1024
CONV_WIDTH = 512
CONV_KERNEL = 31
DIFF_WIDTH = 512
DIFF_HEADS = 4
DIFF_VDIM = 128
DIFF_QKDIM = 64
RMS_EPS = 1e-6
LN_EPS = 1e-5


def _lambda_init(layer_idx):
    return 0.8 - 0.6 * math.exp(-0.3 * layer_idx)


LOG2E = math.log2(math.e)
Q_PRESCALE = (DIFF_QKDIM ** -0.5) * LOG2E

C_GLU = 0
C_CGATE = 1024
C_Q = 1536
C_K = 2048
C_V = 2560
C_DGATE = 3072

PROJ_TILE = 1024
OUT_TILE = 2048
CONV_HALO = 32
CONV_ROWS = 256
LANES = 128
ATT_HEADS = 2
ATT_TQ = 512
ATT_TK = 512
ATT_ITEM = 1024
POS_SHIFT = 8
POS_RADIX = 1 << POS_SHIFT
SLOPE_PIECES = 3
VT_PAD = 16
VMEM_LIMIT = 58 * 1024 * 1024

_NT = (((1,), (1,)), ((), ()))


def _silu(x):
    return x * jax.nn.sigmoid(x)


def _aligned(start, multiple):
    return start if isinstance(start, int) else pl.multiple_of(start, multiple)


def _proj_conv_kernel(x_ref, g_ref, w_ref, dww_ref, dwb_ref, lng_ref, lnb_ref, pww_ref, pwb_ref,
                      yc_ref, qt_ref, k_ref, vt_ref, gd_ref, hbuf, cbuf, wqt_ref, wvt_ref):
    ts = x_ref.shape[0]
    s_idx = pl.program_id(1)

    @pl.when((pl.program_id(0) == 0) & (s_idx == 0))
    def _():
        wqt_ref[...] = w_ref[:, C_Q:C_Q + DIFF_WIDTH].T
        wvt_ref[...] = w_ref[:, C_V:C_V + DIFF_WIDTH].T

    x = x_ref[...]
    ms = jnp.mean(x * x, axis=-1, keepdims=True)
    xb = (x * lax.rsqrt(ms + RMS_EPS) * g_ref[...]).astype(jnp.bfloat16)

    def proj(c0, width):
        return jnp.dot(xb, w_ref[:, c0:c0 + width], preferred_element_type=jnp.float32)

    n_slab = CONV_WIDTH // LANES

    @pl.when(s_idx == 0)
    def _():
        hbuf[:, 0:CONV_HALO, :] = jnp.zeros((n_slab, CONV_HALO, LANES), jnp.float32)

    a = proj(C_GLU, CONV_WIDTH)
    b = proj(C_GLU + CONV_WIDTH, CONV_WIDTH)
    h = a * jax.nn.sigmoid(b)
    for l in range(n_slab):
        hbuf[l, CONV_HALO:CONV_HALO + ts, :] = h[:, l * LANES:(l + 1) * LANES]

    off = CONV_HALO - (CONV_KERNEL - 1)
    half = CONV_ROWS // 2

    def conv_block(l, r0):
        lanes = slice(l * LANES, (l + 1) * LANES)
        acc = [jnp.broadcast_to(dwb_ref[:, lanes], (half, LANES)) for _ in range(2)]
        for tau in range(CONV_KERNEL):
            w = dww_ref[tau:tau + 1, lanes]
            for par in range(2):
                acc[par] = acc[par] + w * hbuf[l, pl.ds(r0 + off + tau + par, half, stride=2), :]
        for par in range(2):
            cbuf[l, pl.ds(r0 + par, half, stride=2), :] = acc[par]

    qt_ref[...] = (lax.dot_general(wqt_ref[...], xb, _NT, preferred_element_type=jnp.float32)
                   * Q_PRESCALE).astype(jnp.bfloat16)
    k_ref[...] = proj(C_K, DIFF_WIDTH).astype(jnp.bfloat16)
    vt_ref[...] = lax.dot_general(wvt_ref[...], xb, _NT,
                                  preferred_element_type=jnp.float32).astype(jnp.bfloat16)
    gd_ref[...] = _silu(proj(C_DGATE, DIFF_WIDTH))
    gate_c = _silu(proj(C_CGATE, CONV_WIDTH))

    for r0 in range(0, ts, CONV_ROWS):
        for l in range(n_slab):
            conv_block(l, r0)

    hbuf[:, 0:CONV_HALO, :] = hbuf[:, ts:ts + CONV_HALO, :]

    cv = jnp.concatenate([cbuf[l] for l in range(n_slab)], axis=-1)
    mu = jnp.mean(cv, axis=-1, keepdims=True)
    d = cv - mu
    var = jnp.mean(d * d, axis=-1, keepdims=True)
    hn = d * lax.rsqrt(var + LN_EPS) * lng_ref[...] + lnb_ref[...]
    hs = _silu(hn).astype(jnp.bfloat16)
    yc = jnp.dot(hs, pww_ref[...], preferred_element_type=jnp.float32) + pwb_ref[...]
    yc_ref[...] = (yc * gate_c).astype(jnp.bfloat16)


def _proj_conv(x, pre_g, w_in_b, dw_w, dw_b, ln_g, ln_b, pw_w_b, pw_b):
    B, S, D = x.shape
    ts = PROJ_TILE
    assert S % ts == 0 and ts % CONV_ROWS == 0
    grid = (B, S // ts)
    row = lambda b, s: (b, s, 0)
    const2 = lambda b, s: (0, 0)
    in_specs = [
        pl.BlockSpec((None, ts, D), row),
        pl.BlockSpec((1, D), const2),
        pl.BlockSpec(w_in_b.shape, const2),
        pl.BlockSpec(dw_w.shape, const2),
        pl.BlockSpec((1, CONV_WIDTH), const2),
        pl.BlockSpec((1, CONV_WIDTH), const2),
        pl.BlockSpec((1, CONV_WIDTH), const2),
        pl.BlockSpec(pw_w_b.shape, const2),
        pl.BlockSpec((1, CONV_WIDTH), const2),
    ]
    out_specs = [
        pl.BlockSpec((None, ts, CONV_WIDTH), row),
        pl.BlockSpec((None, DIFF_WIDTH, ts), lambda b, s: (b, 0, s)),
        pl.BlockSpec((None, ts, DIFF_WIDTH), row),
        pl.BlockSpec((None, DIFF_WIDTH, ts), lambda b, s: (b, 0, s)),
        pl.BlockSpec((None, ts, DIFF_WIDTH), row),
    ]
    out_shape = [
        jax.ShapeDtypeStruct((B, S, CONV_WIDTH), jnp.bfloat16),
        jax.ShapeDtypeStruct((B, DIFF_WIDTH, S), jnp.bfloat16),
        jax.ShapeDtypeStruct((B, S, DIFF_WIDTH), jnp.bfloat16),
        jax.ShapeDtypeStruct((B, DIFF_WIDTH, S), jnp.bfloat16),
        jax.ShapeDtypeStruct((B, S, DIFF_WIDTH), jnp.float32),
    ]
    return pl.pallas_call(
        _proj_conv_kernel,
        grid=grid,
        in_specs=in_specs,
        out_specs=out_specs,
        out_shape=out_shape,
        scratch_shapes=[
            pltpu.VMEM((CONV_WIDTH // LANES, CONV_HALO + ts, LANES), jnp.float32),
            pltpu.VMEM((CONV_WIDTH // LANES, ts, LANES), jnp.float32),
            pltpu.VMEM((DIFF_WIDTH, D), jnp.bfloat16),
            pltpu.VMEM((DIFF_WIDTH, D), jnp.bfloat16),
        ],
        compiler_params=pltpu.CompilerParams(
            dimension_semantics=("arbitrary", "arbitrary"),
            vmem_limit_bytes=VMEM_LIMIT),
        name="proj_conv",
    )(x, pre_g, w_in_b, dw_w, dw_b, ln_g, ln_b, pw_w_b, pw_b)


def _attn_kernel(qt_ref, k_ref, vt_ref, gd_ref, lq1_ref, lk1_ref, lq2_ref, lk2_ref, sg_ref,
                 o_ref, acc_ref, m_ref, mask_ref, t_ref, tmax_ref, kaug_ref, vta_ref, qaug_ref,
                 *, lam_init):
    tq = ATT_TQ
    tk = ATT_TK
    hq, hk = tq // 2, tk // 2
    seq = k_ref.shape[0]
    n_q = seq // tq
    group = pl.program_id(1)
    heads = range(ATT_HEADS)
    f32, bf16 = jnp.float32, jnp.bfloat16

    def head_rows(e):
        return slice(e * DIFF_VDIM, (e + 1) * DIFF_VDIM)

    slopes = []
    for e in heads:
        hv = jnp.full((1, 1), group * ATT_HEADS + e, jnp.int32).astype(f32)
        slopes.append(jnp.exp2(-(8.0 / DIFF_HEADS) * (hv + 1.0)) * LOG2E)

    @pl.when((pl.program_id(0) == 0) & (group == 0))
    def _():
        jj = lax.broadcasted_iota(jnp.int32, (ATT_ITEM, LANES), 0)
        ln = lax.broadcasted_iota(jnp.int32, (ATT_ITEM, LANES), 1)
        feat = jnp.where(ln < SLOPE_PIECES, jj & (POS_RADIX - 1),
                         jnp.where(ln < 2 * SLOPE_PIECES, jj >> POS_SHIFT, 0))
        feat = feat.astype(f32).astype(bf16)
        row = lax.broadcasted_iota(jnp.int32, (VT_PAD, seq), 0)
        ones_row = jnp.where(row == 0, 1.0, 0.0).astype(bf16)
        for e in heads:
            for blk in range(seq // ATT_ITEM):
                kaug_ref[e, blk * ATT_ITEM:(blk + 1) * ATT_ITEM, DIFF_VDIM:] = feat
            vta_ref[e, DIFF_VDIM:, :] = ones_row
        kk = lax.broadcasted_iota(jnp.int32, (tk, tq), 0)
        ii = lax.broadcasted_iota(jnp.int32, (tk, tq), 1)
        mask_ref[...] = jnp.where(kk > ii, -jnp.inf, 0.0).astype(f32)

    for e in heads:
        kaug_ref[e, :, 0:DIFF_VDIM] = k_ref[:, head_rows(e)]
        vta_ref[e, 0:DIFF_VDIM, :] = vt_ref[head_rows(e), :]

    r = lax.broadcasted_iota(jnp.int32, (VT_PAD, tq), 0)
    zhalf = jnp.zeros((DIFF_QKDIM, tq), bf16)
    saug = []
    for e in heads:
        sl = jnp.broadcast_to(slopes[e], (1, tq))
        s_hi = sl.astype(bf16).astype(f32)
        s_mid = (sl - s_hi).astype(bf16).astype(f32)
        s_lo = (sl - s_hi - s_mid).astype(bf16).astype(f32)
        piece = jnp.where((r == 0) | (r == 3), s_hi, jnp.where((r == 1) | (r == 4), s_mid, s_lo))
        srows = jnp.where(r < SLOPE_PIECES, piece,
                          jnp.where(r < 2 * SLOPE_PIECES, piece * POS_RADIX, 0.0)).astype(bf16)
        saug.append(jnp.concatenate([srows, jnp.zeros((LANES - VT_PAD, tq), bf16)], axis=0))

    def query_operands(e, q0):
        qt = qt_ref[head_rows(e), pl.ds(q0, tq)]
        return (jnp.concatenate([qt[0:DIFF_QKDIM], zhalf, saug[e]], axis=0),
                jnp.concatenate([zhalf, qt[DIFF_QKDIM:], saug[e]], axis=0))

    for e in heads:
        own = query_operands(e, 0)
        qaug_ref[e, 0] = own[0]
        qaug_ref[e, 1] = own[1]
    qpos = lax.broadcasted_iota(jnp.int32, (1, tq), 1).astype(f32)
    lam = (jnp.exp(jnp.sum(lq1_ref[...] * lk1_ref[...], axis=-1, keepdims=True))
           - jnp.exp(jnp.sum(lq2_ref[...] * lk2_ref[...], axis=-1, keepdims=True)) + lam_init)

    def tile(qi, carry):
        _tile(qi)
        return carry

    def _tile(qi, has_next=True):
        qaug = [(qaug_ref[e, 0], qaug_ref[e, 1]) for e in heads]
        if has_next:
            qaug_next = [query_operands(e, _aligned((qi + 1) * tq, tq)) for e in heads]

        m_ref[...] = jnp.full(m_ref.shape, -jnp.inf, f32)

        def scores(c, u0, n_units, mask_last, next_tile=False):
            if next_tile and not has_next:
                return
            for e in heads:
                qop = qaug_next[e][c] if next_tile else qaug[e][c]
                mx = None
                for u in range(n_units):
                    k0 = _aligned((u0 + u) * tk, tk)
                    if mask_last and u == n_units - 1:
                        k1 = _aligned((u0 + u) * tk + hk, hk)
                        ta = jnp.dot(kaug_ref[e, pl.ds(k0, hk), :], qop,
                                     preferred_element_type=f32) + mask_ref[0:hk, :]
                        tb = jnp.dot(kaug_ref[e, pl.ds(k1, hk), :], qop[:, hq:],
                                     preferred_element_type=f32) + mask_ref[hk:, hq:]
                        t_ref[e, c, u * tk:u * tk + hk, :] = ta
                        t_ref[e, c, u * tk + hk:(u + 1) * tk, hq:] = tb
                        tm = jnp.maximum(
                            jnp.max(ta, axis=0, keepdims=True),
                            jnp.concatenate([jnp.full((1, hq), -jnp.inf, f32),
                                             jnp.max(tb, axis=0, keepdims=True)], axis=1))
                    else:
                        t = jnp.dot(kaug_ref[e, pl.ds(k0, tk), :], qop, preferred_element_type=f32)
                        t_ref[e, c, u * tk:(u + 1) * tk, :] = t
                        tm = jnp.max(t, axis=0, keepdims=True)
                    mx = tm if mx is None else jnp.maximum(mx, tm)
                tmax_ref[e, c] = mx

        def softmax_pv(par, c, u0, n_units, masked=False):
            k0 = _aligned(u0 * tk, tk)
            nk = n_units * tk
            for e in heads:
                cq = slopes[e] * (jnp.asarray(u0 * tk - qi * tq, jnp.int32).astype(f32) - qpos)
                m_old = m_ref[e, c]
                m_new = jnp.maximum(m_old, tmax_ref[e, c] + cq)
                alpha = jnp.exp2(m_old - m_new)
                shift = m_new - cq
                if masked:
                    nf = nk - hk
                    k1 = _aligned(u0 * tk + nf, hk)
                    p = jnp.exp2(t_ref[e, c, 0:nf, :] - shift)
                    pb = jnp.exp2(t_ref[e, c, nf:nk, hq:] - shift[:, hq:])
                    upd = alpha * acc_ref[e, par, c] + jnp.dot(
                        vta_ref[e, :, pl.ds(k0, nf)], p.astype(bf16), preferred_element_type=f32)
                    late = jnp.dot(vta_ref[e, :, pl.ds(k1, hk)], pb.astype(bf16), preferred_element_type=f32)
                    acc_ref[e, par, c] = jnp.concatenate([upd[:, 0:hq], upd[:, hq:] + late], axis=1)
                else:
                    p = jnp.exp2(t_ref[e, c, 0:nk, :] - shift)
                    acc_ref[e, par, c] = alpha * acc_ref[e, par, c] + jnp.dot(
                        vta_ref[e, :, pl.ds(k0, nk)], p.astype(bf16), preferred_element_type=f32)
                m_ref[e, c] = m_new

        units = ATT_ITEM // tk
        nb = qi // units
        tail0 = nb * units

        def run(n_tail):
            par = (n_tail - 1) % 2

            def clear_sums():
                for e in heads:
                    acc_ref[e, par] = jnp.zeros(acc_ref.shape[2:], f32)

            @pl.when((qi % units == n_tail - 1) & (nb == 0))
            def _():
                first_tile = n_tail == 1
                clear_sums()
                if first_tile:
                    scores(0, 0, n_tail, True)
                scores(1, 0, n_tail, True)
                if not first_tile:
                    finish(qi - 1, 1 - par)
                softmax_pv(par, 0, 0, n_tail, masked=True)
                if n_tail < units:
                    scores(0, 0, n_tail + 1, True, next_tile=True)
                else:
                    scores(0, 0, units, False, next_tile=True)
                softmax_pv(par, 1, 0, n_tail, masked=True)

            @pl.when((qi % units == n_tail - 1) & (nb > 0))
            def _():
                clear_sums()

                def body(j, carry):
                    scores(1, j * units, units, False)
                    softmax_pv(par, 0, j * units, units)
                    scores(0, (j + 1) * units, units, False)
                    softmax_pv(par, 1, j * units, units)
                    return carry

                lax.fori_loop(0, nb - 1, body, 0)
                last = tail0 - units
                scores(1, last, units, False)
                finish(qi - 1, 1 - par)
                softmax_pv(par, 0, last, units)
                scores(0, tail0, n_tail, True)
                softmax_pv(par, 1, last, units)
                scores(1, tail0, n_tail, True)
                softmax_pv(par, 0, tail0, n_tail, masked=True)
                scores(0, 0, units, False, next_tile=True)
                softmax_pv(par, 1, tail0, n_tail, masked=True)

        for n_tail in range(1, units + 1):
            run(n_tail)

        if has_next:
            for e in heads:
                qaug_ref[e, 0] = qaug_next[e][0]
                qaug_ref[e, 1] = qaug_next[e][1]

    def finish(q_tile, par):
        q0 = _aligned(q_tile * tq, tq)
        for e in heads:
            a0 = acc_ref[e, par, 0]
            a1 = acc_ref[e, par, 1]
            o = (a0[0:DIFF_VDIM] / a0[DIFF_VDIM:DIFF_VDIM + 1]
                 - lam * (a1[0:DIFF_VDIM] / a1[DIFF_VDIM:DIFF_VDIM + 1]))
            ms = jnp.mean(o * o, axis=0, keepdims=True)
            on = (o * lax.rsqrt(ms + RMS_EPS)).T
            y = on * (sg_ref[...] * (1.0 - lam_init)) * gd_ref[pl.ds(q0, tq), head_rows(e)]
            o_ref[pl.ds(q0, tq), head_rows(e)] = y.astype(bf16)

    lax.fori_loop(0, n_q - 1, tile, 0)
    _tile(n_q - 1, has_next=False)
    finish(n_q - 1, (n_q - 1) % 2)


def _attention(qt, k, vt, gd, lq1, lk1, lq2, lk2, sg, lam_init):
    B, S, _ = k.shape
    assert ATT_TQ == ATT_TK and ATT_ITEM % ATT_TK == 0 and S % ATT_ITEM == 0
    assert ATT_ITEM <= POS_RADIX * POS_RADIX
    assert DIFF_HEADS % ATT_HEADS == 0
    tq = ATT_TQ
    width = ATT_HEADS * DIFF_VDIM
    grid = (B, DIFF_HEADS // ATT_HEADS)
    vec = lambda b, g: (0, 0)
    rows = lambda b, g: (b, 0, g)
    cols = lambda b, g: (b, g, 0)
    in_specs = [
        pl.BlockSpec((None, width, S), cols),
        pl.BlockSpec((None, S, width), rows),
        pl.BlockSpec((None, width, S), cols),
        pl.BlockSpec((None, S, width), rows),
        pl.BlockSpec((1, DIFF_QKDIM), vec),
        pl.BlockSpec((1, DIFF_QKDIM), vec),
        pl.BlockSpec((1, DIFF_QKDIM), vec),
        pl.BlockSpec((1, DIFF_QKDIM), vec),
        pl.BlockSpec((1, DIFF_VDIM), vec),
    ]
    n_h = ATT_HEADS
    return pl.pallas_call(
        functools.partial(_attn_kernel, lam_init=lam_init),
        grid=grid,
        in_specs=in_specs,
        out_specs=pl.BlockSpec((None, S, width), rows),
        out_shape=jax.ShapeDtypeStruct((B, S, DIFF_WIDTH), jnp.bfloat16),
        scratch_shapes=[
            pltpu.VMEM((n_h, 2, 2, DIFF_VDIM + VT_PAD, tq), jnp.float32),
            pltpu.VMEM((n_h, 2, 1, tq), jnp.float32),
            pltpu.VMEM((ATT_TK, tq), jnp.float32),
            pltpu.VMEM((n_h, 2, ATT_ITEM, tq), jnp.float32),
            pltpu.VMEM((n_h, 2, 1, tq), jnp.float32),
            pltpu.VMEM((n_h, S, 2 * LANES), jnp.bfloat16),
            pltpu.VMEM((n_h, DIFF_VDIM + VT_PAD, S), jnp.bfloat16),
            pltpu.VMEM((n_h, 2, 2 * LANES, tq), jnp.bfloat16),
        ],
        compiler_params=pltpu.CompilerParams(
            dimension_semantics=("arbitrary", "arbitrary"),
            vmem_limit_bytes=VMEM_LIMIT),
        name="diff_attn",
    )(qt, k, vt, gd, lq1, lk1, lq2, lk2, sg)


def _out_kernel(x_ref, yc_ref, yd_ref, w_ref, g_ref, o_ref):
    y = (jnp.dot(yc_ref[...], w_ref[0:CONV_WIDTH, :], preferred_element_type=jnp.float32)
         + jnp.dot(yd_ref[...], w_ref[CONV_WIDTH:, :], preferred_element_type=jnp.float32))
    ms = jnp.mean(y * y, axis=-1, keepdims=True)
    o_ref[...] = x_ref[...] + y * lax.rsqrt(ms + RMS_EPS) * g_ref[...]


def _out_proj(x, yc, yd, w_out_b, post_g):
    B, S, D = x.shape
    ts = OUT_TILE
    assert S % ts == 0
    row = lambda b, s: (b, s, 0)
    const2 = lambda b, s: (0, 0)
    return pl.pallas_call(
        _out_kernel,
        grid=(B, S // ts),
        in_specs=[
            pl.BlockSpec((None, ts, D), row),
            pl.BlockSpec((None, ts, CONV_WIDTH), row),
            pl.BlockSpec((None, ts, DIFF_WIDTH), row),
            pl.BlockSpec(w_out_b.shape, const2),
            pl.BlockSpec((1, D), const2),
        ],
        out_specs=pl.BlockSpec((None, ts, D), row),
        out_shape=jax.ShapeDtypeStruct((B, S, D), jnp.float32),
        compiler_params=pltpu.CompilerParams(
            dimension_semantics=("arbitrary", "arbitrary"),
            vmem_limit_bytes=VMEM_LIMIT),
        name="out_proj",
    )(x, yc, yd, w_out_b, post_g)


def kernel(x, pre_norm_g, w_in, conv_dw_w, conv_dw_b, conv_ln_g, conv_ln_b, conv_pw_w, conv_pw_b,
           lambda_q1, lambda_k1, lambda_q2, lambda_k2, diff_subln_g, w_out, post_norm_g):
    depth = pre_norm_g.shape[0]
    h = x
    for i in range(depth):
        w_in_b = w_in[i].astype(jnp.bfloat16)
        yc, qt, k, vt, gd = _proj_conv(
            h, pre_norm_g[i][None], w_in_b, conv_dw_w[i], conv_dw_b[i][None],
            conv_ln_g[i][None], conv_ln_b[i][None], conv_pw_w[i].astype(jnp.bfloat16), conv_pw_b[i][None])
        yd = _attention(qt, k, vt, gd, lambda_q1[i][None], lambda_k1[i][None], lambda_q2[i][None],
                        lambda_k2[i][None], diff_subln_g[i][None], _lambda_init(i))
        h = _out_proj(h, yc, yd, w_out[i].astype(jnp.bfloat16), post_norm_g[i][None])
    return h
```

```python
import functools
import math

import jax
import jax.numpy as jnp
from jax import lax
from jax.experimental import pallas as pl
from jax.experimental.pallas import tpu as pltpu

D_MODEL = 1024
CONV_WIDTH = 512
CONV_KERNEL = 31
DIFF_WIDTH = 512
DIFF_HEADS = 4
DIFF_VDIM = 128
DIFF_QKDIM = 64
RMS_EPS = 1e-6
LN_EPS = 1e-5


def _lambda_init(layer_idx):
    return 0.8 - 0.6 * math.exp(-0.3 * layer_idx)


LOG2E = math.log2(math.e)
Q_PRESCALE = (DIFF_QKDIM ** -0.5) * LOG2E

C_GLU = 0
C_CGATE = 1024
C_Q = 1536
C_K = 2048
C_V = 2560
C_DGATE = 3072

PROJ_TILE = 1024
OUT_TILE = 2048
CONV_HALO = 32
CONV_ROWS = 256
LANES = 128
ATT_HEADS = 2
ATT_TQ = 512
ATT_TK = 512
ATT_ITEM = 1024
POS_SHIFT = 8
POS_RADIX = 1 << POS_SHIFT
SLOPE_PIECES = 3
VT_PAD = 16
VMEM_BYTES = 64 * 1024 * 1024
VMEM_LIMIT = VMEM_BYTES - 6 * 1024 * 1024

_NT = (((1,), (1,)), ((), ()))


def _silu(x):
    return x * jax.nn.sigmoid(x)


def _aligned(start, multiple):
    return start if isinstance(start, int) else pl.multiple_of(start, multiple)


def _proj_conv_kernel(x_ref, g_ref, w_ref, dww_ref, dwb_ref, lng_ref, lnb_ref, pww_ref, pwb_ref,
                      yc_ref, qt_ref, k_ref, vt_ref, gd_ref, hbuf, cbuf, wqt_ref, wvt_ref):
    ts = x_ref.shape[0]
    s_idx = pl.program_id(1)

    @pl.when((pl.program_id(0) == 0) & (s_idx == 0))
    def _():
        wqt_ref[...] = w_ref[:, C_Q:C_Q + DIFF_WIDTH].T
        wvt_ref[...] = w_ref[:, C_V:C_V + DIFF_WIDTH].T

    x = x_ref[...]
    ms = jnp.mean(x * x, axis=-1, keepdims=True)
    xb = (x * lax.rsqrt(ms + RMS_EPS) * g_ref[...]).astype(jnp.bfloat16)

    def proj(c0, width):
        return jnp.dot(xb, w_ref[:, c0:c0 + width], preferred_element_type=jnp.float32)

    n_slab = CONV_WIDTH // LANES

    @pl.when(s_idx == 0)
    def _():
        hbuf[:, 0:CONV_HALO, :] = jnp.zeros((n_slab, CONV_HALO, LANES), jnp.float32)

    a = proj(C_GLU, CONV_WIDTH)
    b = proj(C_GLU + CONV_WIDTH, CONV_WIDTH)
    h = a * jax.nn.sigmoid(b)
    for l in range(n_slab):
        hbuf[l, CONV_HALO:CONV_HALO + ts, :] = h[:, l * LANES:(l + 1) * LANES]

    off = CONV_HALO - (CONV_KERNEL - 1)
    half = CONV_ROWS // 2

    def conv_block(l, r0):
        lanes = slice(l * LANES, (l + 1) * LANES)
        acc = [jnp.broadcast_to(dwb_ref[:, lanes], (half, LANES)) for _ in range(2)]
        for tau in range(CONV_KERNEL):
            w = dww_ref[tau:tau + 1, lanes]
            for par in range(2):
                acc[par] = acc[par] + w * hbuf[l, pl.ds(r0 + off + tau + par, half, stride=2), :]
        for par in range(2):
            cbuf[l, pl.ds(r0 + par, half, stride=2), :] = acc[par]

    qt_ref[...] = (lax.dot_general(wqt_ref[...], xb, _NT, preferred_element_type=jnp.float32)
                   * Q_PRESCALE).astype(jnp.bfloat16)
    k_ref[...] = proj(C_K, DIFF_WIDTH).astype(jnp.bfloat16)
    vt_ref[...] = lax.dot_general(wvt_ref[...], xb, _NT,
                                  preferred_element_type=jnp.float32).astype(jnp.bfloat16)
    gd_ref[...] = _silu(proj(C_DGATE, DIFF_WIDTH))
    gate_c = _silu(proj(C_CGATE, CONV_WIDTH))

    for r0 in range(0, ts, CONV_ROWS):
        for l in range(n_slab):
            conv_block(l, r0)

    hbuf[:, 0:CONV_HALO, :] = hbuf[:, ts:ts + CONV_HALO, :]

    cv = jnp.concatenate([cbuf[l] for l in range(n_slab)], axis=-1)
    mu = jnp.mean(cv, axis=-1, keepdims=True)
    d = cv - mu
    var = jnp.mean(d * d, axis=-1, keepdims=True)
    hn = d * lax.rsqrt(var + LN_EPS) * lng_ref[...] + lnb_ref[...]
    hs = _silu(hn).astype(jnp.bfloat16)
    yc = jnp.dot(hs, pww_ref[...], preferred_element_type=jnp.float32) + pwb_ref[...]
    yc_ref[...] = (yc * gate_c).astype(jnp.bfloat16)


def _proj_conv(x, pre_g, w_in_b, dw_w, dw_b, ln_g, ln_b, pw_w_b, pw_b):
    B, S, D = x.shape
    ts = PROJ_TILE
    assert S % ts == 0 and ts % CONV_ROWS == 0
    grid = (B, S // ts)
    row = lambda b, s: (b, s, 0)
    const2 = lambda b, s: (0, 0)
    in_specs = [
        pl.BlockSpec((None, ts, D), row),
        pl.BlockSpec((1, D), const2),
        pl.BlockSpec(w_in_b.shape, const2),
        pl.BlockSpec(dw_w.shape, const2),
        pl.BlockSpec((1, CONV_WIDTH), const2),
        pl.BlockSpec((1, CONV_WIDTH), const2),
        pl.BlockSpec((1, CONV_WIDTH), const2),
        pl.BlockSpec(pw_w_b.shape, const2),
        pl.BlockSpec((1, CONV_WIDTH), const2),
    ]
    out_specs = [
        pl.BlockSpec((None, ts, CONV_WIDTH), row),
        pl.BlockSpec((None, DIFF_WIDTH, ts), lambda b, s: (b, 0, s)),
        pl.BlockSpec((None, ts, DIFF_WIDTH), row),
        pl.BlockSpec((None, DIFF_WIDTH, ts), lambda b, s: (b, 0, s)),
        pl.BlockSpec((None, ts, DIFF_WIDTH), row),
    ]
    out_shape = [
        jax.ShapeDtypeStruct((B, S, CONV_WIDTH), jnp.bfloat16),
        jax.ShapeDtypeStruct((B, DIFF_WIDTH, S), jnp.bfloat16),
        jax.ShapeDtypeStruct((B, S, DIFF_WIDTH), jnp.bfloat16),
        jax.ShapeDtypeStruct((B, DIFF_WIDTH, S), jnp.bfloat16),
        jax.ShapeDtypeStruct((B, S, DIFF_WIDTH), jnp.float32),
    ]
    return pl.pallas_call(
        _proj_conv_kernel,
        grid=grid,
        in_specs=in_specs,
        out_specs=out_specs,
        out_shape=out_shape,
        scratch_shapes=[
            pltpu.VMEM((CONV_WIDTH // LANES, CONV_HALO + ts, LANES), jnp.float32),
            pltpu.VMEM((CONV_WIDTH // LANES, ts, LANES), jnp.float32),
            pltpu.VMEM((DIFF_WIDTH, D), jnp.bfloat16),
            pltpu.VMEM((DIFF_WIDTH, D), jnp.bfloat16),
        ],
        compiler_params=pltpu.CompilerParams(
            dimension_semantics=("arbitrary", "arbitrary"),
            vmem_limit_bytes=VMEM_LIMIT),
        name="proj_conv",
    )(x, pre_g, w_in_b, dw_w, dw_b, ln_g, ln_b, pw_w_b, pw_b)


def _attn_kernel(qt_ref, k_ref, vt_ref, gd_ref, lq1_ref, lk1_ref, lq2_ref, lk2_ref, sg_ref,
                 o_ref, acc_ref, m_ref, mask_ref, t_ref, tmax_ref, kaug_ref, vta_ref, qaug_ref,
                 *, lam_init):
    tq = ATT_TQ
    tk = ATT_TK
    hq, hk = tq // 2, tk // 2
    seq = k_ref.shape[0]
    n_q = seq // tq
    group = pl.program_id(1)
    heads = range(ATT_HEADS)
    f32, bf16 = jnp.float32, jnp.bfloat16

    def head_rows(e):
        return slice(e * DIFF_VDIM, (e + 1) * DIFF_VDIM)

    slopes = []
    for e in heads:
        hv = jnp.full((1, 1), group * ATT_HEADS + e, jnp.int32).astype(f32)
        slopes.append(jnp.exp2(-(8.0 / DIFF_HEADS) * (hv + 1.0)) * LOG2E)

    @pl.when((pl.program_id(0) == 0) & (group == 0))
    def _():
        jj = lax.broadcasted_iota(jnp.int32, (ATT_ITEM, LANES), 0)
        ln = lax.broadcasted_iota(jnp.int32, (ATT_ITEM, LANES), 1)
        feat = jnp.where(ln < SLOPE_PIECES, jj & (POS_RADIX - 1),
                         jnp.where(ln < 2 * SLOPE_PIECES, jj >> POS_SHIFT, 0))
        feat = feat.astype(f32).astype(bf16)
        row = lax.broadcasted_iota(jnp.int32, (VT_PAD, seq), 0)
        ones_row = jnp.where(row == 0, 1.0, 0.0).astype(bf16)
        for e in heads:
            for blk in range(seq // ATT_ITEM):
                kaug_ref[e, blk * ATT_ITEM:(blk + 1) * ATT_ITEM, DIFF_VDIM:] = feat
            vta_ref[e, DIFF_VDIM:, :] = ones_row
        kk = lax.broadcasted_iota(jnp.int32, (tk, tq), 0)
        ii = lax.broadcasted_iota(jnp.int32, (tk, tq), 1)
        mask_ref[...] = jnp.where(kk > ii, -jnp.inf, 0.0).astype(f32)

    for e in heads:
        kaug_ref[e, :, 0:DIFF_VDIM] = k_ref[:, head_rows(e)]
        vta_ref[e, 0:DIFF_VDIM, :] = vt_ref[head_rows(e), :]

    r = lax.broadcasted_iota(jnp.int32, (VT_PAD, tq), 0)
    zhalf = jnp.zeros((DIFF_QKDIM, tq), bf16)
    saug = []
    for e in heads:
        sl = jnp.broadcast_to(slopes[e], (1, tq))
        s_hi = sl.astype(bf16).astype(f32)
        s_mid = (sl - s_hi).astype(bf16).astype(f32)
        s_lo = (sl - s_hi - s_mid).astype(bf16).astype(f32)
        piece = jnp.where((r == 0) | (r == 3), s_hi, jnp.where((r == 1) | (r == 4), s_mid, s_lo))
        srows = jnp.where(r < SLOPE_PIECES, piece,
                          jnp.where(r < 2 * SLOPE_PIECES, piece * POS_RADIX, 0.0)).astype(bf16)
        saug.append(jnp.concatenate([srows, jnp.zeros((LANES - VT_PAD, tq), bf16)], axis=0))

    def query_operands(e, q0):
        qt = qt_ref[head_rows(e), pl.ds(q0, tq)]
        return (jnp.concatenate([qt[0:DIFF_QKDIM], zhalf, saug[e]], axis=0),
                jnp.concatenate([zhalf, qt[DIFF_QKDIM:], saug[e]], axis=0))

    for e in heads:
        own = query_operands(e, 0)
        qaug_ref[e, 0] = own[0]
        qaug_ref[e, 1] = own[1]
    qpos = lax.broadcasted_iota(jnp.int32, (1, tq), 1).astype(f32)
    lam = (jnp.exp(jnp.sum(lq1_ref[...] * lk1_ref[...], axis=-1, keepdims=True))
           - jnp.exp(jnp.sum(lq2_ref[...] * lk2_ref[...], axis=-1, keepdims=True)) + lam_init)

    def tile(qi, carry):
        _tile(qi)
        return carry

    def _tile(qi, has_next=True):
        qaug = [(qaug_ref[e, 0], qaug_ref[e, 1]) for e in heads]
        if has_next:
            qaug_next = [query_operands(e, _aligned((qi + 1) * tq, tq)) for e in heads]

        m_ref[...] = jnp.full(m_ref.shape, -jnp.inf, f32)

        def scores(c, u0, n_units, mask_last, next_tile=False):
            if next_tile and not has_next:
                return
            for e in heads:
                qop = qaug_next[e][c] if next_tile else qaug[e][c]
                mx = None
                for u in range(n_units):
                    k0 = _aligned((u0 + u) * tk, tk)
                    if mask_last and u == n_units - 1:
                        k1 = _aligned((u0 + u) * tk + hk, hk)
                        ta = jnp.dot(kaug_ref[e, pl.ds(k0, hk), :], qop,
                                     preferred_element_type=f32) + mask_ref[0:hk, :]
                        tb = jnp.dot(kaug_ref[e, pl.ds(k1, hk), :], qop[:, hq:],
                                     preferred_element_type=f32) + mask_ref[hk:, hq:]
                        t_ref[e, c, u * tk:u * tk + hk, :] = ta
                        t_ref[e, c, u * tk + hk:(u + 1) * tk, hq:] = tb
                        tm = jnp.maximum(
                            jnp.max(ta, axis=0, keepdims=True),
                            jnp.concatenate([jnp.full((1, hq), -jnp.inf, f32),
                                             jnp.max(tb, axis=0, keepdims=True)], axis=1))
                    else:
                        t = jnp.dot(kaug_ref[e, pl.ds(k0, tk), :], qop, preferred_element_type=f32)
                        t_ref[e, c, u * tk:(u + 1) * tk, :] = t
                        tm = jnp.max(t, axis=0, keepdims=True)
                    mx = tm if mx is None else jnp.maximum(mx, tm)
                tmax_ref[e, c] = mx

        def softmax_pv(par, c, u0, n_units, masked=False):
            k0 = _aligned(u0 * tk, tk)
            nk = n_units * tk
            for e in heads:
                cq = slopes[e] * (jnp.asarray(u0 * tk - qi * tq, jnp.int32).astype(f32) - qpos)
                m_old = m_ref[e, c]
                m_new = jnp.maximum(m_old, tmax_ref[e, c] + cq)
                alpha = jnp.exp2(m_old - m_new)
                shift = m_new - cq
                if masked:
                    nf = nk - hk
                    k1 = _aligned(u0 * tk + nf, hk)
                    p = jnp.exp2(t_ref[e, c, 0:nf, :] - shift)
                    pb = jnp.exp2(t_ref[e, c, nf:nk, hq:] - shift[:, hq:])
                    upd = alpha * acc_ref[e, par, c] + jnp.dot(
                        vta_ref[e, :, pl.ds(k0, nf)], p.astype(bf16), preferred_element_type=f32)
                    late = jnp.dot(vta_ref[e, :, pl.ds(k1, hk)], pb.astype(bf16), preferred_element_type=f32)
                    acc_ref[e, par, c] = jnp.concatenate([upd[:, 0:hq], upd[:, hq:] + late], axis=1)
                else:
                    p = jnp.exp2(t_ref[e, c, 0:nk, :] - shift)
                    acc_ref[e, par, c] = alpha * acc_ref[e, par, c] + jnp.dot(
                        vta_ref[e, :, pl.ds(k0, nk)], p.astype(bf16), preferred_element_type=f32)
                m_ref[e, c] = m_new

        units = ATT_ITEM // tk
        nb = qi // units
        tail0 = nb * units

        def run(n_tail):
            par = (n_tail - 1) % 2

            def clear_sums():
                for e in heads:
                    acc_ref[e, par] = jnp.zeros(acc_ref.shape[2:], f32)

            @pl.when((qi % units == n_tail - 1) & (nb == 0))
            def _():
                first_tile = n_tail == 1
                clear_sums()
                if first_tile:
                    scores(0, 0, n_tail, True)
                scores(1, 0, n_tail, True)
                if not first_tile:
                    finish(qi - 1, 1 - par)
                softmax_pv(par, 0, 0, n_tail, masked=True)
                if n_tail < units:
                    scores(0, 0, n_tail + 1, True, next_tile=True)
                else:
                    scores(0, 0, units, False, next_tile=True)
                softmax_pv(par, 1, 0, n_tail, masked=True)

            @pl.when((qi % units == n_tail - 1) & (nb > 0))
            def _():
                clear_sums()

                def body(j, carry):
                    scores(1, j * units, units, False)
                    softmax_pv(par, 0, j * units, units)
                    scores(0, (j + 1) * units, units, False)
                    softmax_pv(par, 1, j * units, units)
                    return carry

                lax.fori_loop(0, nb - 1, body, 0)
                last = tail0 - units
                scores(1, last, units, False)
                finish(qi - 1, 1 - par)
                softmax_pv(par, 0, last, units)
                scores(0, tail0, n_tail, True)
                softmax_pv(par, 1, last, units)
                scores(1, tail0, n_tail, True)
                softmax_pv(par, 0, tail0, n_tail, masked=True)
                scores(0, 0, units, False, next_tile=True)
                softmax_pv(par, 1, tail0, n_tail, masked=True)

        for n_tail in range(1, units + 1):
            run(n_tail)

        if has_next:
            for e in heads:
                qaug_ref[e, 0] = qaug_next[e][0]
                qaug_ref[e, 1] = qaug_next[e][1]

    def finish(q_tile, par):
        q0 = _aligned(q_tile * tq, tq)
        for e in heads:
            a0 = acc_ref[e, par, 0]
            a1 = acc_ref[e, par, 1]
            o = (a0[0:DIFF_VDIM] / a0[DIFF_VDIM:DIFF_VDIM + 1]
                 - lam * (a1[0:DIFF_VDIM] / a1[DIFF_VDIM:DIFF_VDIM + 1]))
            ms = jnp.mean(o * o, axis=0, keepdims=True)
            on = (o * lax.rsqrt(ms + RMS_EPS)).T
            y = on * (sg_ref[...] * (1.0 - lam_init)) * gd_ref[pl.ds(q0, tq), head_rows(e)]
            o_ref[pl.ds(q0, tq), head_rows(e)] = y.astype(bf16)

    lax.fori_loop(0, n_q - 1, tile, 0)
    _tile(n_q - 1, has_next=False)
    finish(n_q - 1, (n_q - 1) % 2)


def _attention(qt, k, vt, gd, lq1, lk1, lq2, lk2, sg, lam_init):
    B, S, _ = k.shape
    assert ATT_TQ == ATT_TK and ATT_ITEM % ATT_TK == 0 and S % ATT_ITEM == 0
    assert ATT_ITEM <= POS_RADIX * POS_RADIX
    assert DIFF_HEADS % ATT_HEADS == 0
    tq = ATT_TQ
    width = ATT_HEADS * DIFF_VDIM
    grid = (B, DIFF_HEADS // ATT_HEADS)
    vec = lambda b, g: (0, 0)
    rows = lambda b, g: (b, 0, g)
    cols = lambda b, g: (b, g, 0)
    in_specs = [
        pl.BlockSpec((None, width, S), cols),
        pl.BlockSpec((None, S, width), rows),
        pl.BlockSpec((None, width, S), cols),
        pl.BlockSpec((None, S, width), rows),
        pl.BlockSpec((1, DIFF_QKDIM), vec),
        pl.BlockSpec((1, DIFF_QKDIM), vec),
        pl.BlockSpec((1, DIFF_QKDIM), vec),
        pl.BlockSpec((1, DIFF_QKDIM), vec),
        pl.BlockSpec((1, DIFF_VDIM), vec),
    ]
    n_h = ATT_HEADS
    return pl.pallas_call(
        functools.partial(_attn_kernel, lam_init=lam_init),
        grid=grid,
        in_specs=in_specs,
        out_specs=pl.BlockSpec((None, S, width), rows),
        out_shape=jax.ShapeDtypeStruct((B, S, DIFF_WIDTH), jnp.bfloat16),
        scratch_shapes=[
            pltpu.VMEM((n_h, 2, 2, DIFF_VDIM + VT_PAD, tq), jnp.float32),
            pltpu.VMEM((n_h, 2, 1, tq), jnp.float32),
            pltpu.VMEM((ATT_TK, tq), jnp.float32),
            pltpu.VMEM((n_h, 2, ATT_ITEM, tq), jnp.float32),
            pltpu.VMEM((n_h, 2, 1, tq), jnp.float32),
            pltpu.VMEM((n_h, S, 2 * LANES), jnp.bfloat16),
            pltpu.VMEM((n_h, DIFF_VDIM + VT_PAD, S), jnp.bfloat16),
            pltpu.VMEM((n_h, 2, 2 * LANES, tq), jnp.bfloat16),
        ],
        compiler_params=pltpu.CompilerParams(
            dimension_semantics=("arbitrary", "arbitrary"),
            vmem_limit_bytes=VMEM_LIMIT),
        name="diff_attn",
    )(qt, k, vt, gd, lq1, lk1, lq2, lk2, sg)


def _out_kernel(x_ref, yc_ref, yd_ref, w_ref, g_ref, o_ref):
    y = (jnp.dot(yc_ref[...], w_ref[0:CONV_WIDTH, :], preferred_element_type=jnp.float32)
         + jnp.dot(yd_ref[...], w_ref[CONV_WIDTH:, :], preferred_element_type=jnp.float32))
    ms = jnp.mean(y * y, axis=-1, keepdims=True)
    o_ref[...] = x_ref[...] + y * lax.rsqrt(ms + RMS_EPS) * g_ref[...]


def _out_proj(x, yc, yd, w_out_b, post_g):
    B, S, D = x.shape
    ts = OUT_TILE
    assert S % ts == 0
    row = lambda b, s: (b, s, 0)
    const2 = lambda b, s: (0, 0)
    return pl.pallas_call(
        _out_kernel,
        grid=(B, S // ts),
        in_specs=[
            pl.BlockSpec((None, ts, D), row),
            pl.BlockSpec((None, ts, CONV_WIDTH), row),
            pl.BlockSpec((None, ts, DIFF_WIDTH), row),
            pl.BlockSpec(w_out_b.shape, const2),
            pl.BlockSpec((1, D), const2),
        ],
        out_specs=pl.BlockSpec((None, ts, D), row),
        out_shape=jax.ShapeDtypeStruct((B, S, D), jnp.float32),
        compiler_params=pltpu.CompilerParams(
            dimension_semantics=("arbitrary", "arbitrary"),
            vmem_limit_bytes=VMEM_LIMIT),
        name="out_proj",
    )(x, yc, yd, w_out_b, post_g)


def kernel(x, pre_norm_g, w_in, conv_dw_w, conv_dw_b, conv_ln_g, conv_ln_b, conv_pw_w, conv_pw_b,
           lambda_q1, lambda_k1, lambda_q2, lambda_k2, diff_subln_g, w_out, post_norm_g):
    depth = pre_norm_g.shape[0]
    h = x
    for i in range(depth):
        w_in_b = w_in[i].astype(jnp.bfloat16)
        yc, qt, k, vt, gd = _proj_conv(
            h, pre_norm_g[i][None], w_in_b, conv_dw_w[i], conv_dw_b[i][None],
            conv_ln_g[i][None], conv_ln_b[i][None], conv_pw_w[i].astype(jnp.bfloat16), conv_pw_b[i][None])
        yd = _attention(qt, k, vt, gd, lambda_q1[i][None], lambda_k1[i][None], lambda_q2[i][None],
                        lambda_k2[i][None], diff_subln_g[i][None], _lambda_init(i))
        h = _out_proj(h, yc, yd, w_out[i].astype(jnp.bfloat16), post_norm_g[i][None])
    return h
```

```python
import functools
import math

import jax
import jax.numpy as jnp
from jax import lax
from jax.experimental import pallas as pl
from jax.experimental.pallas import tpu as pltpu

D_MODEL = 1024
CONV_WIDTH = 512
CONV_KERNEL = 31
DIFF_WIDTH = 512
DIFF_HEADS = 4
DIFF_VDIM = 128
DIFF_QKDIM = 64
RMS_EPS = 1e-6
LN_EPS = 1e-5


def _lambda_init(layer_idx):
    return 0.8 - 0.6 * math.exp(-0.3 * layer_idx)


LOG2E = math.log2(math.e)
Q_PRESCALE = (DIFF_QKDIM ** -0.5) * LOG2E

C_GLU = 0
C_CGATE = 1024
C_Q = 1536
C_K = 2048
C_V = 2560
C_DGATE = 3072

PROJ_TILE = 1024
OUT_TILE = 2048
CONV_HALO = 32
PROJ_PIECE = 256
CONV_GROUP = 8
SUBLANES = 8
LANES = 128
ATT_HEADS = 2
ATT_TQ = 512
ATT_TK = 512
ATT_ITEM = 1024
POS_SHIFT = 8
POS_RADIX = 1 << POS_SHIFT
SLOPE_PIECES = 3
VT_PAD = 16
VMEM_BYTES = 64 * 1024 * 1024
VMEM_LIMIT = VMEM_BYTES - 6 * 1024 * 1024

_NT = (((1,), (1,)), ((), ()))


def _silu(x):
    return x * jax.nn.sigmoid(x)


def _aligned(start, multiple):
    return start if isinstance(start, int) else pl.multiple_of(start, multiple)


def _proj_conv_kernel(x_ref, g_ref, w_ref, dww_ref, dwb_ref, lng_ref, lnb_ref, pww_ref, pwb_ref,
                      yc_ref, qt_ref, k_ref, vt_ref, gd_ref, hnat, pbuf, cbuf, cnat, wqt_ref, wvt_ref, xb_ref,
                      carry, sem):
    ts = x_ref.shape[0]
    s_idx = pl.program_id(1)
    f32, bf16 = jnp.float32, jnp.bfloat16

    @pl.when((pl.program_id(0) == 0) & (s_idx == 0))
    def _():
        wqt_ref[...] = w_ref[:, C_Q:C_Q + DIFF_WIDTH].T
        wvt_ref[...] = w_ref[:, C_V:C_V + DIFF_WIDTH].T

    x = x_ref[...]
    ms = jnp.mean(x * x, axis=-1, keepdims=True)
    xb_ref[...] = (x * lax.rsqrt(ms + RMS_EPS) * g_ref[...]).astype(bf16)

    def proj(c0, width):
        return jnp.dot(xb_ref[...], w_ref[:, c0:c0 + width], preferred_element_type=f32)

    n_slab = CONV_WIDTH // LANES
    seg = ts // SUBLANES

    def segment_rows(buf, l, i, first, count):
        return buf.at[l, pl.ds(first, count), i]

    def run_copies(copies):
        for cp in copies:
            cp.start()
        for cp in copies:
            cp.wait()

    @pl.when(s_idx == 0)
    def _():
        carry[...] = jnp.zeros(carry.shape, f32)

    a = proj(C_GLU, CONV_WIDTH)
    b = proj(C_GLU + CONV_WIDTH, CONV_WIDTH)
    h = a * jax.nn.sigmoid(b)
    for l in range(n_slab):
        hnat[l] = h[:, l * LANES:(l + 1) * LANES]

    to_segments = []
    for l in range(n_slab):
        to_segments.append(pltpu.make_async_copy(carry.at[l], segment_rows(pbuf, l, 0, 0, CONV_HALO), sem.at[0]))
        for i in range(SUBLANES):
            to_segments.append(pltpu.make_async_copy(
                hnat.at[l, pl.ds(i * seg, seg)], segment_rows(pbuf, l, i, CONV_HALO, seg), sem.at[0]))
            if i + 1 < SUBLANES:
                to_segments.append(pltpu.make_async_copy(
                    hnat.at[l, pl.ds((i + 1) * seg - CONV_HALO, CONV_HALO)],
                    segment_rows(pbuf, l, i + 1, 0, CONV_HALO), sem.at[0]))
    run_copies(to_segments)
    carry[...] = hnat[:, ts - CONV_HALO:ts, :]

    rows = CONV_GROUP * SUBLANES

    def conv_block(l, g):
        lanes = slice(l * LANES, (l + 1) * LANES)
        first = CONV_HALO + g * CONV_GROUP - (CONV_KERNEL - 1)
        win = pbuf[l, first:first + CONV_KERNEL - 1 + CONV_GROUP].reshape(-1, LANES)
        acc = jnp.broadcast_to(dwb_ref[:, lanes], (rows, LANES))
        for tau in range(CONV_KERNEL):
            acc = acc + dww_ref[tau:tau + 1, lanes] * win[tau * SUBLANES:tau * SUBLANES + rows]
        cbuf[l, g * CONV_GROUP:(g + 1) * CONV_GROUP] = acc.reshape(CONV_GROUP, SUBLANES, LANES)

    def put_qt(c):
        qt_ref[c:c + PROJ_PIECE, :] = (lax.dot_general(wqt_ref[c:c + PROJ_PIECE, :], xb_ref[...], _NT,
                                                       preferred_element_type=f32) * Q_PRESCALE).astype(bf16)

    def put_k(c):
        k_ref[:, c:c + PROJ_PIECE] = proj(C_K + c, PROJ_PIECE).astype(bf16)

    def put_vt(c):
        vt_ref[c:c + PROJ_PIECE, :] = lax.dot_general(wvt_ref[c:c + PROJ_PIECE, :], xb_ref[...], _NT,
                                                      preferred_element_type=f32).astype(bf16)

    def put_gd(c):
        gd_ref[:, c:c + PROJ_PIECE] = _silu(proj(C_DGATE + c, PROJ_PIECE))

    pieces = [(put, c) for put in (put_qt, put_k, put_vt, put_gd) for c in range(0, DIFF_WIDTH, PROJ_PIECE)]
    blocks = [(l, g) for l in range(n_slab) for g in range(seg // CONV_GROUP)]
    share = len(blocks) // len(pieces)
    assert share * len(pieces) == len(blocks)
    for n, (put, c) in enumerate(pieces):
        @pl.when(s_idx >= 0)
        def _(put=put, c=c, n=n):
            put(c)
            for l, g in blocks[n * share:(n + 1) * share]:
                conv_block(l, g)

    gate_c = _silu(proj(C_CGATE, CONV_WIDTH))

    run_copies([pltpu.make_async_copy(segment_rows(cbuf, l, i, 0, seg), cnat.at[l, pl.ds(i * seg, seg)], sem.at[0])
                for l in range(n_slab) for i in range(SUBLANES)])

    cv = jnp.concatenate([cnat[l] for l in range(n_slab)], axis=-1)
    mu = jnp.mean(cv, axis=-1, keepdims=True)
    d = cv - mu
    var = jnp.mean(d * d, axis=-1, keepdims=True)
    hn = d * lax.rsqrt(var + LN_EPS) * lng_ref[...] + lnb_ref[...]
    hs = _silu(hn).astype(bf16)
    yc = jnp.dot(hs, pww_ref[...], preferred_element_type=f32) + pwb_ref[...]
    yc_ref[...] = (yc * gate_c).astype(bf16)


def _proj_conv(x, pre_g, w_in_b, dw_w, dw_b, ln_g, ln_b, pw_w_b, pw_b):
    B, S, D = x.shape
    ts = PROJ_TILE
    assert S % ts == 0 and ts % (SUBLANES * CONV_GROUP) == 0 and ts // SUBLANES >= CONV_HALO
    grid = (B, S // ts)
    row = lambda b, s: (b, s, 0)
    const2 = lambda b, s: (0, 0)
    in_specs = [
        pl.BlockSpec((None, ts, D), row),
        pl.BlockSpec((1, D), const2),
        pl.BlockSpec(w_in_b.shape, const2),
        pl.BlockSpec(dw_w.shape, const2),
        pl.BlockSpec((1, CONV_WIDTH), const2),
        pl.BlockSpec((1, CONV_WIDTH), const2),
        pl.BlockSpec((1, CONV_WIDTH), const2),
        pl.BlockSpec(pw_w_b.shape, const2),
        pl.BlockSpec((1, CONV_WIDTH), const2),
    ]
    out_specs = [
        pl.BlockSpec((None, ts, CONV_WIDTH), row),
        pl.BlockSpec((None, DIFF_WIDTH, ts), lambda b, s: (b, 0, s)),
        pl.BlockSpec((None, ts, DIFF_WIDTH), row),
        pl.BlockSpec((None, DIFF_WIDTH, ts), lambda b, s: (b, 0, s)),
        pl.BlockSpec((None, ts, DIFF_WIDTH), row),
    ]
    out_shape = [
        jax.ShapeDtypeStruct((B, S, CONV_WIDTH), jnp.bfloat16),
        jax.ShapeDtypeStruct((B, DIFF_WIDTH, S), jnp.bfloat16),
        jax.ShapeDtypeStruct((B, S, DIFF_WIDTH), jnp.bfloat16),
        jax.ShapeDtypeStruct((B, DIFF_WIDTH, S), jnp.bfloat16),
        jax.ShapeDtypeStruct((B, S, DIFF_WIDTH), jnp.float32),
    ]
    return pl.pallas_call(
        _proj_conv_kernel,
        grid=grid,
        in_specs=in_specs,
        out_specs=out_specs,
        out_shape=out_shape,
        scratch_shapes=[
            pltpu.VMEM((CONV_WIDTH // LANES, ts, LANES), jnp.float32),
            pltpu.VMEM((CONV_WIDTH // LANES, CONV_HALO + ts // SUBLANES, SUBLANES, LANES), jnp.float32),
            pltpu.VMEM((CONV_WIDTH // LANES, ts // SUBLANES, SUBLANES, LANES), jnp.float32),
            pltpu.VMEM((CONV_WIDTH // LANES, ts, LANES), jnp.float32),
            pltpu.VMEM((DIFF_WIDTH, D), jnp.bfloat16),
            pltpu.VMEM((DIFF_WIDTH, D), jnp.bfloat16),
            pltpu.VMEM((ts, D), jnp.bfloat16),
            pltpu.VMEM((CONV_WIDTH // LANES, CONV_HALO, LANES), jnp.float32),
            pltpu.SemaphoreType.DMA((1,)),
        ],
        compiler_params=pltpu.CompilerParams(
            dimension_semantics=("arbitrary", "arbitrary"),
            vmem_limit_bytes=VMEM_LIMIT),
        name="proj_conv",
    )(x, pre_g, w_in_b, dw_w, dw_b, ln_g, ln_b, pw_w_b, pw_b)


def _attn_kernel(qt_ref, k_ref, vt_ref, gd_ref, lq1_ref, lk1_ref, lq2_ref, lk2_ref, sg_ref,
                 o_ref, acc_ref, m_ref, mask_ref, t_ref, tmax_ref, kaug_ref, vta_ref, qaug_ref,
                 *, lam_init):
    tq = ATT_TQ
    tk = ATT_TK
    hq, hk = tq // 2, tk // 2
    seq = k_ref.shape[0]
    n_q = seq // tq
    group = pl.program_id(1)
    heads = range(ATT_HEADS)
    f32, bf16 = jnp.float32, jnp.bfloat16

    def head_rows(e):
        return slice(e * DIFF_VDIM, (e + 1) * DIFF_VDIM)

    slopes = []
    for e in heads:
        hv = jnp.full((1, 1), group * ATT_HEADS + e, jnp.int32).astype(f32)
        slopes.append(jnp.exp2(-(8.0 / DIFF_HEADS) * (hv + 1.0)) * LOG2E)

    @pl.when((pl.program_id(0) == 0) & (group == 0))
    def _():
        jj = lax.broadcasted_iota(jnp.int32, (ATT_ITEM, LANES), 0)
        ln = lax.broadcasted_iota(jnp.int32, (ATT_ITEM, LANES), 1)
        feat = jnp.where(ln < SLOPE_PIECES, jj & (POS_RADIX - 1),
                         jnp.where(ln < 2 * SLOPE_PIECES, jj >> POS_SHIFT, 0))
        feat = feat.astype(f32).astype(bf16)
        row = lax.broadcasted_iota(jnp.int32, (VT_PAD, seq), 0)
        ones_row = jnp.where(row == 0, 1.0, 0.0).astype(bf16)
        for e in heads:
            for blk in range(seq // ATT_ITEM):
                kaug_ref[e, blk * ATT_ITEM:(blk + 1) * ATT_ITEM, DIFF_VDIM:] = feat
            vta_ref[e, DIFF_VDIM:, :] = ones_row
        kk = lax.broadcasted_iota(jnp.int32, (tk, tq), 0)
        ii = lax.broadcasted_iota(jnp.int32, (tk, tq), 1)
        mask_ref[...] = jnp.where(kk > ii, -jnp.inf, 0.0).astype(f32)

    for e in heads:
        kaug_ref[e, :, 0:DIFF_VDIM] = k_ref[:, head_rows(e)]
        vta_ref[e, 0:DIFF_VDIM, :] = vt_ref[head_rows(e), :]

    r = lax.broadcasted_iota(jnp.int32, (VT_PAD, tq), 0)
    zhalf = jnp.zeros((DIFF_QKDIM, tq), bf16)
    saug = []
    for e in heads:
        sl = jnp.broadcast_to(slopes[e], (1, tq))
        s_hi = sl.astype(bf16).astype(f32)
        s_mid = (sl - s_hi).astype(bf16).astype(f32)
        s_lo = (sl - s_hi - s_mid).astype(bf16).astype(f32)
        piece = jnp.where((r == 0) | (r == 3), s_hi, jnp.where((r == 1) | (r == 4), s_mid, s_lo))
        srows = jnp.where(r < SLOPE_PIECES, piece,
                          jnp.where(r < 2 * SLOPE_PIECES, piece * POS_RADIX, 0.0)).astype(bf16)
        saug.append(jnp.concatenate([srows, jnp.zeros((LANES - VT_PAD, tq), bf16)], axis=0))

    def query_operands(e, q0):
        qt = qt_ref[head_rows(e), pl.ds(q0, tq)]
        return (jnp.concatenate([qt[0:DIFF_QKDIM], zhalf, saug[e]], axis=0),
                jnp.concatenate([zhalf, qt[DIFF_QKDIM:], saug[e]], axis=0))

    for e in heads:
        own = query_operands(e, 0)
        qaug_ref[e, 0] = own[0]
        qaug_ref[e, 1] = own[1]
    qpos = lax.broadcasted_iota(jnp.int32, (1, tq), 1).astype(f32)
    lam = (jnp.exp(jnp.sum(lq1_ref[...] * lk1_ref[...], axis=-1, keepdims=True))
           - jnp.exp(jnp.sum(lq2_ref[...] * lk2_ref[...], axis=-1, keepdims=True)) + lam_init)

    def tile(qi, carry):
        _tile(qi)
        return carry

    def _tile(qi, has_next=True):
        qaug = [(qaug_ref[e, 0], qaug_ref[e, 1]) for e in heads]
        if has_next:
            qaug_next = [query_operands(e, _aligned((qi + 1) * tq, tq)) for e in heads]

        m_ref[...] = jnp.full(m_ref.shape, -jnp.inf, f32)

        def scores(c, u0, n_units, mask_last, next_tile=False):
            if next_tile and not has_next:
                return
            for e in heads:
                qop = qaug_next[e][c] if next_tile else qaug[e][c]
                mx = None
                for u in range(n_units):
                    k0 = _aligned((u0 + u) * tk, tk)
                    if mask_last and u == n_units - 1:
                        k1 = _aligned((u0 + u) * tk + hk, hk)
                        ta = jnp.dot(kaug_ref[e, pl.ds(k0, hk), :], qop,
                                     preferred_element_type=f32) + mask_ref[0:hk, :]
                        tb = jnp.dot(kaug_ref[e, pl.ds(k1, hk), :], qop[:, hq:],
                                     preferred_element_type=f32) + mask_ref[hk:, hq:]
                        t_ref[e, c, u * tk:u * tk + hk, :] = ta
                        t_ref[e, c, u * tk + hk:(u + 1) * tk, hq:] = tb
                        tm = jnp.maximum(
                            jnp.max(ta, axis=0, keepdims=True),
                            jnp.concatenate([jnp.full((1, hq), -jnp.inf, f32),
                                             jnp.max(tb, axis=0, keepdims=True)], axis=1))
                    else:
                        t = jnp.dot(kaug_ref[e, pl.ds(k0, tk), :], qop, preferred_element_type=f32)
                        t_ref[e, c, u * tk:(u + 1) * tk, :] = t
                        tm = jnp.max(t, axis=0, keepdims=True)
                    mx = tm if mx is None else jnp.maximum(mx, tm)
                tmax_ref[e, c] = mx

        def softmax_pv(par, c, u0, n_units, masked=False):
            k0 = _aligned(u0 * tk, tk)
            nk = n_units * tk
            for e in heads:
                cq = slopes[e] * (jnp.asarray(u0 * tk - qi * tq, jnp.int32).astype(f32) - qpos)
                m_old = m_ref[e, c]
                m_new = jnp.maximum(m_old, tmax_ref[e, c] + cq)
                alpha = jnp.exp2(m_old - m_new)
                shift = m_new - cq
                if masked:
                    nf = nk - hk
                    k1 = _aligned(u0 * tk + nf, hk)
                    p = jnp.exp2(t_ref[e, c, 0:nf, :] - shift)
                    pb = jnp.exp2(t_ref[e, c, nf:nk, hq:] - shift[:, hq:])
                    upd = alpha * acc_ref[e, par, c] + jnp.dot(
                        vta_ref[e, :, pl.ds(k0, nf)], p.astype(bf16), preferred_element_type=f32)
                    late = jnp.dot(vta_ref[e, :, pl.ds(k1, hk)], pb.astype(bf16), preferred_element_type=f32)
                    acc_ref[e, par, c] = jnp.concatenate([upd[:, 0:hq], upd[:, hq:] + late], axis=1)
                else:
                    p = jnp.exp2(t_ref[e, c, 0:nk, :] - shift)
                    acc_ref[e, par, c] = alpha * acc_ref[e, par, c] + jnp.dot(
                        vta_ref[e, :, pl.ds(k0, nk)], p.astype(bf16), preferred_element_type=f32)
                m_ref[e, c] = m_new

        units = ATT_ITEM // tk
        nb = qi // units
        tail0 = nb * units

        def run(n_tail):
            par = (n_tail - 1) % 2

            def clear_sums():
                for e in heads:
                    acc_ref[e, par] = jnp.zeros(acc_ref.shape[2:], f32)

            @pl.when((qi % units == n_tail - 1) & (nb == 0))
            def _():
                first_tile = n_tail == 1
                clear_sums()
                if first_tile:
                    scores(0, 0, n_tail, True)
                scores(1, 0, n_tail, True)
                if not first_tile:
                    finish(qi - 1, 1 - par)
                softmax_pv(par, 0, 0, n_tail, masked=True)
                if n_tail < units:
                    scores(0, 0, n_tail + 1, True, next_tile=True)
                else:
                    scores(0, 0, units, False, next_tile=True)
                softmax_pv(par, 1, 0, n_tail, masked=True)

            @pl.when((qi % units == n_tail - 1) & (nb > 0))
            def _():
                clear_sums()

                def body(j, carry):
                    scores(1, j * units, units, False)
                    softmax_pv(par, 0, j * units, units)
                    scores(0, (j + 1) * units, units, False)
                    softmax_pv(par, 1, j * units, units)
                    return carry

                lax.fori_loop(0, nb - 1, body, 0)
                last = tail0 - units
                scores(1, last, units, False)
                finish(qi - 1, 1 - par)
                softmax_pv(par, 0, last, units)
                scores(0, tail0, n_tail, True)
                softmax_pv(par, 1, last, units)
                scores(1, tail0, n_tail, True)
                softmax_pv(par, 0, tail0, n_tail, masked=True)
                scores(0, 0, units, False, next_tile=True)
                softmax_pv(par, 1, tail0, n_tail, masked=True)

        for n_tail in range(1, units + 1):
            run(n_tail)

        if has_next:
            for e in heads:
                qaug_ref[e, 0] = qaug_next[e][0]
                qaug_ref[e, 1] = qaug_next[e][1]

    def finish(q_tile, par):
        q0 = _aligned(q_tile * tq, tq)
        for e in heads:
            a0 = acc_ref[e, par, 0]
            a1 = acc_ref[e, par, 1]
            o = (a0[0:DIFF_VDIM] / a0[DIFF_VDIM:DIFF_VDIM + 1]
                 - lam * (a1[0:DIFF_VDIM] / a1[DIFF_VDIM:DIFF_VDIM + 1]))
            ms = jnp.mean(o * o, axis=0, keepdims=True)
            on = (o * lax.rsqrt(ms + RMS_EPS)).T
            y = on * (sg_ref[...] * (1.0 - lam_init)) * gd_ref[pl.ds(q0, tq), head_rows(e)]
            o_ref[pl.ds(q0, tq), head_rows(e)] = y.astype(bf16)

    lax.fori_loop(0, n_q - 1, tile, 0)
    _tile(n_q - 1, has_next=False)
    finish(n_q - 1, (n_q - 1) % 2)


def _attention(qt, k, vt, gd, lq1, lk1, lq2, lk2, sg, lam_init):
    B, S, _ = k.shape
    assert ATT_TQ == ATT_TK and ATT_ITEM % ATT_TK == 0 and S % ATT_ITEM == 0
    assert ATT_ITEM <= POS_RADIX * POS_RADIX
    assert DIFF_HEADS % ATT_HEADS == 0
    tq = ATT_TQ
    width = ATT_HEADS * DIFF_VDIM
    grid = (B, DIFF_HEADS // ATT_HEADS)
    vec = lambda b, g: (0, 0)
    rows = lambda b, g: (b, 0, g)
    cols = lambda b, g: (b, g, 0)
    in_specs = [
        pl.BlockSpec((None, width, S), cols),
        pl.BlockSpec((None, S, width), rows),
        pl.BlockSpec((None, width, S), cols),
        pl.BlockSpec((None, S, width), rows),
        pl.BlockSpec((1, DIFF_QKDIM), vec),
        pl.BlockSpec((1, DIFF_QKDIM), vec),
        pl.BlockSpec((1, DIFF_QKDIM), vec),
        pl.BlockSpec((1, DIFF_QKDIM), vec),
        pl.BlockSpec((1, DIFF_VDIM), vec),
    ]
    n_h = ATT_HEADS
    return pl.pallas_call(
        functools.partial(_attn_kernel, lam_init=lam_init),
        grid=grid,
        in_specs=in_specs,
        out_specs=pl.BlockSpec((None, S, width), rows),
        out_shape=jax.ShapeDtypeStruct((B, S, DIFF_WIDTH), jnp.bfloat16),
        scratch_shapes=[
            pltpu.VMEM((n_h, 2, 2, DIFF_VDIM + VT_PAD, tq), jnp.float32),
            pltpu.VMEM((n_h, 2, 1, tq), jnp.float32),
            pltpu.VMEM((ATT_TK, tq), jnp.float32),
            pltpu.VMEM((n_h, 2, ATT_ITEM, tq), jnp.float32),
            pltpu.VMEM((n_h, 2, 1, tq), jnp.float32),
            pltpu.VMEM((n_h, S, 2 * LANES), jnp.bfloat16),
            pltpu.VMEM((n_h, DIFF_VDIM + VT_PAD, S), jnp.bfloat16),
            pltpu.VMEM((n_h, 2, 2 * LANES, tq), jnp.bfloat16),
        ],
        compiler_params=pltpu.CompilerParams(
            dimension_semantics=("arbitrary", "arbitrary"),
            vmem_limit_bytes=VMEM_LIMIT),
        name="diff_attn",
    )(qt, k, vt, gd, lq1, lk1, lq2, lk2, sg)


def _out_kernel(x_ref, yc_ref, yd_ref, w_ref, g_ref, o_ref):
    y = (jnp.dot(yc_ref[...], w_ref[0:CONV_WIDTH, :], preferred_element_type=jnp.float32)
         + jnp.dot(yd_ref[...], w_ref[CONV_WIDTH:, :], preferred_element_type=jnp.float32))
    ms = jnp.mean(y * y, axis=-1, keepdims=True)
    o_ref[...] = x_ref[...] + y * lax.rsqrt(ms + RMS_EPS) * g_ref[...]


def _out_proj(x, yc, yd, w_out_b, post_g):
    B, S, D = x.shape
    ts = OUT_TILE
    assert S % ts == 0
    row = lambda b, s: (b, s, 0)
    const2 = lambda b, s: (0, 0)
    return pl.pallas_call(
        _out_kernel,
        grid=(B, S // ts),
        in_specs=[
            pl.BlockSpec((None, ts, D), row),
            pl.BlockSpec((None, ts, CONV_WIDTH), row),
            pl.BlockSpec((None, ts, DIFF_WIDTH), row),
            pl.BlockSpec(w_out_b.shape, const2),
            pl.BlockSpec((1, D), const2),
        ],
        out_specs=pl.BlockSpec((None, ts, D), row),
        out_shape=jax.ShapeDtypeStruct((B, S, D), jnp.float32),
        compiler_params=pltpu.CompilerParams(
            dimension_semantics=("arbitrary", "arbitrary"),
            vmem_limit_bytes=VMEM_LIMIT),
        name="out_proj",
    )(x, yc, yd, w_out_b, post_g)


def kernel(x, pre_norm_g, w_in, conv_dw_w, conv_dw_b, conv_ln_g, conv_ln_b, conv_pw_w, conv_pw_b,
           lambda_q1, lambda_k1, lambda_q2, lambda_k2, diff_subln_g, w_out, post_norm_g):
    depth = pre_norm_g.shape[0]
    h = x
    for i in range(depth):
        w_in_b = w_in[i].astype(jnp.bfloat16)
        yc, qt, k, vt, gd = _proj_conv(
            h, pre_norm_g[i][None], w_in_b, conv_dw_w[i], conv_dw_b[i][None],
            conv_ln_g[i][None], conv_ln_b[i][None], conv_pw_w[i].astype(jnp.bfloat16), conv_pw_b[i][None])
        yd = _attention(qt, k, vt, gd, lambda_q1[i][None], lambda_k1[i][None], lambda_q2[i][None],
                        lambda_k2[i][None], diff_subln_g[i][None], _lambda_init(i))
        h = _out_proj(h, yc, yd, w_out[i].astype(jnp.bfloat16), post_norm_g[i][None])
    return h
```

```python
import functools
import math

import jax
import jax.numpy as jnp
from jax import lax
from jax.experimental import pallas as pl
from jax.experimental.pallas import tpu as pltpu

D_MODEL = 1024
CONV_WIDTH = 512
CONV_KERNEL = 31
DIFF_WIDTH = 512
DIFF_HEADS = 4
DIFF_VDIM = 128
DIFF_QKDIM = 64
RMS_EPS = 1e-6
LN_EPS = 1e-5


def _lambda_init(layer_idx):
    return 0.8 - 0.6 * math.exp(-0.3 * layer_idx)


LOG2E = math.log2(math.e)
Q_PRESCALE = (DIFF_QKDIM ** -0.5) * LOG2E

C_GLU = 0
C_CGATE = 1024
C_Q = 1536
C_K = 2048
C_V = 2560
C_DGATE = 3072

PROJ_TILE = 1024
OUT_TILE = 2048
CONV_HALO = 32
CONV_ROWS = 256
LANES = 128
ATT_HEADS = 2
ATT_TQ = 512
ATT_TK = 512
ATT_ITEM = 1024
POS_SHIFT = 8
POS_RADIX = 1 << POS_SHIFT
SLOPE_PIECES = 3
VT_PAD = 16
VMEM_BYTES = 64 * 1024 * 1024
VMEM_LIMIT = VMEM_BYTES - 6 * 1024 * 1024

_NT = (((1,), (1,)), ((), ()))


def _silu(x):
    return x * jax.nn.sigmoid(x)


def _aligned(start, multiple):
    return start if isinstance(start, int) else pl.multiple_of(start, multiple)


def _proj_conv_kernel(x_ref, g_ref, w_ref, dww_ref, dwb_ref, lng_ref, lnb_ref, pww_ref, pwb_ref,
                      yc_ref, qt_ref, k_ref, vt_ref, gd_ref, hbuf, cbuf, wqt_ref, wvt_ref):
    ts = x_ref.shape[0]
    s_idx = pl.program_id(1)

    @pl.when((pl.program_id(0) == 0) & (s_idx == 0))
    def _():
        wqt_ref[...] = w_ref[:, C_Q:C_Q + DIFF_WIDTH].T
        wvt_ref[...] = w_ref[:, C_V:C_V + DIFF_WIDTH].T

    x = x_ref[...]
    ms = jnp.mean(x * x, axis=-1, keepdims=True)
    xb = (x * lax.rsqrt(ms + RMS_EPS) * g_ref[...]).astype(jnp.bfloat16)

    def proj(c0, width):
        return jnp.dot(xb, w_ref[:, c0:c0 + width], preferred_element_type=jnp.float32)

    n_slab = CONV_WIDTH // LANES

    @pl.when(s_idx == 0)
    def _():
        hbuf[:, 0:CONV_HALO, :] = jnp.zeros((n_slab, CONV_HALO, LANES), jnp.float32)

    a = proj(C_GLU, CONV_WIDTH)
    b = proj(C_GLU + CONV_WIDTH, CONV_WIDTH)
    h = a * jax.nn.sigmoid(b)
    for l in range(n_slab):
        hbuf[l, CONV_HALO:CONV_HALO + ts, :] = h[:, l * LANES:(l + 1) * LANES]

    off = CONV_HALO - (CONV_KERNEL - 1)
    half = CONV_ROWS // 2

    def conv_block(l, r0):
        lanes = slice(l * LANES, (l + 1) * LANES)
        acc = [jnp.broadcast_to(dwb_ref[:, lanes], (half, LANES)) for _ in range(2)]
        for tau in range(CONV_KERNEL):
            w = dww_ref[tau:tau + 1, lanes]
            for par in range(2):
                acc[par] = acc[par] + w * hbuf[l, pl.ds(r0 + off + tau + par, half, stride=2), :]
        for par in range(2):
            cbuf[l, pl.ds(r0 + par, half, stride=2), :] = acc[par]

    qt_ref[...] = (lax.dot_general(wqt_ref[...], xb, _NT, preferred_element_type=jnp.float32)
                   * Q_PRESCALE).astype(jnp.bfloat16)
    k_ref[...] = proj(C_K, DIFF_WIDTH).astype(jnp.bfloat16)
    vt_ref[...] = lax.dot_general(wvt_ref[...], xb, _NT,
                                  preferred_element_type=jnp.float32).astype(jnp.bfloat16)
    gd_ref[...] = _silu(proj(C_DGATE, DIFF_WIDTH))
    gate_c = _silu(proj(C_CGATE, CONV_WIDTH))

    for r0 in range(0, ts, CONV_ROWS):
        for l in range(n_slab):
            conv_block(l, r0)

    hbuf[:, 0:CONV_HALO, :] = hbuf[:, ts:ts + CONV_HALO, :]

    cv = jnp.concatenate([cbuf[l] for l in range(n_slab)], axis=-1)
    mu = jnp.mean(cv, axis=-1, keepdims=True)
    d = cv - mu
    var = jnp.mean(d * d, axis=-1, keepdims=True)
    hn = d * lax.rsqrt(var + LN_EPS) * lng_ref[...] + lnb_ref[...]
    hs = _silu(hn).astype(jnp.bfloat16)
    yc = jnp.dot(hs, pww_ref[...], preferred_element_type=jnp.float32) + pwb_ref[...]
    yc_ref[...] = (yc * gate_c).astype(jnp.bfloat16)


def _proj_conv(x, pre_g, w_in_b, dw_w, dw_b, ln_g, ln_b, pw_w_b, pw_b):
    B, S, D = x.shape
    ts = PROJ_TILE
    assert S % ts == 0 and ts % CONV_ROWS == 0
    grid = (B, S // ts)
    row = lambda b, s: (b, s, 0)
    const2 = lambda b, s: (0, 0)
    in_specs = [
        pl.BlockSpec((None, ts, D), row),
        pl.BlockSpec((1, D), const2),
        pl.BlockSpec(w_in_b.shape, const2),
        pl.BlockSpec(dw_w.shape, const2),
        pl.BlockSpec((1, CONV_WIDTH), const2),
        pl.BlockSpec((1, CONV_WIDTH), const2),
        pl.BlockSpec((1, CONV_WIDTH), const2),
        pl.BlockSpec(pw_w_b.shape, const2),
        pl.BlockSpec((1, CONV_WIDTH), const2),
    ]
    out_specs = [
        pl.BlockSpec((None, ts, CONV_WIDTH), row),
        pl.BlockSpec((None, DIFF_WIDTH, ts), lambda b, s: (b, 0, s)),
        pl.BlockSpec((None, ts, DIFF_WIDTH), row),
        pl.BlockSpec((None, DIFF_WIDTH, ts), lambda b, s: (b, 0, s)),
        pl.BlockSpec((None, ts, DIFF_WIDTH), row),
    ]
    out_shape = [
        jax.ShapeDtypeStruct((B, S, CONV_WIDTH), jnp.bfloat16),
        jax.ShapeDtypeStruct((B, DIFF_WIDTH, S), jnp.bfloat16),
        jax.ShapeDtypeStruct((B, S, DIFF_WIDTH), jnp.bfloat16),
        jax.ShapeDtypeStruct((B, DIFF_WIDTH, S), jnp.bfloat16),
        jax.ShapeDtypeStruct((B, S, DIFF_WIDTH), jnp.float32),
    ]
    return pl.pallas_call(
        _proj_conv_kernel,
        grid=grid,
        in_specs=in_specs,
        out_specs=out_specs,
        out_shape=out_shape,
        scratch_shapes=[
            pltpu.VMEM((CONV_WIDTH // LANES, CONV_HALO + ts, LANES), jnp.float32),
            pltpu.VMEM((CONV_WIDTH // LANES, ts, LANES), jnp.float32),
            pltpu.VMEM((DIFF_WIDTH, D), jnp.bfloat16),
            pltpu.VMEM((DIFF_WIDTH, D), jnp.bfloat16),
        ],
        compiler_params=pltpu.CompilerParams(
            dimension_semantics=("arbitrary", "arbitrary"),
            vmem_limit_bytes=VMEM_LIMIT),
        name="proj_conv",
    )(x, pre_g, w_in_b, dw_w, dw_b, ln_g, ln_b, pw_w_b, pw_b)


def _attn_kernel(qt_ref, k_ref, vt_ref, gd_ref, lq1_ref, lk1_ref, lq2_ref, lk2_ref, sg_ref,
                 o_ref, acc_ref, m_ref, mask_ref, t_ref, tmax_ref, kaug_ref, vta_ref, qaug_ref,
                 *, lam_init):
    tq = ATT_TQ
    tk = ATT_TK
    hq, hk = tq // 2, tk // 2
    seq = k_ref.shape[0]
    n_q = seq // tq
    group = pl.program_id(1)
    heads = range(ATT_HEADS)
    f32, bf16 = jnp.float32, jnp.bfloat16

    def head_rows(e):
        return slice(e * DIFF_VDIM, (e + 1) * DIFF_VDIM)

    slopes = []
    for e in heads:
        hv = jnp.full((1, 1), group * ATT_HEADS + e, jnp.int32).astype(f32)
        slopes.append(jnp.exp2(-(8.0 / DIFF_HEADS) * (hv + 1.0)) * LOG2E)

    @pl.when((pl.program_id(0) == 0) & (group == 0))
    def _():
        jj = lax.broadcasted_iota(jnp.int32, (ATT_ITEM, LANES), 0)
        ln = lax.broadcasted_iota(jnp.int32, (ATT_ITEM, LANES), 1)
        feat = jnp.where(ln < SLOPE_PIECES, jj & (POS_RADIX - 1),
                         jnp.where(ln < 2 * SLOPE_PIECES, jj >> POS_SHIFT, 0))
        feat = feat.astype(f32).astype(bf16)
        row = lax.broadcasted_iota(jnp.int32, (VT_PAD, seq), 0)
        ones_row = jnp.where(row == 0, 1.0, 0.0).astype(bf16)
        for e in heads:
            for blk in range(seq // ATT_ITEM):
                kaug_ref[e, blk * ATT_ITEM:(blk + 1) * ATT_ITEM, DIFF_VDIM:] = feat
            vta_ref[e, DIFF_VDIM:, :] = ones_row
        kk = lax.broadcasted_iota(jnp.int32, (tk, tq), 0)
        ii = lax.broadcasted_iota(jnp.int32, (tk, tq), 1)
        mask_ref[...] = jnp.where(kk > ii, -jnp.inf, 0.0).astype(f32)

    for e in heads:
        kaug_ref[e, :, 0:DIFF_VDIM] = k_ref[:, head_rows(e)]
        vta_ref[e, 0:DIFF_VDIM, :] = vt_ref[head_rows(e), :]

    r = lax.broadcasted_iota(jnp.int32, (VT_PAD, tq), 0)
    zhalf = jnp.zeros((DIFF_QKDIM, tq), bf16)
    saug = []
    for e in heads:
        sl = jnp.broadcast_to(slopes[e], (1, tq))
        s_hi = sl.astype(bf16).astype(f32)
        s_mid = (sl - s_hi).astype(bf16).astype(f32)
        s_lo = (sl - s_hi - s_mid).astype(bf16).astype(f32)
        piece = jnp.where((r == 0) | (r == 3), s_hi, jnp.where((r == 1) | (r == 4), s_mid, s_lo))
        srows = jnp.where(r < SLOPE_PIECES, piece,
                          jnp.where(r < 2 * SLOPE_PIECES, piece * POS_RADIX, 0.0)).astype(bf16)
        saug.append(jnp.concatenate([srows, jnp.zeros((LANES - VT_PAD, tq), bf16)], axis=0))

    def query_operands(e, q0):
        qt = qt_ref[head_rows(e), pl.ds(q0, tq)]
        return (jnp.concatenate([qt[0:DIFF_QKDIM], zhalf, saug[e]], axis=0),
                jnp.concatenate([zhalf, qt[DIFF_QKDIM:], saug[e]], axis=0))

    for e in heads:
        own = query_operands(e, 0)
        qaug_ref[e, 0] = own[0]
        qaug_ref[e, 1] = own[1]
    qpos = lax.broadcasted_iota(jnp.int32, (1, tq), 1).astype(f32)
    lam = (jnp.exp(jnp.sum(lq1_ref[...] * lk1_ref[...], axis=-1, keepdims=True))
           - jnp.exp(jnp.sum(lq2_ref[...] * lk2_ref[...], axis=-1, keepdims=True)) + lam_init)

    def tile(qi, carry):
        _tile(qi)
        return carry

    def _tile(qi, has_next=True):
        qaug = [(qaug_ref[e, 0], qaug_ref[e, 1]) for e in heads]
        if has_next:
            qaug_next = [query_operands(e, _aligned((qi + 1) * tq, tq)) for e in heads]

        m_ref[...] = jnp.full(m_ref.shape, -jnp.inf, f32)

        def scores(c, u0, n_units, mask_last, next_tile=False):
            if next_tile and not has_next:
                return
            for e in heads:
                qop = qaug_next[e][c] if next_tile else qaug[e][c]
                mx = None
                for u in range(n_units):
                    k0 = _aligned((u0 + u) * tk, tk)
                    if mask_last and u == n_units - 1:
                        k1 = _aligned((u0 + u) * tk + hk, hk)
                        ta = jnp.dot(kaug_ref[e, pl.ds(k0, hk), :], qop,
                                     preferred_element_type=f32) + mask_ref[0:hk, :]
                        tb = jnp.dot(kaug_ref[e, pl.ds(k1, hk), :], qop[:, hq:],
                                     preferred_element_type=f32) + mask_ref[hk:, hq:]
                        t_ref[e, c, u * tk:u * tk + hk, :] = ta
                        t_ref[e, c, u * tk + hk:(u + 1) * tk, hq:] = tb
                        tm = jnp.maximum(
                            jnp.max(ta, axis=0, keepdims=True),
                            jnp.concatenate([jnp.full((1, hq), -jnp.inf, f32),
                                             jnp.max(tb, axis=0, keepdims=True)], axis=1))
                    else:
                        t = jnp.dot(kaug_ref[e, pl.ds(k0, tk), :], qop, preferred_element_type=f32)
                        t_ref[e, c, u * tk:(u + 1) * tk, :] = t
                        tm = jnp.max(t, axis=0, keepdims=True)
                    mx = tm if mx is None else jnp.maximum(mx, tm)
                tmax_ref[e, c] = mx

        def softmax_pv(par, c, u0, n_units, masked=False):
            k0 = _aligned(u0 * tk, tk)
            nk = n_units * tk
            for e in heads:
                cq = slopes[e] * (jnp.asarray(u0 * tk - qi * tq, jnp.int32).astype(f32) - qpos)
                m_old = m_ref[e, c]
                m_new = jnp.maximum(m_old, tmax_ref[e, c] + cq)
                alpha = jnp.exp2(m_old - m_new)
                shift = m_new - cq
                upd = alpha * acc_ref[e, par, c]
                nf = nk - hk if masked else nk
                for r0 in range(0, nf, tk):
                    rn = min(tk, nf - r0)
                    kc = _aligned(u0 * tk + r0, rn)
                    p = jnp.exp2(t_ref[e, c, r0:r0 + rn, :] - shift)
                    upd = upd + jnp.dot(vta_ref[e, :, pl.ds(kc, rn)], p.astype(bf16),
                                        preferred_element_type=f32)
                if masked:
                    k1 = _aligned(u0 * tk + nf, hk)
                    pb = jnp.exp2(t_ref[e, c, nf:nk, hq:] - shift[:, hq:])
                    late = jnp.dot(vta_ref[e, :, pl.ds(k1, hk)], pb.astype(bf16), preferred_element_type=f32)
                    upd = jnp.concatenate([upd[:, 0:hq], upd[:, hq:] + late], axis=1)
                acc_ref[e, par, c] = upd
                m_ref[e, c] = m_new

        units = ATT_ITEM // tk
        nb = qi // units
        tail0 = nb * units

        def run(n_tail):
            par = (n_tail - 1) % 2

            def clear_sums():
                for e in heads:
                    acc_ref[e, par] = jnp.zeros(acc_ref.shape[2:], f32)

            @pl.when((qi % units == n_tail - 1) & (nb == 0))
            def _():
                first_tile = n_tail == 1
                clear_sums()
                if first_tile:
                    scores(0, 0, n_tail, True)
                scores(1, 0, n_tail, True)
                if not first_tile:
                    finish(qi - 1, 1 - par)
                softmax_pv(par, 0, 0, n_tail, masked=True)
                if n_tail < units:
                    scores(0, 0, n_tail + 1, True, next_tile=True)
                else:
                    scores(0, 0, units, False, next_tile=True)
                softmax_pv(par, 1, 0, n_tail, masked=True)

            @pl.when((qi % units == n_tail - 1) & (nb > 0))
            def _():
                clear_sums()

                def body(j, carry):
                    scores(1, j * units, units, False)
                    softmax_pv(par, 0, j * units, units)
                    scores(0, (j + 1) * units, units, False)
                    softmax_pv(par, 1, j * units, units)
                    return carry

                lax.fori_loop(0, nb - 1, body, 0)
                last = tail0 - units
                scores(1, last, units, False)
                finish(qi - 1, 1 - par)
                softmax_pv(par, 0, last, units)
                scores(0, tail0, n_tail, True)
                softmax_pv(par, 1, last, units)
                scores(1, tail0, n_tail, True)
                softmax_pv(par, 0, tail0, n_tail, masked=True)
                scores(0, 0, units, False, next_tile=True)
                softmax_pv(par, 1, tail0, n_tail, masked=True)

        for n_tail in range(1, units + 1):
            run(n_tail)

        if has_next:
            for e in heads:
                qaug_ref[e, 0] = qaug_next[e][0]
                qaug_ref[e, 1] = qaug_next[e][1]

    def finish(q_tile, par):
        q0 = _aligned(q_tile * tq, tq)
        for e in heads:
            a0 = acc_ref[e, par, 0]
            a1 = acc_ref[e, par, 1]
            o = (a0[0:DIFF_VDIM] / a0[DIFF_VDIM:DIFF_VDIM + 1]
                 - lam * (a1[0:DIFF_VDIM] / a1[DIFF_VDIM:DIFF_VDIM + 1]))
            ms = jnp.mean(o * o, axis=0, keepdims=True)
            on = (o * lax.rsqrt(ms + RMS_EPS)).T
            y = on * (sg_ref[...] * (1.0 - lam_init)) * gd_ref[pl.ds(q0, tq), head_rows(e)]
            o_ref[pl.ds(q0, tq), head_rows(e)] = y.astype(bf16)

    lax.fori_loop(0, n_q - 1, tile, 0)
    _tile(n_q - 1, has_next=False)
    finish(n_q - 1, (n_q - 1) % 2)


def _attention(qt, k, vt, gd, lq1, lk1, lq2, lk2, sg, lam_init):
    B, S, _ = k.shape
    assert ATT_TQ == ATT_TK and ATT_ITEM % ATT_TK == 0 and S % ATT_ITEM == 0
    assert ATT_ITEM <= POS_RADIX * POS_RADIX
    assert DIFF_HEADS % ATT_HEADS == 0
    tq = ATT_TQ
    width = ATT_HEADS * DIFF_VDIM
    grid = (B, DIFF_HEADS // ATT_HEADS)
    vec = lambda b, g: (0, 0)
    rows = lambda b, g: (b, 0, g)
    cols = lambda b, g: (b, g, 0)
    in_specs = [
        pl.BlockSpec((None, width, S), cols),
        pl.BlockSpec((None, S, width), rows),
        pl.BlockSpec((None, width, S), cols),
        pl.BlockSpec((None, S, width), rows),
        pl.BlockSpec((1, DIFF_QKDIM), vec),
        pl.BlockSpec((1, DIFF_QKDIM), vec),
        pl.BlockSpec((1, DIFF_QKDIM), vec),
        pl.BlockSpec((1, DIFF_QKDIM), vec),
        pl.BlockSpec((1, DIFF_VDIM), vec),
    ]
    n_h = ATT_HEADS
    return pl.pallas_call(
        functools.partial(_attn_kernel, lam_init=lam_init),
        grid=grid,
        in_specs=in_specs,
        out_specs=pl.BlockSpec((None, S, width), rows),
        out_shape=jax.ShapeDtypeStruct((B, S, DIFF_WIDTH), jnp.bfloat16),
        scratch_shapes=[
            pltpu.VMEM((n_h, 2, 2, DIFF_VDIM + VT_PAD, tq), jnp.float32),
            pltpu.VMEM((n_h, 2, 1, tq), jnp.float32),
            pltpu.VMEM((ATT_TK, tq), jnp.float32),
            pltpu.VMEM((n_h, 2, ATT_ITEM, tq), jnp.float32),
            pltpu.VMEM((n_h, 2, 1, tq), jnp.float32),
            pltpu.VMEM((n_h, S, 2 * LANES), jnp.bfloat16),
            pltpu.VMEM((n_h, DIFF_VDIM + VT_PAD, S), jnp.bfloat16),
            pltpu.VMEM((n_h, 2, 2 * LANES, tq), jnp.bfloat16),
        ],
        compiler_params=pltpu.CompilerParams(
            dimension_semantics=("arbitrary", "arbitrary"),
            vmem_limit_bytes=VMEM_LIMIT),
        name="diff_attn",
    )(qt, k, vt, gd, lq1, lk1, lq2, lk2, sg)


def _out_kernel(x_ref, yc_ref, yd_ref, w_ref, g_ref, o_ref):
    y = (jnp.dot(yc_ref[...], w_ref[0:CONV_WIDTH, :], preferred_element_type=jnp.float32)
         + jnp.dot(yd_ref[...], w_ref[CONV_WIDTH:, :], preferred_element_type=jnp.float32))
    ms = jnp.mean(y * y, axis=-1, keepdims=True)
    o_ref[...] = x_ref[...] + y * lax.rsqrt(ms + RMS_EPS) * g_ref[...]


def _out_proj(x, yc, yd, w_out_b, post_g):
    B, S, D = x.shape
    ts = OUT_TILE
    assert S % ts == 0
    row = lambda b, s: (b, s, 0)
    const2 = lambda b, s: (0, 0)
    return pl.pallas_call(
        _out_kernel,
        grid=(B, S // ts),
        in_specs=[
            pl.BlockSpec((None, ts, D), row),
            pl.BlockSpec((None, ts, CONV_WIDTH), row),
            pl.BlockSpec((None, ts, DIFF_WIDTH), row),
            pl.BlockSpec(w_out_b.shape, const2),
            pl.BlockSpec((1, D), const2),
        ],
        out_specs=pl.BlockSpec((None, ts, D), row),
        out_shape=jax.ShapeDtypeStruct((B, S, D), jnp.float32),
        compiler_params=pltpu.CompilerParams(
            dimension_semantics=("arbitrary", "arbitrary"),
            vmem_limit_bytes=VMEM_LIMIT),
        name="out_proj",
    )(x, yc, yd, w_out_b, post_g)


def kernel(x, pre_norm_g, w_in, conv_dw_w, conv_dw_b, conv_ln_g, conv_ln_b, conv_pw_w, conv_pw_b,
           lambda_q1, lambda_k1, lambda_q2, lambda_k2, diff_subln_g, w_out, post_norm_g):
    depth = pre_norm_g.shape[0]
    h = x
    for i in range(depth):
        w_in_b = w_in[i].astype(jnp.bfloat16)
        yc, qt, k, vt, gd = _proj_conv(
            h, pre_norm_g[i][None], w_in_b, conv_dw_w[i], conv_dw_b[i][None],
            conv_ln_g[i][None], conv_ln_b[i][None], conv_pw_w[i].astype(jnp.bfloat16), conv_pw_b[i][None])
        yd = _attention(qt, k, vt, gd, lambda_q1[i][None], lambda_k1[i][None], lambda_q2[i][None],
                        lambda_k2[i][None], diff_subln_g[i][None], _lambda_init(i))
        h = _out_proj(h, yc, yd, w_out[i].astype(jnp.bfloat16), post_norm_g[i][None])
    return h
```
